```python
import jax, jax.numpy as jnp
from jax import lax
import numpy as np


D_MODEL = 2048
BATCH = 4
SEQ = 2048
DEPTH = 2
DEC_BATCH = 128
DEC_SEQ = 1
PAST_LEN = 16384
PAGE_SIZE = 128

N_MIXERS = 2
N_RG_LAYERS = (DEPTH + 1) // 2
N_GLA_LAYERS = DEPTH // 2
D_FF = 5632
RMS_EPS = 1e-6
D_RNN = D_MODEL
RG_BLOCKS = 8
RG_BLOCK_W = D_RNN // RG_BLOCKS
CONV_W = 4
RG_C = 8.0
GLA_HEADS = 4
GLA_DK_TOTAL = D_MODEL // 2
GLA_DV_TOTAL = D_MODEL
GLA_DK = GLA_DK_TOTAL // GLA_HEADS
GLA_DV = GLA_DV_TOTAL // GLA_HEADS
GLA_GATE_RANK = 16
GLA_GATE_NORM = 16.0
GLA_CHUNK = 64

kernel_name = "hybrid_rglru_gla_macaron_step"


def rmsnorm(x, g):
    xf = x.astype(jnp.float32)
    y = xf * lax.rsqrt(jnp.mean(xf * xf, axis=-1, keepdims=True) + RMS_EPS)
    return (y * g.astype(jnp.float32)).astype(x.dtype)


def swiglu(x, w1, w3, w2):
    return (jax.nn.silu(x @ w1) * (x @ w3)) @ w2


def causal_conv(xb, buf, w, bias):
    T = xb.shape[1]
    xp = jnp.concatenate([buf.astype(xb.dtype), xb], axis=1)
    out = bias
    for j in range(CONV_W):
        out = out + xp[:, j:j + T] * w[j]
    new_buf = xp[:, xp.shape[1] - (CONV_W - 1):]
    return out, new_buf


def block_diag(x, w):
    B, T, _ = x.shape
    xr = x.reshape(B, T, RG_BLOCKS, RG_BLOCK_W)
    return jnp.einsum('btnc,ncd->btnd', xr, w).reshape(B, T, D_RNN)


def rglru(xc, h0, w_a, b_a, w_i, b_i, lam):
    xf = xc.astype(jnp.float32)
    r = jax.nn.sigmoid(block_diag(xc, w_a).astype(jnp.float32) + b_a.astype(jnp.float32))
    i = jax.nn.sigmoid(block_diag(xc, w_i).astype(jnp.float32) + b_i.astype(jnp.float32))
    log_a = -RG_C * r * jax.nn.softplus(-lam.astype(jnp.float32))
    a = jnp.exp(log_a)
    b = jnp.sqrt(-jnp.expm1(2.0 * log_a)) * (i * xf)
    b = b.at[:, 0].add(a[:, 0] * h0.astype(jnp.float32))

    def combine(p, q):
        return (p[0] * q[0], q[0] * p[1] + q[1])

    _, h = lax.associative_scan(combine, (a, b), axis=1)
    return h, h[:, -1]


def rg_block(x, conv_buf, h0, w_y, w_x, conv_w, conv_b, w_a, b_a, w_i, b_i, lam, w_o):
    gate = jax.nn.gelu(x @ w_y, approximate=True)
    xc, new_buf = causal_conv(x @ w_x, conv_buf, conv_w, conv_b)
    h, h_last = rglru(xc, h0, w_a, b_a, w_i, b_i, lam)
    out = (h.astype(x.dtype) * gate) @ w_o
    return out, h_last, new_buf


def _pad_t(a, pad):
    return jnp.pad(a, ((0, 0), (0, pad), (0, 0), (0, 0)))


def gla_chunked(q, k, v, g, S0):
    B, T, H, _ = q.shape
    C = min(GLA_CHUNK, T)
    pad = (-T) % C
    q, k, v, g = (_pad_t(t, pad) for t in (q, k, v, g))
    n = (T + pad) // C

    def to_chunks(a):
        return a.reshape(B, n, C, H, a.shape[-1]).transpose(1, 0, 3, 2, 4)

    qc, kc, vc, gc = (to_chunks(t) for t in (q, k, v, g))
    mask = jnp.tril(jnp.ones((C, C), dtype=bool))

    def step(S, inp):
        qb, kb, vb, gb = inp
        bcum = jnp.cumsum(gb, axis=-2)
        b_last = bcum[..., -1:, :]
        q_i = qb * jnp.exp(bcum)
        k_i = kb * jnp.exp(-bcum)
        k_e = kb * jnp.exp(b_last - bcum)
        scores = jnp.where(mask, jnp.einsum('bhtk,bhsk->bhts', q_i, k_i), 0.0)
        o = jnp.einsum('bhtk,bhkv->bhtv', q_i, S) + jnp.einsum('bhts,bhsv->bhtv', scores, vb)
        S_new = jnp.exp(b_last[..., 0, :])[..., None] * S + jnp.einsum('bhsk,bhsv->bhkv', k_e, vb)
        return S_new, o

    S_fin, o = lax.scan(step, S0, (qc, kc, vc, gc))
    o = o.transpose(1, 0, 3, 2, 4).reshape(B, n * C, H, GLA_DV)[:, :T]
    return o, S_fin


def gla_block(x, S0, w_q, w_k, w_v, w_g1, w_g2, b_g, w_r, onorm, w_o):
    B, T, _ = x.shape
    f32 = jnp.float32
    q = (x @ w_q).astype(f32).reshape(B, T, GLA_HEADS, GLA_DK) * (GLA_DK ** -0.5)
    k = (x @ w_k).astype(f32).reshape(B, T, GLA_HEADS, GLA_DK)
    v = (x @ w_v).astype(f32).reshape(B, T, GLA_HEADS, GLA_DV)
    g_logit = ((x @ w_g1) @ w_g2 + b_g).astype(f32)
    g = (jax.nn.log_sigmoid(g_logit) / GLA_GATE_NORM).reshape(B, T, GLA_HEADS, GLA_DK)
    o, S_fin = gla_chunked(q, k, v, g, S0.astype(f32))
    o = rmsnorm(o, onorm).reshape(B, T, GLA_DV_TOTAL).astype(x.dtype)
    out = (o * jax.nn.silu(x @ w_r)) @ w_o
    return out, S_fin


def setup_inputs(seed: int = 0) -> dict:
    key = jax.random.key(seed)
    ks = jax.random.split(key, 48)
    kit = iter([ks[i] for i in range(48)])
    f32 = jnp.float32

    def w(shape, fan_in):
        return jax.random.normal(next(kit), shape, f32) * fan_in ** -0.5

    def gain(shape):
        return 1.0 + 0.05 * jax.random.normal(next(kit), shape, f32)

    def bias(shape):
        return 0.02 * jax.random.normal(next(kit), shape, f32)

    d = {}
    d['x_prompt'] = jax.random.normal(next(kit), (BATCH, SEQ, D_MODEL), f32)
    d['x_sample'] = jax.random.normal(next(kit), (DEC_BATCH, DEC_SEQ, D_MODEL), f32)
    d['state_rglru_h'] = jax.random.normal(next(kit), (N_RG_LAYERS, DEC_BATCH, D_RNN), f32)
    d['state_rglru_conv'] = jax.random.normal(next(kit), (N_RG_LAYERS, DEC_BATCH, CONV_W - 1, D_RNN), f32)
    d['state_gla_S'] = jax.random.normal(next(kit), (N_GLA_LAYERS, DEC_BATCH, GLA_HEADS, GLA_DK, GLA_DV), f32)
    d['ln_ffn1'] = gain((DEPTH, D_MODEL))
    d['ffn1_w1'] = w((DEPTH, D_MODEL, D_FF), D_MODEL)
    d['ffn1_w3'] = w((DEPTH, D_MODEL, D_FF), D_MODEL)
    d['ffn1_w2'] = w((DEPTH, D_FF, D_MODEL), D_FF)
    d['ln_mix'] = gain((DEPTH, D_MODEL))
    d['ln_ffn2'] = gain((DEPTH, D_MODEL))
    d['ffn2_w1'] = w((DEPTH, D_MODEL, D_FF), D_MODEL)
    d['ffn2_w3'] = w((DEPTH, D_MODEL, D_FF), D_MODEL)
    d['ffn2_w2'] = w((DEPTH, D_FF, D_MODEL), D_FF)
    d['rg_w_y'] = w((N_RG_LAYERS, D_MODEL, D_RNN), D_MODEL)
    d['rg_w_x'] = w((N_RG_LAYERS, D_MODEL, D_RNN), D_MODEL)
    d['rg_conv_w'] = w((N_RG_LAYERS, CONV_W, D_RNN), CONV_W)
    d['rg_conv_b'] = bias((N_RG_LAYERS, D_RNN))
    d['rg_w_a'] = w((N_RG_LAYERS, RG_BLOCKS, RG_BLOCK_W, RG_BLOCK_W), RG_BLOCK_W)
    d['rg_b_a'] = bias((N_RG_LAYERS, D_RNN))
    d['rg_w_i'] = w((N_RG_LAYERS, RG_BLOCKS, RG_BLOCK_W, RG_BLOCK_W), RG_BLOCK_W)
    d['rg_b_i'] = bias((N_RG_LAYERS, D_RNN))
    a0 = jax.random.uniform(next(kit), (N_RG_LAYERS, D_RNN), f32, minval=0.9, maxval=0.999)
    d['rg_lambda'] = jnp.log(a0) - jnp.log1p(-a0)
    d['rg_w_o'] = w((N_RG_LAYERS, D_RNN, D_MODEL), D_RNN)
    d['gla_w_q'] = w((N_GLA_LAYERS, D_MODEL, GLA_DK_TOTAL), D_MODEL)
    d['gla_w_k'] = w((N_GLA_LAYERS, D_MODEL, GLA_DK_TOTAL), D_MODEL)
    d['gla_w_v'] = w((N_GLA_LAYERS, D_MODEL, GLA_DV_TOTAL), D_MODEL)
    d['gla_w_g1'] = w((N_GLA_LAYERS, D_MODEL, GLA_GATE_RANK), D_MODEL)
    d['gla_w_g2'] = w((N_GLA_LAYERS, GLA_GATE_RANK, GLA_DK_TOTAL), GLA_GATE_RANK)
    d['gla_b_g'] = bias((N_GLA_LAYERS, GLA_DK_TOTAL))
    d['gla_w_r'] = w((N_GLA_LAYERS, D_MODEL, GLA_DV_TOTAL), D_MODEL)
    d['gla_onorm'] = gain((N_GLA_LAYERS, GLA_DV))
    d['gla_w_o'] = w((N_GLA_LAYERS, GLA_DV_TOTAL, D_MODEL), GLA_DV_TOTAL)
    d['ln_final'] = gain((D_MODEL,))
    return d


def reference(x_prompt, x_sample, state_rglru_h, state_rglru_conv, state_gla_S,
              ln_ffn1, ffn1_w1, ffn1_w3, ffn1_w2, ln_mix, ln_ffn2, ffn2_w1, ffn2_w3, ffn2_w2,
              rg_w_y, rg_w_x, rg_conv_w, rg_conv_b, rg_w_a, rg_b_a, rg_w_i, rg_b_i, rg_lambda, rg_w_o,
              gla_w_q, gla_w_k, gla_w_v, gla_w_g1, gla_w_g2, gla_b_g, gla_w_r, gla_onorm, gla_w_o,
              ln_final):

    def run(x, rg_h, rg_conv, gla_S):
        new_h, new_conv, new_S = [], [], []
        for i in range(DEPTH):
            x = x + 0.5 * swiglu(rmsnorm(x, ln_ffn1[i]), ffn1_w1[i], ffn1_w3[i], ffn1_w2[i])
            u = rmsnorm(x, ln_mix[i])
            j = i // N_MIXERS
            if i % N_MIXERS == 0:
                m, h_last, c_new = rg_block(u, rg_conv[j], rg_h[j], rg_w_y[j], rg_w_x[j], rg_conv_w[j],
                                            rg_conv_b[j], rg_w_a[j], rg_b_a[j], rg_w_i[j], rg_b_i[j],
                                            rg_lambda[j], rg_w_o[j])
                new_h.append(h_last.astype(rg_h.dtype))
                new_conv.append(c_new.astype(rg_conv.dtype))
            else:
                m, S_fin = gla_block(u, gla_S[j], gla_w_q[j], gla_w_k[j], gla_w_v[j], gla_w_g1[j],
                                     gla_w_g2[j], gla_b_g[j], gla_w_r[j], gla_onorm[j], gla_w_o[j])
                new_S.append(S_fin.astype(gla_S.dtype))
            x = x + m
            x = x + 0.5 * swiglu(rmsnorm(x, ln_ffn2[i]), ffn2_w1[i], ffn2_w3[i], ffn2_w2[i])
        return rmsnorm(x, ln_final), jnp.stack(new_h), jnp.stack(new_conv), jnp.stack(new_S)

    Bp = x_prompt.shape[0]
    dt = x_prompt.dtype
    h0_p = jnp.zeros((N_RG_LAYERS, Bp, D_RNN), dt)
    conv0_p = jnp.zeros((N_RG_LAYERS, Bp, CONV_W - 1, D_RNN), dt)
    S0_p = jnp.zeros((N_GLA_LAYERS, Bp, GLA_HEADS, GLA_DK, GLA_DV), dt)

    y_prompt, p_h, p_conv, p_S = run(x_prompt, h0_p, conv0_p, S0_p)
    y_sample, s_h, s_conv, s_S = run(x_sample, state_rglru_h, state_rglru_conv, state_gla_S)
    return (y_prompt, y_sample, p_h, p_conv, p_S, s_h, s_conv, s_S)
```

```python
import functools

import jax
import jax.numpy as jnp
from jax import lax
from jax.experimental import pallas as pl
from jax.experimental.pallas import tpu as pltpu

F32 = jnp.float32
BF16 = jnp.bfloat16

RMS_EPS = 1e-6
RG_C = 8.0
CONV_W = 4
GLA_GATE_NORM = 16.0
GLA_CHUNK = 64

V7X_VMEM_LIMIT_BYTES = 56 * 1024 * 1024
SUBLANES = 8


def _cparams(*sem):
    return pltpu.CompilerParams(dimension_semantics=sem, vmem_limit_bytes=V7X_VMEM_LIMIT_BYTES)


def _rms_scale(x):
    return x * lax.rsqrt(jnp.mean(x * x, axis=-1, keepdims=True) + RMS_EPS)


def _silu(x):
    return x * jax.nn.sigmoid(x)


def _softplus(x):
    return jnp.maximum(x, 0.0) + jnp.log1p(jnp.exp(-jnp.abs(x)))


def _neg_expm1(z):
    t = jnp.tanh(0.5 * z)
    return -2.0 * t / (1.0 - t)


def _ffn_body(x_ref, g_ref, w1_ref, w3_ref, w2_ref, *rest, final_norm):
    if final_norm:
        gf_ref, o_ref, xn_ref = rest
    else:
        o_ref, xn_ref = rest
    f = pl.program_id(1)

    @pl.when(f == 0)
    def _():
        xn_ref[...] = (_rms_scale(x_ref[...]) * g_ref[...]).astype(BF16)
        o_ref[...] = jnp.zeros_like(o_ref)

    xn = xn_ref[...]
    h1 = jnp.dot(xn, w1_ref[...], preferred_element_type=F32)
    h3 = jnp.dot(xn, w3_ref[...], preferred_element_type=F32)
    h = (_silu(h1) * h3).astype(BF16)
    o_ref[...] += jnp.dot(h, w2_ref[...], preferred_element_type=F32)

    @pl.when(f == pl.num_programs(1) - 1)
    def _():
        y = x_ref[...] + 0.5 * o_ref[...]
        if final_norm:
            y = _rms_scale(y) * gf_ref[...]
        o_ref[...] = y


def _ffn(x, g, w1, w3, w2, g_final=None, *, tm=640, tf=512):
    m, d = x.shape
    f = w1.shape[1]
    final_norm = g_final is not None
    in_specs = [
        pl.BlockSpec((tm, d), lambda i, j: (i, 0)),
        pl.BlockSpec((1, d), lambda i, j: (0, 0)),
        pl.BlockSpec((d, tf), lambda i, j: (0, j)),
        pl.BlockSpec((d, tf), lambda i, j: (0, j)),
        pl.BlockSpec((tf, d), lambda i, j: (j, 0)),
    ]
    args = [x, g.reshape(1, d), w1, w3, w2]
    if final_norm:
        in_specs.append(pl.BlockSpec((1, d), lambda i, j: (0, 0)))
        args.append(g_final.reshape(1, d))
    return pl.pallas_call(
        functools.partial(_ffn_body, final_norm=final_norm),
        grid=(m // tm, f // tf),
        in_specs=in_specs,
        out_specs=pl.BlockSpec((tm, d), lambda i, j: (i, 0)),
        out_shape=jax.ShapeDtypeStruct((m, d), F32),
        scratch_shapes=[pltpu.VMEM((tm, d), BF16)],
        compiler_params=_cparams("parallel", "arbitrary"),
        name="ffn",
    )(*args)


def _gelu_tanh(y):
    return jax.nn.gelu(y, approximate=True)


def _proj_body(x_ref, g_ref, w_ref, o_ref, xn_ref, *, ranges):
    j = pl.program_id(1)

    @pl.when(j == 0)
    def _():
        xn_ref[...] = (_rms_scale(x_ref[...]) * g_ref[...]).astype(BF16)

    y = jnp.dot(xn_ref[...], w_ref[...], preferred_element_type=F32)
    for lo, hi, act in ranges:
        @pl.when((j >= lo) & (j < hi))
        def _(act=act):
            o_ref[...] = act(y)


def _norm_proj(x, g, w, col_acts, *, tm=1040, tn=512):
    m, d = x.shape
    n = w.shape[1]
    ranges, lo = [], 0
    for ncols, act in col_acts:
        ranges.append((lo // tn, (lo + ncols) // tn, act))
        lo += ncols
    assert lo == n
    return pl.pallas_call(
        functools.partial(_proj_body, ranges=tuple(ranges)),
        grid=(m // tm, n // tn),
        in_specs=[
            pl.BlockSpec((tm, d), lambda i, j: (i, 0)),
            pl.BlockSpec((1, d), lambda i, j: (0, 0)),
            pl.BlockSpec((d, tn), lambda i, j: (0, j)),
        ],
        out_specs=pl.BlockSpec((tm, tn), lambda i, j: (i, j)),
        out_shape=jax.ShapeDtypeStruct((m, n), F32),
        scratch_shapes=[pltpu.VMEM((tm, d), BF16)],
        compiler_params=_cparams("parallel", "arbitrary"),
        name="norm_proj",
    )(x, g.reshape(1, d), w)


def _gla_gate_body(x_ref, g_ref, w1_ref, w2_ref, b_ref, o_ref):
    xn = (_rms_scale(x_ref[...]) * g_ref[...]).astype(BF16)
    t = jnp.dot(xn, w1_ref[...], preferred_element_type=F32)
    logit = jnp.dot(t.astype(BF16), w2_ref[...], preferred_element_type=F32) + b_ref[...]
    o_ref[...] = -_softplus(-logit) * (1.0 / GLA_GATE_NORM)


def _gla_gate(x, g, w_g1, w_g2, b_g, *, tm=1040):
    m, d = x.shape
    rank, n = w_g2.shape
    return pl.pallas_call(
        _gla_gate_body,
        grid=(m // tm,),
        in_specs=[
            pl.BlockSpec((tm, d), lambda i: (i, 0)),
            pl.BlockSpec((1, d), lambda i: (0, 0)),
            pl.BlockSpec((d, rank), lambda i: (0, 0)),
            pl.BlockSpec((rank, n), lambda i: (0, 0)),
            pl.BlockSpec((1, n), lambda i: (0, 0)),
        ],
        out_specs=pl.BlockSpec((tm, n), lambda i: (i, 0)),
        out_shape=jax.ShapeDtypeStruct((m, n), F32),
        compiler_params=_cparams("parallel"),
        name="gla_gate",
    )(x, g.reshape(1, d), w_g1, w_g2, b_g.reshape(1, n))


def _matmul_res_body(y_ref, w_ref, r_ref, o_ref):
    o_ref[...] = r_ref[...] + jnp.dot(y_ref[...], w_ref[...], preferred_element_type=F32)


def _matmul_res(y, w, res, *, tm=1040, tn=512):
    m, k = y.shape
    n = w.shape[1]
    return pl.pallas_call(
        _matmul_res_body,
        grid=(m // tm, n // tn),
        in_specs=[
            pl.BlockSpec((tm, k), lambda i, j: (i, 0)),
            pl.BlockSpec((k, tn), lambda i, j: (0, j)),
            pl.BlockSpec((tm, tn), lambda i, j: (i, j)),
        ],
        out_specs=pl.BlockSpec((tm, tn), lambda i, j: (i, j)),
        out_shape=jax.ShapeDtypeStruct((m, n), F32),
        compiler_params=_cparams("parallel", "arbitrary"),
        name="matmul_res",
    )(y, w, res)


def _rglru_coeffs(xc, wa_ref, ba_ref, wi_ref, bi_ref, lam_ref, store):
    n_blocks, bw, _ = wa_ref.shape
    for n in range(n_blocks):
        cols = slice(n * bw, (n + 1) * bw)
        xb = xc[:, cols]
        xb16 = xb.astype(BF16)
        r = jax.nn.sigmoid(jnp.dot(xb16, wa_ref[n], preferred_element_type=F32) + ba_ref[:, cols])
        i = jax.nn.sigmoid(jnp.dot(xb16, wi_ref[n], preferred_element_type=F32) + bi_ref[:, cols])
        log_a = -RG_C * r * _softplus(-lam_ref[:, cols])
        a = jnp.exp(log_a)
        b = jnp.sqrt(_neg_expm1(2.0 * log_a)) * (i * xb)
        store(n, a, b)


def _rg_prompt_body(xw_ref, gate_ref, cw_ref, cb_ref, wa_ref, ba_ref, wi_ref, bi_ref, lam_ref,
                    o_ref, hl_ref, cl_ref, ext_ref, a_ref, b_ref, h_ref):
    t = pl.program_id(1)
    tb = xw_ref.shape[0]
    bw = wa_ref.shape[1]
    pad = SUBLANES

    @pl.when(t == 0)
    def _():
        ext_ref[0:pad, :] = jnp.zeros((pad, ext_ref.shape[1]), F32)
        h_ref[...] = jnp.zeros_like(h_ref)

    ext_ref[pad:pad + tb, :] = xw_ref[...]
    xc = cb_ref[...]
    for j in range(CONV_W):
        off = pad - (CONV_W - 1) + j
        xc = xc + ext_ref[off:off + tb, :] * cw_ref[j:j + 1, :]

    def store(n, a, b):
        a_ref[:, n * bw:(n + 1) * bw] = a
        b_ref[:, n * bw:(n + 1) * bw] = b

    _rglru_coeffs(xc, wa_ref, ba_ref, wi_ref, bi_ref, lam_ref, store)

    def step(r, h):
        h = a_ref[pl.ds(r, 1), :] * h + b_ref[pl.ds(r, 1), :]
        b_ref[pl.ds(r, 1), :] = h
        return h

    h = lax.fori_loop(0, tb, step, h_ref[0:1, :], unroll=8)
    h_ref[0:1, :] = h
    o_ref[...] = (b_ref[...] * gate_ref[...]).astype(o_ref.dtype)
    hl_ref[0] = h
    cl_ref[0] = ext_ref[pad + tb - (CONV_W - 1):pad + tb, :]
    ext_ref[0:pad, :] = ext_ref[tb:tb + pad, :]


def _rg_prompt(proj, conv_w, conv_b, w_a, b_a, w_i, b_i, lam, *, batch, seq, tb=256):
    m, d2 = proj.shape
    d = d2 // 2
    nt = seq // tb
    vec = pl.BlockSpec((1, d), lambda b, t: (0, 0))
    wspec = pl.BlockSpec(w_a.shape, lambda b, t: (0, 0, 0))
    return pl.pallas_call(
        _rg_prompt_body,
        grid=(batch, nt),
        in_specs=[
            pl.BlockSpec((tb, d), lambda b, t: (b * nt + t, 1)),
            pl.BlockSpec((tb, d), lambda b, t: (b * nt + t, 0)),
            pl.BlockSpec((CONV_W, d), lambda b, t: (0, 0)),
            vec, wspec, vec, wspec, vec, vec,
        ],
        out_specs=[
            pl.BlockSpec((tb, d), lambda b, t: (b * nt + t, 0)),
            pl.BlockSpec((1, 1, d), lambda b, t: (b, 0, 0)),
            pl.BlockSpec((1, CONV_W - 1, d), lambda b, t: (b, 0, 0)),
        ],
        out_shape=[
            jax.ShapeDtypeStruct((m, d), BF16),
            jax.ShapeDtypeStruct((batch, 1, d), F32),
            jax.ShapeDtypeStruct((batch, CONV_W - 1, d), F32),
        ],
        scratch_shapes=[
            pltpu.VMEM((tb + 2 * SUBLANES, d), F32),
            pltpu.VMEM((tb, d), F32),
            pltpu.VMEM((tb, d), F32),
            pltpu.VMEM((SUBLANES, d), F32),
        ],
        compiler_params=_cparams("parallel", "arbitrary"),
        name="rg_prompt",
    )(proj, proj, conv_w, conv_b.reshape(1, d), w_a, b_a.reshape(1, d), w_i, b_i.reshape(1, d),
      lam.reshape(1, d))


def _rg_sample_body(mix_ref, xw_ref, gate_ref, conv_ref, h0_ref, cw_ref, cb_ref, wa_ref, ba_ref,
                    wi_ref, bi_ref, lam_ref, o_ref, hn_ref, cn_ref, a_ref, b_ref):
    del mix_ref
    d = xw_ref.shape[1]
    bw = wa_ref.shape[1]
    x = xw_ref[...]
    xc = cb_ref[...]
    for j in range(CONV_W - 1):
        xc = xc + conv_ref[:, j * d:(j + 1) * d] * cw_ref[j:j + 1, :]
    xc = xc + x * cw_ref[CONV_W - 1:CONV_W, :]

    def store(n, a, b):
        a_ref[:, n * bw:(n + 1) * bw] = a
        b_ref[:, n * bw:(n + 1) * bw] = b

    _rglru_coeffs(xc, wa_ref, ba_ref, wi_ref, bi_ref, lam_ref, store)
    h = b_ref[...] + a_ref[...] * h0_ref[...]
    hn_ref[...] = h
    o_ref[...] = (h * gate_ref[...]).astype(o_ref.dtype)
    cn_ref[:, 0:(CONV_W - 2) * d] = conv_ref[:, d:(CONV_W - 1) * d]
    cn_ref[:, (CONV_W - 2) * d:] = x


def _rg_sample(mix, proj, conv_state, h0, conv_w, conv_b, w_a, b_a, w_i, b_i, lam, *, row0):
    n, d = h0.shape
    blk = row0 // n
    full2 = lambda shape: pl.BlockSpec(shape, lambda i: (0, 0))
    wspec = pl.BlockSpec(w_a.shape, lambda i: (0, 0, 0))
    kc = (CONV_W - 1) * d
    return pl.pallas_call(
        _rg_sample_body,
        grid=(1,),
        in_specs=[
            pl.BlockSpec(memory_space=pl.ANY),
            pl.BlockSpec((n, d), lambda i: (blk, 1)),
            pl.BlockSpec((n, d), lambda i: (blk, 0)),
            full2((n, kc)), full2((n, d)), full2((CONV_W, d)), full2((1, d)),
            wspec, full2((1, d)), wspec, full2((1, d)), full2((1, d)),
        ],
        out_specs=[
            pl.BlockSpec((n, d), lambda i: (blk, 0)),
            full2((n, d)),
            full2((n, kc)),
        ],
        out_shape=[
            jax.ShapeDtypeStruct(mix.shape, mix.dtype),
            jax.ShapeDtypeStruct((n, d), F32),
            jax.ShapeDtypeStruct((n, kc), F32),
        ],
        scratch_shapes=[pltpu.VMEM((n, d), F32), pltpu.VMEM((n, d), F32)],
        input_output_aliases={0: 0},
        compiler_params=_cparams("arbitrary"),
        name="rg_sample",
    )(mix, proj, proj, conv_state.reshape(n, kc), h0, conv_w, conv_b.reshape(1, d), w_a,
      b_a.reshape(1, d), w_i, b_i.reshape(1, d), lam.reshape(1, d))


def _gla_prompt_body(q_ref, k_ref, v_ref, r_ref, g_ref, on_ref, o_ref, s_ref):
    t = pl.program_id(2)
    tb = q_ref.shape[0]
    c = GLA_CHUNK
    dk = q_ref.shape[1]

    @pl.when(t == 0)
    def _():
        s_ref[...] = jnp.zeros_like(s_ref)

    row = lax.broadcasted_iota(jnp.int32, (c, c), 0)
    col = lax.broadcasted_iota(jnp.int32, (c, c), 1)
    causal = row >= col
    tri = causal.astype(F32)

    def chunk(ci, carry):
        rows = pl.ds(pl.multiple_of(ci * c, c), c)
        g = g_ref[rows, :]
        bcum = jnp.dot(tri, g, preferred_element_type=F32, precision=lax.Precision.HIGHEST)
        b_last = bcum[c - 1:c, :]
        q_i = (q_ref[rows, :] * jnp.exp(bcum)).astype(BF16)
        kk = k_ref[rows, :]
        k_i = (kk * jnp.exp(-bcum)).astype(BF16)
        k_e = (kk * jnp.exp(b_last - bcum)).astype(BF16)
        v = v_ref[rows, :].astype(BF16)
        s = s_ref[0, 0]
        scores = lax.dot_general(q_i, k_i, (((1,), (1,)), ((), ())), preferred_element_type=F32)
        scores = jnp.where(causal, scores, 0.0).astype(BF16)
        o = (jnp.dot(q_i, s.astype(BF16), preferred_element_type=F32)
             + jnp.dot(scores, v, preferred_element_type=F32))
        decay = jnp.exp(jnp.broadcast_to(b_last, (SUBLANES, dk))).T[:, 0:1]
        s_ref[0, 0] = decay * s + lax.dot_general(k_e, v, (((0,), (0,)), ((), ())),
                                                  preferred_element_type=F32)
        o = _rms_scale(o) * on_ref[...]
        o_ref[rows, :] = (o * r_ref[rows, :]).astype(o_ref.dtype)
        return carry

    lax.fori_loop(0, tb // c, chunk, 0)


def _gla_prompt(qkvr, g, onorm, *, batch, seq, heads, dk, dv, tb=512):
    m = qkvr.shape[0]
    nt = seq // tb
    kq = heads
    vq = (2 * heads * dk) // dv
    rq = vq + heads
    return pl.pallas_call(
        _gla_prompt_body,
        grid=(batch, heads, nt),
        in_specs=[
            pl.BlockSpec((tb, dk), lambda b, h, t: (b * nt + t, h)),
            pl.BlockSpec((tb, dk), lambda b, h, t: (b * nt + t, kq + h)),
            pl.BlockSpec((tb, dv), lambda b, h, t: (b * nt + t, vq + h)),
            pl.BlockSpec((tb, dv), lambda b, h, t: (b * nt + t, rq + h)),
            pl.BlockSpec((tb, dk), lambda b, h, t: (b * nt + t, h)),
            pl.BlockSpec((1, dv), lambda b, h, t: (0, 0)),
        ],
        out_specs=[
            pl.BlockSpec((tb, dv), lambda b, h, t: (b * nt + t, h)),
            pl.BlockSpec((1, 1, dk, dv), lambda b, h, t: (b, h, 0, 0)),
        ],
        out_shape=[
            jax.ShapeDtypeStruct((m, heads * dv), BF16),
            jax.ShapeDtypeStruct((batch, heads, dk, dv), F32),
        ],
        compiler_params=_cparams("parallel", "parallel", "arbitrary"),
        name="gla_prompt",
    )(qkvr, qkvr, qkvr, qkvr, g, onorm.reshape(1, dv))


def _gla_sample_body(mix_ref, q_ref, k_ref, v_ref, r_ref, g_ref, s0_ref, on_ref, o_ref, s_ref):
    del mix_ref
    bb = q_ref.shape[0]
    alpha_t = jnp.exp(g_ref[...]).T
    k_t = k_ref[...].T
    q_t = q_ref[...].T
    v = v_ref[...]
    rows = []
    for b in range(bb):
        s_new = alpha_t[:, b:b + 1] * s0_ref[b, 0] + k_t[:, b:b + 1] * v[b:b + 1, :]
        s_ref[b, 0] = s_new
        rows.append(jnp.sum(q_t[:, b:b + 1] * s_new, axis=0, keepdims=True))
    o = jnp.concatenate(rows, axis=0)
    o = _rms_scale(o) * on_ref[...]
    o_ref[...] = (o * r_ref[...]).astype(o_ref.dtype)


def _gla_sample(mix, qkvr, g, s0, onorm, *, row0, heads, dk, dv, bb=16):
    n = s0.shape[0]
    blk0 = row0 // bb
    kq = heads
    vq = (2 * heads * dk) // dv
    rq = vq + heads
    return pl.pallas_call(
        _gla_sample_body,
        grid=(n // bb, heads),
        in_specs=[
            pl.BlockSpec(memory_space=pl.ANY),
            pl.BlockSpec((bb, dk), lambda s, h: (blk0 + s, h)),
            pl.BlockSpec((bb, dk), lambda s, h: (blk0 + s, kq + h)),
            pl.BlockSpec((bb, dv), lambda s, h: (blk0 + s, vq + h)),
            pl.BlockSpec((bb, dv), lambda s, h: (blk0 + s, rq + h)),
            pl.BlockSpec((bb, dk), lambda s, h: (blk0 + s, h)),
            pl.BlockSpec((bb, 1, dk, dv), lambda s, h: (s, h, 0, 0)),
            pl.BlockSpec((1, dv), lambda s, h: (0, 0)),
        ],
        out_specs=[
            pl.BlockSpec((bb, dv), lambda s, h: (blk0 + s, h)),
            pl.BlockSpec((bb, 1, dk, dv), lambda s, h: (s, h, 0, 0)),
        ],
        out_shape=[
            jax.ShapeDtypeStruct(mix.shape, mix.dtype),
            jax.ShapeDtypeStruct(s0.shape, F32),
        ],
        input_output_aliases={0: 0},
        compiler_params=_cparams("parallel", "parallel"),
        name="gla_sample",
    )(mix, qkvr, qkvr, qkvr, qkvr, g, s0, onorm.reshape(1, dv))


def kernel(x_prompt, x_sample, state_rglru_h, state_rglru_conv, state_gla_S, ln_ffn1, ffn1_w1, ffn1_w3, ffn1_w2, ln_mix, ln_ffn2, ffn2_w1, ffn2_w3, ffn2_w2, rg_w_y, rg_w_x, rg_conv_w, rg_conv_b, rg_w_a, rg_b_a, rg_w_i, rg_b_i, rg_lambda, rg_w_o, gla_w_q, gla_w_k, gla_w_v, gla_w_g1, gla_w_g2, gla_b_g, gla_w_r, gla_onorm, gla_w_o, ln_final):
    batch, seq, d = x_prompt.shape
    n_dec = x_sample.shape[0]
    depth = ln_ffn1.shape[0]
    heads, dk, dv = state_gla_S.shape[2:]
    row0 = batch * seq
    bf = lambda w: w.astype(BF16)

    x = jnp.concatenate([x_prompt.reshape(row0, d), x_sample.reshape(n_dec, d)], axis=0)

    p_h, p_conv, p_s, s_h, s_conv, s_s = [], [], [], [], [], []
    ident = lambda y: y
    for i in range(depth):
        j = i // 2
        x = _ffn(x, ln_ffn1[i], bf(ffn1_w1[i]), bf(ffn1_w3[i]), bf(ffn1_w2[i]))
        if i % 2 == 0:
            w_cat = jnp.concatenate([bf(rg_w_y[j]), bf(rg_w_x[j])], axis=1)
            proj = _norm_proj(x, ln_mix[i], w_cat, [(d, _gelu_tanh), (d, ident)])
            rg_args = (rg_conv_w[j], rg_conv_b[j], bf(rg_w_a[j]), rg_b_a[j], bf(rg_w_i[j]),
                       rg_b_i[j], rg_lambda[j])
            mix, hl, cl = _rg_prompt(proj, *rg_args, batch=batch, seq=seq)
            mix, hn, cn = _rg_sample(mix, proj, state_rglru_conv[j], state_rglru_h[j], *rg_args,
                                     row0=row0)
            p_h.append(hl.reshape(batch, d))
            p_conv.append(cl)
            s_h.append(hn)
            s_conv.append(cn.reshape(n_dec, CONV_W - 1, d))
            x = _matmul_res(mix, bf(rg_w_o[j]), x)
        else:
            w_cat = jnp.concatenate([bf(gla_w_q[j]), bf(gla_w_k[j]), bf(gla_w_v[j]), bf(gla_w_r[j])],
                                    axis=1)
            scale = dk ** -0.5
            qkvr = _norm_proj(x, ln_mix[i], w_cat,
                              [(heads * dk, lambda y: y * scale), (heads * dk, ident),
                               (heads * dv, ident), (heads * dv, _silu)])
            rank = gla_w_g1.shape[2]
            lanes = 128
            w_g1 = jnp.pad(bf(gla_w_g1[j]), ((0, 0), (0, lanes - rank)))
            w_g2 = jnp.pad(bf(gla_w_g2[j]), ((0, lanes - rank), (0, 0)))
            g = _gla_gate(x, ln_mix[i], w_g1, w_g2, gla_b_g[j])
            mix, s_fin = _gla_prompt(qkvr, g, gla_onorm[j], batch=batch, seq=seq, heads=heads,
                                     dk=dk, dv=dv)
            mix, s_new = _gla_sample(mix, qkvr, g, state_gla_S[j], gla_onorm[j], row0=row0,
                                     heads=heads, dk=dk, dv=dv)
            p_s.append(s_fin)
            s_s.append(s_new)
            x = _matmul_res(mix, bf(gla_w_o[j]), x)
        g_final = ln_final if i == depth - 1 else None
        x = _ffn(x, ln_ffn2[i], bf(ffn2_w1[i]), bf(ffn2_w3[i]), bf(ffn2_w2[i]), g_final)

    y_prompt = x[:row0].reshape(batch, seq, d)
    y_sample = x[row0:].reshape(n_dec, 1, d)
    return (y_prompt, y_sample, jnp.stack(p_h), jnp.stack(p_conv), jnp.stack(p_s),
            jnp.stack(s_h), jnp.stack(s_conv), jnp.stack(s_s))
```

```python
import functools

import jax
import jax.numpy as jnp
from jax import lax
from jax.experimental import pallas as pl
from jax.experimental.pallas import tpu as pltpu

F32 = jnp.float32
BF16 = jnp.bfloat16

RMS_EPS = 1e-6
RG_C = 8.0
CONV_W = 4
GLA_GATE_NORM = 16.0
GLA_CHUNK = 64

V7X_VMEM_LIMIT_BYTES = 60 * 1024 * 1024
SUBLANES = 8
LANES = 128
BF16_SUBLANES = 16
N_TOKEN_TILES = 8


def _cparams(*sem):
    return pltpu.CompilerParams(dimension_semantics=sem, vmem_limit_bytes=V7X_VMEM_LIMIT_BYTES)


def _rms_scale(x):
    return x * lax.rsqrt(jnp.mean(x * x, axis=-1, keepdims=True) + RMS_EPS)


def _silu(x):
    return x * jax.nn.sigmoid(x)


def _gelu_tanh(y):
    return jax.nn.gelu(y, approximate=True)


def _identity(y):
    return y


def _softplus(x):
    return jnp.maximum(x, 0.0) + jnp.log1p(jnp.exp(-jnp.abs(x)))


def _neg_expm1(z):
    t = jnp.tanh(0.5 * z)
    return -2.0 * t / (1.0 - t)


def _token_tiles(n_prompt, n_sample):
    tp, ts = n_prompt // N_TOKEN_TILES, n_sample // N_TOKEN_TILES
    assert tp * N_TOKEN_TILES == n_prompt and ts * N_TOKEN_TILES == n_sample
    assert tp % BF16_SUBLANES == 0 and ts % BF16_SUBLANES == 0 and tp % ts == 0
    return tp, ts


def _ffn_body(*refs, tp, ts, split_in, split_out, mix_norm):
    refs = list(refs)
    x_hbm = [refs.pop(0) for _ in range(2 if split_in else 1)]
    g_ref, w1_ref, w3_ref, w2_ref = (refs.pop(0) for _ in range(4))
    g2_ref = refs.pop(0) if (split_out or mix_norm) else None
    o_refs = [refs.pop(0) for _ in range(2 if split_out else 1)]
    u_hbm = refs.pop(0) if mix_norm else None
    xn_ref, sem = refs
    tm = tp + ts
    i = pl.program_id(0)
    f = pl.program_id(1)
    segs = [(slice(0, tp), o_refs[0]), (slice(tp, tm), o_refs[1])] if split_out else \
           [(slice(0, tm), o_refs[0])]

    def seg_ref(rows):
        for srows, o_ref in segs:
            if srows.start <= rows.start and rows.stop <= srows.stop:
                return o_ref.at[pl.ds(rows.start - srows.start, rows.stop - rows.start)]
        raise AssertionError

    @pl.when(f == 0)
    def _():
        pieces = [slice(0, tp), slice(tp, tm)] if (split_in or split_out) else [slice(0, tm)]
        copies = []
        for k, rows in enumerate(pieces):
            n_rows = rows.stop - rows.start
            if split_in:
                src = x_hbm[k].at[pl.ds(i * n_rows, n_rows)]
            else:
                src = x_hbm[0].at[pl.ds(i * tm + rows.start, n_rows)]
            copies.append(pltpu.make_async_copy(src, seg_ref(rows), sem.at[k]))
        for c in copies:
            c.start()
        for c in copies:
            c.wait()
        for rows, o_ref in segs:
            x = o_ref[...]
            xn_ref[rows, :] = (_rms_scale(x) * g_ref[...]).astype(BF16)
            o_ref[...] = 2.0 * x

    xn = xn_ref[...]
    h1 = jnp.dot(xn, w1_ref[...].astype(BF16), preferred_element_type=F32)
    h3 = jnp.dot(xn, w3_ref[...].astype(BF16), preferred_element_type=F32)
    h = (_silu(h1) * h3).astype(BF16)
    y = jnp.dot(h, w2_ref[...].astype(BF16), preferred_element_type=F32)
    for rows, o_ref in segs:
        o_ref[...] += y[rows]

    @pl.when(f == pl.num_programs(1) - 1)
    def _():
        for rows, o_ref in segs:
            y = 0.5 * o_ref[...]
            if split_out:
                y = _rms_scale(y) * g2_ref[...]
            o_ref[...] = y
            if mix_norm:
                xn_ref[rows, :] = (_rms_scale(y) * g2_ref[...]).astype(BF16)
        if mix_norm:
            cu = pltpu.make_async_copy(xn_ref, u_hbm.at[pl.ds(i * tm, tm)], sem.at[0])
            cu.start()
            cu.wait()


def _ffn(x, g, w1, w3, w2, *, tp, ts, g_mix=None, g_final=None, tf=512):
    split_in = isinstance(x, tuple)
    split_out = g_final is not None
    mix_norm = g_mix is not None
    assert not (split_out and mix_norm)
    xs = list(x) if split_in else [x]
    d, f = w1.shape
    tm = tp + ts
    vec = pl.BlockSpec((1, d), lambda i, j: (0, 0))
    in_specs = [pl.BlockSpec(memory_space=pl.ANY)] * len(xs) + [
        vec,
        pl.BlockSpec((d, tf), lambda i, j: (0, j)),
        pl.BlockSpec((d, tf), lambda i, j: (0, j)),
        pl.BlockSpec((tf, d), lambda i, j: (j, 0)),
    ]
    args = xs + [g.reshape(1, d), w1, w3, w2]
    if split_out or mix_norm:
        in_specs.append(vec)
        args.append((g_final if split_out else g_mix).reshape(1, d))
    if split_out:
        out_specs = [pl.BlockSpec((tp, d), lambda i, j: (i, 0)), pl.BlockSpec((ts, d), lambda i, j: (i, 0))]
        out_shape = [jax.ShapeDtypeStruct((tp * N_TOKEN_TILES, d), F32),
                     jax.ShapeDtypeStruct((ts * N_TOKEN_TILES, d), F32)]
    else:
        out_specs = [pl.BlockSpec((tm, d), lambda i, j: (i, 0))]
        out_shape = [jax.ShapeDtypeStruct((tm * N_TOKEN_TILES, d), F32)]
    if mix_norm:
        out_specs.append(pl.BlockSpec(memory_space=pl.ANY))
        out_shape.append(jax.ShapeDtypeStruct((tm * N_TOKEN_TILES, d), BF16))
    out = pl.pallas_call(
        functools.partial(_ffn_body, tp=tp, ts=ts, split_in=split_in, split_out=split_out,
                          mix_norm=mix_norm),
        grid=(N_TOKEN_TILES, f // tf),
        in_specs=in_specs,
        out_specs=out_specs,
        out_shape=out_shape,
        scratch_shapes=[pltpu.VMEM((tm, d), BF16), pltpu.SemaphoreType.DMA((2,))],
        compiler_params=_cparams("arbitrary", "arbitrary"),
        name="ffn",
    )(*args)
    return out if (split_out or mix_norm) else out[0]


def _matmul_body(u_ref, *refs, ranges, has_res):
    w_refs = refs[:len(ranges)]
    rest = refs[len(ranges):]
    if has_res:
        r_ref, o_ref, wb_ref = rest
    else:
        o_ref, wb_ref = rest
    j = pl.program_id(0)
    i = pl.program_id(1)
    for (blo, bhi, act), w_ref in zip(ranges, w_refs):
        in_range = (j >= blo) & (j < bhi)

        @pl.when(in_range & (i == 0))
        def _(w_ref=w_ref):
            wb_ref[...] = w_ref[...].astype(BF16)

        @pl.when(in_range)
        def _(act=act):
            z = act(jnp.dot(u_ref[...], wb_ref[...], preferred_element_type=F32))
            if has_res:
                z = r_ref[...] + z
            o_ref[...] = z


def _matmul(u, ws, acts, res=None, *, tm, tn=512):
    m, d = u.shape
    ranges, lo = [], 0
    for w, act in zip(ws, acts):
        ranges.append((lo // tn, (lo + w.shape[1]) // tn, act))
        lo += w.shape[1]
    n = lo
    w_specs = [
        pl.BlockSpec((d, tn), lambda j, i, blo=blo, bhi=bhi: (0, jnp.clip(j - blo, 0, bhi - blo - 1)))
        for blo, bhi, _ in ranges
    ]
    io_spec = pl.BlockSpec((tm, tn), lambda j, i: (i, j))
    has_res = res is not None
    return pl.pallas_call(
        functools.partial(_matmul_body, ranges=tuple(ranges), has_res=has_res),
        grid=(n // tn, m // tm),
        in_specs=[pl.BlockSpec((tm, d), lambda j, i: (i, 0))] + w_specs + ([io_spec] if has_res else []),
        out_specs=io_spec,
        out_shape=jax.ShapeDtypeStruct((m, n), F32),
        scratch_shapes=[pltpu.VMEM((d, tn), BF16)],
        compiler_params=_cparams("arbitrary", "arbitrary"),
        name="matmul",
    )(u, *ws, *([res] if has_res else []))


def _gla_gate_body(u_ref, w1_ref, w2_ref, b_ref, o_ref):
    t = jnp.dot(u_ref[...], w1_ref[...].astype(BF16), preferred_element_type=F32)
    logit = jnp.dot(t.astype(BF16), w2_ref[...].astype(BF16), preferred_element_type=F32) + b_ref[...]
    o_ref[...] = -_softplus(-logit) * (1.0 / GLA_GATE_NORM)


def _gla_gate(u, w_g1, w_g2, b_g, *, tm):
    m, d = u.shape
    rank, n = w_g2.shape
    return pl.pallas_call(
        _gla_gate_body,
        grid=(m // tm,),
        in_specs=[
            pl.BlockSpec((tm, d), lambda i: (i, 0)),
            pl.BlockSpec((d, rank), lambda i: (0, 0)),
            pl.BlockSpec((rank, n), lambda i: (0, 0)),
            pl.BlockSpec((1, n), lambda i: (0, 0)),
        ],
        out_specs=pl.BlockSpec((tm, n), lambda i: (i, 0)),
        out_shape=jax.ShapeDtypeStruct((m, n), F32),
        compiler_params=_cparams("arbitrary"),
        name="gla_gate",
    )(u, w_g1, w_g2, b_g.reshape(1, n))


def _prompt_rows_of_tile(o_ref, nt, per_tile, tb):
    blk = pl.program_id(0) * nt + pl.program_id(1)
    k = blk % per_tile
    tp = per_tile * tb

    @pl.when(k == 0)
    def _():
        o_ref[0, tp:, :] = jnp.zeros((o_ref.shape[1] - tp, o_ref.shape[2]), o_ref.dtype)

    return pl.ds(pl.multiple_of(k * tb, tb), tb)


def _rglru_coeffs(xc, wa_ref, ba_ref, wi_ref, bi_ref, lam_ref, store):
    n_blocks, bw, _ = wa_ref.shape
    for n in range(n_blocks):
        cols = slice(n * bw, (n + 1) * bw)
        xb = xc[:, cols]
        xb16 = xb.astype(BF16)
        wa = wa_ref[n].astype(BF16)
        wi = wi_ref[n].astype(BF16)
        r = jax.nn.sigmoid(jnp.dot(xb16, wa, preferred_element_type=F32) + ba_ref[:, cols])
        i = jax.nn.sigmoid(jnp.dot(xb16, wi, preferred_element_type=F32) + bi_ref[:, cols])
        log_a = -RG_C * r * _softplus(-lam_ref[:, cols])
        a = jnp.exp(log_a)
        b = jnp.sqrt(_neg_expm1(2.0 * log_a)) * (i * xb)
        store(n, a, b)


def _rg_prompt_body(xw_ref, gate_ref, cw_ref, cb_ref, wa_ref, ba_ref, wi_ref, bi_ref, lam_ref,
                    o_ref, hl_ref, cl_ref, ext_ref, a_ref, b_ref, h_ref, *, nt, per_tile):
    t = pl.program_id(1)
    tb = xw_ref.shape[1]
    bw = wa_ref.shape[1]
    pad = SUBLANES

    @pl.when(t == 0)
    def _():
        ext_ref[0:pad, :] = jnp.zeros((pad, ext_ref.shape[1]), F32)
        h_ref[...] = jnp.zeros_like(h_ref)

    ext_ref[pad:pad + tb, :] = xw_ref[0]
    xc = cb_ref[...]
    for j in range(CONV_W):
        off = pad - (CONV_W - 1) + j
        xc = xc + ext_ref[off:off + tb, :] * cw_ref[j:j + 1, :]

    def store(n, a, b):
        a_ref[:, n * bw:(n + 1) * bw] = a
        b_ref[:, n * bw:(n + 1) * bw] = b

    _rglru_coeffs(xc, wa_ref, ba_ref, wi_ref, bi_ref, lam_ref, store)

    def step(r, h):
        h = a_ref[pl.ds(r, 1), :] * h + b_ref[pl.ds(r, 1), :]
        b_ref[pl.ds(r, 1), :] = h
        return h

    h = lax.fori_loop(0, tb, step, h_ref[0:1, :], unroll=8)
    h_ref[0:1, :] = h
    out_rows = _prompt_rows_of_tile(o_ref, nt, per_tile, tb)
    o_ref[0, out_rows, :] = (b_ref[...] * gate_ref[0]).astype(o_ref.dtype)
    hl_ref[0] = h
    cl_ref[0] = ext_ref[pad + tb - (CONV_W - 1):pad + tb, :]
    ext_ref[0:pad, :] = ext_ref[tb:tb + pad, :]


def _rg_prompt(proj, conv_w, conv_b, w_a, b_a, w_i, b_i, lam, *, batch, seq, tp, ts, tb=256):
    m, d2 = proj.shape
    d = d2 // 2
    tm = tp + ts
    nt = seq // tb
    per_tile = tp // tb
    proj3 = proj.reshape(N_TOKEN_TILES, tm, d2)

    def tok(col):
        def index(b, t):
            blk = b * nt + t
            return (blk // per_tile, blk % per_tile, col)
        return index

    vec = pl.BlockSpec((1, d), lambda b, t: (0, 0))
    wspec = pl.BlockSpec(w_a.shape, lambda b, t: (0, 0, 0))
    mix, hl, cl = pl.pallas_call(
        functools.partial(_rg_prompt_body, nt=nt, per_tile=per_tile),
        grid=(batch, nt),
        in_specs=[
            pl.BlockSpec((1, tb, d), tok(1)),
            pl.BlockSpec((1, tb, d), tok(0)),
            pl.BlockSpec((CONV_W, d), lambda b, t: (0, 0)),
            vec, wspec, vec, wspec, vec, vec,
        ],
        out_specs=[
            pl.BlockSpec((1, tm, d), lambda b, t: ((b * nt + t) // per_tile, 0, 0)),
            pl.BlockSpec((1, 1, d), lambda b, t: (b, 0, 0)),
            pl.BlockSpec((1, CONV_W - 1, d), lambda b, t: (b, 0, 0)),
        ],
        out_shape=[
            jax.ShapeDtypeStruct((N_TOKEN_TILES, tm, d), BF16),
            jax.ShapeDtypeStruct((batch, 1, d), F32),
            jax.ShapeDtypeStruct((batch, CONV_W - 1, d), F32),
        ],
        scratch_shapes=[
            pltpu.VMEM((tb + SUBLANES, d), F32),
            pltpu.VMEM((tb, d), F32),
            pltpu.VMEM((tb, d), F32),
            pltpu.VMEM((SUBLANES, d), F32),
        ],
        compiler_params=_cparams("arbitrary", "arbitrary"),
        name="rg_prompt",
    )(proj3, proj3, conv_w, conv_b.reshape(1, d), w_a, b_a.reshape(1, d), w_i, b_i.reshape(1, d),
      lam.reshape(1, d))
    return mix, hl, cl


def _rg_sample_body(mix_ref, xw_ref, gate_ref, conv_ref, h0_ref, cw_ref, cb_ref, wa_ref, ba_ref,
                    wi_ref, bi_ref, lam_ref, o_ref, hn_ref, cn_ref, a_ref, b_ref):
    del mix_ref
    n, d = h0_ref.shape
    bw = wa_ref.shape[1]
    x = xw_ref[...].reshape(n, d)
    xc = cb_ref[...]
    for j in range(CONV_W - 1):
        xc = xc + conv_ref[:, j * d:(j + 1) * d] * cw_ref[j:j + 1, :]
    xc = xc + x * cw_ref[CONV_W - 1:CONV_W, :]

    def store(k, a, b):
        a_ref[:, k * bw:(k + 1) * bw] = a
        b_ref[:, k * bw:(k + 1) * bw] = b

    _rglru_coeffs(xc, wa_ref, ba_ref, wi_ref, bi_ref, lam_ref, store)
    h = b_ref[...] + a_ref[...] * h0_ref[...]
    hn_ref[...] = h
    o_ref[...] = (h * gate_ref[...].reshape(n, d)).reshape(o_ref.shape).astype(o_ref.dtype)
    cn_ref[:, 0:(CONV_W - 2) * d] = conv_ref[:, d:(CONV_W - 1) * d]
    cn_ref[:, (CONV_W - 2) * d:] = x


def _rg_sample(mix, proj, conv_state, h0, conv_w, conv_b, w_a, b_a, w_i, b_i, lam, *, tp, ts):
    n, d = h0.shape
    tm = tp + ts
    proj3 = proj.reshape(N_TOKEN_TILES, tm, 2 * d)
    s_blk = tp // ts
    full2 = lambda shape: pl.BlockSpec(shape, lambda i: (0, 0))
    wspec = pl.BlockSpec(w_a.shape, lambda i: (0, 0, 0))
    kc = (CONV_W - 1) * d
    return pl.pallas_call(
        _rg_sample_body,
        grid=(1,),
        in_specs=[
            pl.BlockSpec(memory_space=pl.ANY),
            pl.BlockSpec((N_TOKEN_TILES, ts, d), lambda i: (0, s_blk, 1)),
            pl.BlockSpec((N_TOKEN_TILES, ts, d), lambda i: (0, s_blk, 0)),
            full2((n, kc)), full2((n, d)), full2((CONV_W, d)), full2((1, d)),
            wspec, full2((1, d)), wspec, full2((1, d)), full2((1, d)),
        ],
        out_specs=[
            pl.BlockSpec((N_TOKEN_TILES, ts, d), lambda i: (0, s_blk, 0)),
            full2((n, d)),
            full2((n, kc)),
        ],
        out_shape=[
            jax.ShapeDtypeStruct(mix.shape, mix.dtype),
            jax.ShapeDtypeStruct((n, d), F32),
            jax.ShapeDtypeStruct((n, kc), F32),
        ],
        scratch_shapes=[pltpu.VMEM((n, d), F32), pltpu.VMEM((n, d), F32)],
        input_output_aliases={0: 0},
        compiler_params=_cparams("arbitrary"),
        name="rg_sample",
    )(mix, proj3, proj3, conv_state.reshape(n, kc), h0, conv_w, conv_b.reshape(1, d), w_a,
      b_a.reshape(1, d), w_i, b_i.reshape(1, d), lam.reshape(1, d))


def _gla_prompt_body(q_ref, k_ref, v_ref, r_ref, g_ref, on_ref, o_ref, s_ref, *, heads, nt, per_tile):
    t = pl.program_id(1)
    tb = q_ref.shape[1]
    c = GLA_CHUNK
    dk = q_ref.shape[2] // heads
    dv = v_ref.shape[2] // heads

    @pl.when(t == 0)
    def _():
        s_ref[...] = jnp.zeros_like(s_ref)

    row = lax.broadcasted_iota(jnp.int32, (tb, tb), 0)
    col = lax.broadcasted_iota(jnp.int32, (tb, tb), 1)
    same_chunk_causal = ((row >= col) & ((row ^ col) < c)).astype(BF16)
    g_rest = g_ref[0]
    bcum = jnp.zeros(g_rest.shape, F32)
    for _ in range(3):
        g_part = g_rest.astype(BF16)
        bcum = bcum + jnp.dot(same_chunk_causal, g_part, preferred_element_type=F32)
        g_rest = g_rest - g_part.astype(F32)
    q_i = (q_ref[0] * jnp.exp(bcum)).astype(BF16)
    k_i = (k_ref[0] * jnp.exp(-bcum)).astype(BF16)
    causal = (lax.broadcasted_iota(jnp.int32, (c, c), 0) >= lax.broadcasted_iota(jnp.int32, (c, c), 1))

    out_base = _prompt_rows_of_tile(o_ref, nt, per_tile, tb).start
    for ci in range(tb // c):
        rows = slice(ci * c, (ci + 1) * c)
        out_rows = pl.ds(pl.multiple_of(out_base + ci * c, c), c)
        b_last = bcum[(ci + 1) * c - 1:(ci + 1) * c, :]
        k_e = (k_ref[0, rows, :] * jnp.exp(b_last - bcum[rows, :])).astype(BF16)
        decay_t = jnp.exp(jnp.broadcast_to(b_last, (SUBLANES, heads * dk))).T
        for h in range(heads):
            kcols = slice(h * dk, (h + 1) * dk)
            vcols = slice(h * dv, (h + 1) * dv)
            qs = q_i[rows, kcols]
            v = v_ref[0, rows, vcols].astype(BF16)
            scores = lax.dot_general(qs, k_i[rows, kcols], (((1,), (1,)), ((), ())),
                                     preferred_element_type=F32)
            scores = jnp.where(causal, scores, 0.0).astype(BF16)
            s = s_ref[0, h]
            o = (jnp.dot(qs, s.astype(BF16), preferred_element_type=F32)
                 + jnp.dot(scores, v, preferred_element_type=F32))
            s_ref[0, h] = decay_t[kcols, 0:1] * s + lax.dot_general(
                k_e[:, kcols], v, (((0,), (0,)), ((), ())), preferred_element_type=F32)
            o = _rms_scale(o) * on_ref[...]
            o_ref[0, out_rows, vcols] = (o * r_ref[0, rows, vcols]).astype(o_ref.dtype)


def _gla_prompt(qkvr, g, onorm, *, batch, seq, heads, dk, dv, tp, ts, tb=256):
    tm = tp + ts
    nt = seq // tb
    per_tile = tp // tb
    hk, hv = heads * dk, heads * dv
    assert hv == 2 * hk
    qkvr3 = qkvr.reshape(N_TOKEN_TILES, tm, qkvr.shape[1])
    g3 = g.reshape(N_TOKEN_TILES, tm, hk)

    def tok(col):
        def index(b, t):
            blk = b * nt + t
            return (blk // per_tile, blk % per_tile, col)
        return index

    return pl.pallas_call(
        functools.partial(_gla_prompt_body, heads=heads, nt=nt, per_tile=per_tile),
        grid=(batch, nt),
        in_specs=[
            pl.BlockSpec((1, tb, hk), tok(0)),
            pl.BlockSpec((1, tb, hk), tok(1)),
            pl.BlockSpec((1, tb, hv), tok(1)),
            pl.BlockSpec((1, tb, hv), tok(2)),
            pl.BlockSpec((1, tb, hk), tok(0)),
            pl.BlockSpec((1, dv), lambda b, t: (0, 0)),
        ],
        out_specs=[
            pl.BlockSpec((1, tm, hv), lambda b, t: ((b * nt + t) // per_tile, 0, 0)),
            pl.BlockSpec((1, heads, dk, dv), lambda b, t: (b, 0, 0, 0)),
        ],
        out_shape=[
            jax.ShapeDtypeStruct((N_TOKEN_TILES, tm, hv), BF16),
            jax.ShapeDtypeStruct((batch, heads, dk, dv), F32),
        ],
        compiler_params=_cparams("arbitrary", "arbitrary"),
        name="gla_prompt",
    )(qkvr3, qkvr3, qkvr3, qkvr3, g3, onorm.reshape(1, dv))


def _gla_sample_body(mix_ref, q_ref, k_ref, v_ref, r_ref, g_ref, s0_ref, on_ref, o_ref, s_ref):
    del mix_ref
    bb = q_ref.shape[1]
    alpha_t = jnp.exp(g_ref[0]).T
    k_t = k_ref[0].T
    q_t = q_ref[0].T
    v = v_ref[0]
    rows = []
    for b in range(bb):
        s_new = alpha_t[:, b:b + 1] * s0_ref[b, 0] + k_t[:, b:b + 1] * v[b:b + 1, :]
        s_ref[b, 0] = s_new
        rows.append(jnp.sum(q_t[:, b:b + 1] * s_new, axis=0, keepdims=True))
    o = jnp.concatenate(rows, axis=0)
    o = _rms_scale(o) * on_ref[...]
    o_ref[0] = (o * r_ref[0]).astype(o_ref.dtype)


def _gla_sample(mix, qkvr, g, s0, onorm, *, heads, dk, dv, tp, ts):
    tm = tp + ts
    s_blk = tp // ts
    kq = heads
    vq = (2 * heads * dk) // dv
    rq = vq + heads
    qkvr3 = qkvr.reshape(N_TOKEN_TILES, tm, qkvr.shape[1])
    g3 = g.reshape(N_TOKEN_TILES, tm, heads * dk)
    return pl.pallas_call(
        _gla_sample_body,
        grid=(N_TOKEN_TILES, heads),
        in_specs=[
            pl.BlockSpec(memory_space=pl.ANY),
            pl.BlockSpec((1, ts, dk), lambda s, h: (s, s_blk, h)),
            pl.BlockSpec((1, ts, dk), lambda s, h: (s, s_blk, kq + h)),
            pl.BlockSpec((1, ts, dv), lambda s, h: (s, s_blk, vq + h)),
            pl.BlockSpec((1, ts, dv), lambda s, h: (s, s_blk, rq + h)),
            pl.BlockSpec((1, ts, dk), lambda s, h: (s, s_blk, h)),
            pl.BlockSpec((ts, 1, dk, dv), lambda s, h: (s, h, 0, 0)),
            pl.BlockSpec((1, dv), lambda s, h: (0, 0)),
        ],
        out_specs=[
            pl.BlockSpec((1, ts, dv), lambda s, h: (s, s_blk, h)),
            pl.BlockSpec((ts, 1, dk, dv), lambda s, h: (s, h, 0, 0)),
        ],
        out_shape=[
            jax.ShapeDtypeStruct(mix.shape, mix.dtype),
            jax.ShapeDtypeStruct(s0.shape, F32),
        ],
        input_output_aliases={0: 0},
        compiler_params=_cparams("arbitrary", "arbitrary"),
        name="gla_sample",
    )(mix, qkvr3, qkvr3, qkvr3, qkvr3, g3, s0, onorm.reshape(1, dv))


def kernel(x_prompt, x_sample, state_rglru_h, state_rglru_conv, state_gla_S, ln_ffn1, ffn1_w1, ffn1_w3, ffn1_w2, ln_mix, ln_ffn2, ffn2_w1, ffn2_w3, ffn2_w2, rg_w_y, rg_w_x, rg_conv_w, rg_conv_b, rg_w_a, rg_b_a, rg_w_i, rg_b_i, rg_lambda, rg_w_o, gla_w_q, gla_w_k, gla_w_v, gla_w_g1, gla_w_g2, gla_b_g, gla_w_r, gla_onorm, gla_w_o, ln_final):
    batch, seq, d = x_prompt.shape
    n_dec = x_sample.shape[0]
    depth = ln_ffn1.shape[0]
    heads, dk, dv = state_gla_S.shape[2:]
    tp, ts = _token_tiles(batch * seq, n_dec)
    tm = tp + ts
    tiles = dict(tp=tp, ts=ts)

    x = (x_prompt.reshape(batch * seq, d), x_sample.reshape(n_dec, d))

    p_h, p_conv, p_s, s_h, s_conv, s_s = [], [], [], [], [], []
    for i in range(depth):
        j = i // 2
        x, u = _ffn(x, ln_ffn1[i], ffn1_w1[i], ffn1_w3[i], ffn1_w2[i], g_mix=ln_mix[i], **tiles)
        if i % 2 == 0:
            proj = _matmul(u, [rg_w_y[j], rg_w_x[j]], [_gelu_tanh, _identity], tm=tm)
            rg_args = (rg_conv_w[j], rg_conv_b[j], rg_w_a[j], rg_b_a[j], rg_w_i[j], rg_b_i[j],
                       rg_lambda[j])
            mix, hl, cl = _rg_prompt(proj, *rg_args, batch=batch, seq=seq, **tiles)
            mix, hn, cn = _rg_sample(mix, proj, state_rglru_conv[j], state_rglru_h[j], *rg_args,
                                     **tiles)
            p_h.append(hl.reshape(batch, d))
            p_conv.append(cl)
            s_h.append(hn)
            s_conv.append(cn.reshape(n_dec, CONV_W - 1, d))
            w_o = rg_w_o[j]
        else:
            scale = dk ** -0.5
            qkvr = _matmul(u, [gla_w_q[j], gla_w_k[j], gla_w_v[j], gla_w_r[j]],
                           [lambda y: y * scale, _identity, _identity, _silu], tm=tm)
            rank = gla_w_g1.shape[2]
            w_g1 = jnp.pad(gla_w_g1[j], ((0, 0), (0, LANES - rank)))
            w_g2 = jnp.pad(gla_w_g2[j], ((0, LANES - rank), (0, 0)))
            g = _gla_gate(u, w_g1, w_g2, gla_b_g[j], tm=tm)
            mix, s_fin = _gla_prompt(qkvr, g, gla_onorm[j], batch=batch, seq=seq, heads=heads,
                                     dk=dk, dv=dv, **tiles)
            mix, s_new = _gla_sample(mix, qkvr, g, state_gla_S[j], gla_onorm[j], heads=heads,
                                     dk=dk, dv=dv, **tiles)
            p_s.append(s_fin)
            s_s.append(s_new)
            w_o = gla_w_o[j]
        x = _matmul(mix.reshape(N_TOKEN_TILES * tm, d), [w_o], [_identity], res=x, tm=tm)
        g_final = ln_final if i == depth - 1 else None
        x = _ffn(x, ln_ffn2[i], ffn2_w1[i], ffn2_w3[i], ffn2_w2[i], g_final=g_final, **tiles)

    y_prompt = x[0].reshape(batch, seq, d)
    y_sample = x[1].reshape(n_dec, 1, d)
    return (y_prompt, y_sample, jnp.stack(p_h), jnp.stack(p_conv), jnp.stack(p_s),
            jnp.stack(s_h), jnp.stack(s_conv), jnp.stack(s_s))
```

```python
import functools

import jax
import jax.numpy as jnp
from jax import lax
from jax.experimental import pallas as pl
from jax.experimental.pallas import tpu as pltpu

F32 = jnp.float32
BF16 = jnp.bfloat16

RMS_EPS = 1e-6
RG_C = 8.0
CONV_W = 4
GLA_GATE_NORM = 16.0
GLA_CHUNK = 64

V7X_VMEM_LIMIT_BYTES = 60 * 1024 * 1024
SUBLANES = 8
LANES = 128
BF16_SUBLANES = 16
N_TOKEN_TILES = 8


def _cparams(*sem):
    return pltpu.CompilerParams(dimension_semantics=sem, vmem_limit_bytes=V7X_VMEM_LIMIT_BYTES)


def _rms_scale(x):
    return x * lax.rsqrt(jnp.mean(x * x, axis=-1, keepdims=True) + RMS_EPS)


def _silu(x):
    return x * jax.nn.sigmoid(x)


def _gelu_tanh(y):
    return jax.nn.gelu(y, approximate=True)


def _identity(y):
    return y


def _softplus(x):
    return jnp.maximum(x, 0.0) + jnp.log1p(jnp.exp(-jnp.abs(x)))


def _neg_expm1(z):
    t = jnp.tanh(0.5 * z)
    return -2.0 * t / (1.0 - t)


def _layer_operand(stacked, layer, block, index_map):
    spec = pl.BlockSpec((None,) + tuple(block), lambda *g: (layer,) + tuple(index_map(*g)))
    return stacked, spec


def _layer_vector(stacked, layer):
    n_layers, n = stacked.shape
    return _layer_operand(stacked.reshape(n_layers, 1, n), layer, (1, n), lambda *g: (0, 0))


def _layer_full(stacked, layer):
    block = stacked.shape[1:]
    return _layer_operand(stacked, layer, block, lambda *g: (0,) * len(block))


def _token_tiles(n_prompt, n_sample):
    tp, ts = n_prompt // N_TOKEN_TILES, n_sample // N_TOKEN_TILES
    assert tp * N_TOKEN_TILES == n_prompt and ts * N_TOKEN_TILES == n_sample
    assert tp % BF16_SUBLANES == 0 and ts % BF16_SUBLANES == 0 and tp % ts == 0
    return tp, ts


def _ffn_body(*refs, tp, ts, split_in, split_out, mix_norm):
    refs = list(refs)
    x_hbm = [refs.pop(0) for _ in range(2 if split_in else 1)]
    g_ref, w1_ref, w3_ref, w2_ref = (refs.pop(0) for _ in range(4))
    g2_ref = refs.pop(0) if (split_out or mix_norm) else None
    o_refs = [refs.pop(0) for _ in range(2 if split_out else 1)]
    u_hbm = refs.pop(0) if mix_norm else None
    xn_ref, sem = refs
    tm = tp + ts
    i = pl.program_id(0)
    f = pl.program_id(1)
    segs = [(slice(0, tp), o_refs[0]), (slice(tp, tm), o_refs[1])] if split_out else \
           [(slice(0, tm), o_refs[0])]

    def seg_ref(rows):
        for srows, o_ref in segs:
            if srows.start <= rows.start and rows.stop <= srows.stop:
                return o_ref.at[pl.ds(rows.start - srows.start, rows.stop - rows.start)]
        raise AssertionError

    @pl.when(f == 0)
    def _():
        pieces = [slice(0, tp), slice(tp, tm)] if (split_in or split_out) else [slice(0, tm)]
        copies = []
        for k, rows in enumerate(pieces):
            n_rows = rows.stop - rows.start
            if split_in:
                src = x_hbm[k].at[pl.ds(i * n_rows, n_rows)]
            else:
                src = x_hbm[0].at[pl.ds(i * tm + rows.start, n_rows)]
            copies.append(pltpu.make_async_copy(src, seg_ref(rows), sem.at[k]))
        for c in copies:
            c.start()
        for c in copies:
            c.wait()
        for rows, o_ref in segs:
            x = o_ref[...]
            xn_ref[rows, :] = (_rms_scale(x) * g_ref[...]).astype(BF16)
            o_ref[...] = 2.0 * x

    xn = xn_ref[...]
    h1 = jnp.dot(xn, w1_ref[...].astype(BF16), preferred_element_type=F32)
    h3 = jnp.dot(xn, w3_ref[...].astype(BF16), preferred_element_type=F32)
    h = (_silu(h1) * h3).astype(BF16)
    y = jnp.dot(h, w2_ref[...].astype(BF16), preferred_element_type=F32)
    for rows, o_ref in segs:
        o_ref[...] += y[rows]

    @pl.when(f == pl.num_programs(1) - 1)
    def _():
        for rows, o_ref in segs:
            y = 0.5 * o_ref[...]
            if split_out:
                y = _rms_scale(y) * g2_ref[...]
            o_ref[...] = y
            if mix_norm:
                xn_ref[rows, :] = (_rms_scale(y) * g2_ref[...]).astype(BF16)
        if mix_norm:
            cu = pltpu.make_async_copy(xn_ref, u_hbm.at[pl.ds(i * tm, tm)], sem.at[0])
            cu.start()
            cu.wait()


def _ffn(x, layer, g, w1, w3, w2, *, tp, ts, g_mix=None, g_final=None, tf=512):
    split_in = isinstance(x, tuple)
    split_out = g_final is not None
    mix_norm = g_mix is not None
    assert not (split_out and mix_norm)
    xs = list(x) if split_in else [x]
    _, d, f = w1.shape
    tm = tp + ts
    operands = [
        _layer_vector(g, layer),
        _layer_operand(w1, layer, (d, tf), lambda i, j: (0, j)),
        _layer_operand(w3, layer, (d, tf), lambda i, j: (0, j)),
        _layer_operand(w2, layer, (tf, d), lambda i, j: (j, 0)),
    ]
    if mix_norm:
        operands.append(_layer_vector(g_mix, layer))
    if split_out:
        operands.append(_layer_vector(g_final, 0))
    in_specs = [pl.BlockSpec(memory_space=pl.ANY)] * len(xs) + [spec for _, spec in operands]
    args = xs + [arr for arr, _ in operands]
    if split_out:
        out_specs = [pl.BlockSpec((tp, d), lambda i, j: (i, 0)), pl.BlockSpec((ts, d), lambda i, j: (i, 0))]
        out_shape = [jax.ShapeDtypeStruct((tp * N_TOKEN_TILES, d), F32),
                     jax.ShapeDtypeStruct((ts * N_TOKEN_TILES, d), F32)]
    else:
        out_specs = [pl.BlockSpec((tm, d), lambda i, j: (i, 0))]
        out_shape = [jax.ShapeDtypeStruct((tm * N_TOKEN_TILES, d), F32)]
    if mix_norm:
        out_specs.append(pl.BlockSpec(memory_space=pl.ANY))
        out_shape.append(jax.ShapeDtypeStruct((tm * N_TOKEN_TILES, d), BF16))
    out = pl.pallas_call(
        functools.partial(_ffn_body, tp=tp, ts=ts, split_in=split_in, split_out=split_out,
                          mix_norm=mix_norm),
        grid=(N_TOKEN_TILES, f // tf),
        in_specs=in_specs,
        out_specs=out_specs,
        out_shape=out_shape,
        scratch_shapes=[pltpu.VMEM((tm, d), BF16), pltpu.SemaphoreType.DMA((2,))],
        compiler_params=_cparams("arbitrary", "arbitrary"),
        name="ffn",
    )(*args)
    return out if (split_out or mix_norm) else out[0]


def _matmul_body(u_ref, *refs, ranges, has_res):
    w_refs = refs[:len(ranges)]
    rest = refs[len(ranges):]
    if has_res:
        r_ref, o_ref, wb_ref = rest
    else:
        o_ref, wb_ref = rest
    j = pl.program_id(0)
    i = pl.program_id(1)
    for (blo, bhi, act), w_ref in zip(ranges, w_refs):
        in_range = (j >= blo) & (j < bhi)

        @pl.when(in_range & (i == 0))
        def _(w_ref=w_ref):
            wb_ref[...] = w_ref[...].astype(BF16)

        @pl.when(in_range)
        def _(act=act):
            z = act(jnp.dot(u_ref[...], wb_ref[...], preferred_element_type=F32))
            if has_res:
                z = r_ref[...] + z
            o_ref[...] = z


def _matmul(u, layer, ws, acts, res=None, *, tm, tn=512):
    m, d = u.shape
    ranges, lo = [], 0
    for w, act in zip(ws, acts):
        ranges.append((lo // tn, (lo + w.shape[2]) // tn, act))
        lo += w.shape[2]
    n = lo
    w_specs = [
        _layer_operand(w, layer, (d, tn),
                       lambda j, i, blo=blo, bhi=bhi: (0, jnp.clip(j - blo, 0, bhi - blo - 1)))[1]
        for w, (blo, bhi, _) in zip(ws, ranges)
    ]
    io_spec = pl.BlockSpec((tm, tn), lambda j, i: (i, j))
    has_res = res is not None
    return pl.pallas_call(
        functools.partial(_matmul_body, ranges=tuple(ranges), has_res=has_res),
        grid=(n // tn, m // tm),
        in_specs=[pl.BlockSpec((tm, d), lambda j, i: (i, 0))] + w_specs + ([io_spec] if has_res else []),
        out_specs=io_spec,
        out_shape=jax.ShapeDtypeStruct((m, n), F32),
        scratch_shapes=[pltpu.VMEM((d, tn), BF16)],
        compiler_params=_cparams("arbitrary", "arbitrary"),
        name="matmul",
    )(u, *ws, *([res] if has_res else []))


def _gla_gate_body(u_ref, w1_ref, w2_ref, b_ref, o_ref):
    t = jnp.dot(u_ref[...], w1_ref[...].astype(BF16), preferred_element_type=F32)
    logit = jnp.dot(t.astype(BF16), w2_ref[...].astype(BF16), preferred_element_type=F32) + b_ref[...]
    o_ref[...] = -_softplus(-logit) * (1.0 / GLA_GATE_NORM)


def _gla_gate(u, w_g1, w_g2, b_g, *, tm):
    m, d = u.shape
    rank, n = w_g2.shape
    return pl.pallas_call(
        _gla_gate_body,
        grid=(m // tm,),
        in_specs=[
            pl.BlockSpec((tm, d), lambda i: (i, 0)),
            pl.BlockSpec((d, rank), lambda i: (0, 0)),
            pl.BlockSpec((rank, n), lambda i: (0, 0)),
            pl.BlockSpec((1, n), lambda i: (0, 0)),
        ],
        out_specs=pl.BlockSpec((tm, n), lambda i: (i, 0)),
        out_shape=jax.ShapeDtypeStruct((m, n), F32),
        compiler_params=_cparams("arbitrary"),
        name="gla_gate",
    )(u, w_g1, w_g2, b_g.reshape(1, n))


def _prompt_rows_of_tile(o_ref, nt, per_tile, tb):
    blk = pl.program_id(0) * nt + pl.program_id(1)
    k = blk % per_tile
    tp = per_tile * tb

    @pl.when(k == 0)
    def _():
        o_ref[0, tp:, :] = jnp.zeros((o_ref.shape[1] - tp, o_ref.shape[2]), o_ref.dtype)

    return pl.ds(pl.multiple_of(k * tb, tb), tb)


def _rglru_coeffs(xc, wa_ref, ba_ref, wi_ref, bi_ref, lam_ref, store):
    n_blocks, bw, _ = wa_ref.shape
    for n in range(n_blocks):
        cols = slice(n * bw, (n + 1) * bw)
        xb = xc[:, cols]
        xb16 = xb.astype(BF16)
        wa = wa_ref[n].astype(BF16)
        wi = wi_ref[n].astype(BF16)
        r = jax.nn.sigmoid(jnp.dot(xb16, wa, preferred_element_type=F32) + ba_ref[:, cols])
        i = jax.nn.sigmoid(jnp.dot(xb16, wi, preferred_element_type=F32) + bi_ref[:, cols])
        log_a = -RG_C * r * _softplus(-lam_ref[:, cols])
        a = jnp.exp(log_a)
        b = jnp.sqrt(_neg_expm1(2.0 * log_a)) * (i * xb)
        store(n, a, b)


def _rg_prompt_body(xw_ref, gate_ref, cw_ref, cb_ref, wa_ref, ba_ref, wi_ref, bi_ref, lam_ref,
                    o_ref, hl_ref, cl_ref, ext_ref, a_ref, b_ref, h_ref, *, nt, per_tile):
    t = pl.program_id(1)
    tb = xw_ref.shape[1]
    bw = wa_ref.shape[1]
    pad = SUBLANES

    @pl.when(t == 0)
    def _():
        ext_ref[0:pad, :] = jnp.zeros((pad, ext_ref.shape[1]), F32)
        h_ref[...] = jnp.zeros_like(h_ref)

    ext_ref[pad:pad + tb, :] = xw_ref[0]
    xc = cb_ref[...]
    for j in range(CONV_W):
        off = pad - (CONV_W - 1) + j
        xc = xc + ext_ref[off:off + tb, :] * cw_ref[j:j + 1, :]

    def store(n, a, b):
        a_ref[:, n * bw:(n + 1) * bw] = a
        b_ref[:, n * bw:(n + 1) * bw] = b

    _rglru_coeffs(xc, wa_ref, ba_ref, wi_ref, bi_ref, lam_ref, store)

    def step(r, h):
        h = a_ref[pl.ds(r, 1), :] * h + b_ref[pl.ds(r, 1), :]
        b_ref[pl.ds(r, 1), :] = h
        return h

    h = lax.fori_loop(0, tb, step, h_ref[0:1, :], unroll=8)
    h_ref[0:1, :] = h
    out_rows = _prompt_rows_of_tile(o_ref, nt, per_tile, tb)
    o_ref[0, out_rows, :] = (b_ref[...] * gate_ref[0]).astype(o_ref.dtype)
    hl_ref[0] = h
    cl_ref[0] = ext_ref[pad + tb - (CONV_W - 1):pad + tb, :]
    ext_ref[0:pad, :] = ext_ref[tb:tb + pad, :]


def _rg_params(layer, conv_w, conv_b, w_a, b_a, w_i, b_i, lam):
    operands = [_layer_full(conv_w, layer), _layer_vector(conv_b, layer), _layer_full(w_a, layer),
                _layer_vector(b_a, layer), _layer_full(w_i, layer), _layer_vector(b_i, layer),
                _layer_vector(lam, layer)]
    return [arr for arr, _ in operands], [spec for _, spec in operands]


def _rg_prompt(proj, params, *, batch, seq, tp, ts, tb=256):
    p_args, p_specs = params
    m, d2 = proj.shape
    d = d2 // 2
    tm = tp + ts
    nt = seq // tb
    per_tile = tp // tb
    proj3 = proj.reshape(N_TOKEN_TILES, tm, d2)

    def tok(col):
        def index(b, t):
            blk = b * nt + t
            return (blk // per_tile, blk % per_tile, col)
        return index

    mix, hl, cl = pl.pallas_call(
        functools.partial(_rg_prompt_body, nt=nt, per_tile=per_tile),
        grid=(batch, nt),
        in_specs=[pl.BlockSpec((1, tb, d), tok(1)), pl.BlockSpec((1, tb, d), tok(0))] + p_specs,
        out_specs=[
            pl.BlockSpec((1, tm, d), lambda b, t: ((b * nt + t) // per_tile, 0, 0)),
            pl.BlockSpec((1, 1, d), lambda b, t: (b, 0, 0)),
            pl.BlockSpec((1, CONV_W - 1, d), lambda b, t: (b, 0, 0)),
        ],
        out_shape=[
            jax.ShapeDtypeStruct((N_TOKEN_TILES, tm, d), BF16),
            jax.ShapeDtypeStruct((batch, 1, d), F32),
            jax.ShapeDtypeStruct((batch, CONV_W - 1, d), F32),
        ],
        scratch_shapes=[
            pltpu.VMEM((tb + SUBLANES, d), F32),
            pltpu.VMEM((tb, d), F32),
            pltpu.VMEM((tb, d), F32),
            pltpu.VMEM((SUBLANES, d), F32),
        ],
        compiler_params=_cparams("arbitrary", "arbitrary"),
        name="rg_prompt",
    )(proj3, proj3, *p_args)
    return mix, hl, cl


def _rg_sample_body(mix_ref, xw_ref, gate_ref, conv_ref, h0_ref, cw_ref, cb_ref, wa_ref, ba_ref,
                    wi_ref, bi_ref, lam_ref, o_ref, hn_ref, cn_ref, a_ref, b_ref):
    del mix_ref
    n, d = h0_ref.shape
    bw = wa_ref.shape[1]
    x = xw_ref[...].reshape(n, d)
    xc = cb_ref[...]
    for j in range(CONV_W - 1):
        xc = xc + conv_ref[:, j * d:(j + 1) * d] * cw_ref[j:j + 1, :]
    xc = xc + x * cw_ref[CONV_W - 1:CONV_W, :]

    def store(k, a, b):
        a_ref[:, k * bw:(k + 1) * bw] = a
        b_ref[:, k * bw:(k + 1) * bw] = b

    _rglru_coeffs(xc, wa_ref, ba_ref, wi_ref, bi_ref, lam_ref, store)
    h = b_ref[...] + a_ref[...] * h0_ref[...]
    hn_ref[...] = h
    o_ref[...] = (h * gate_ref[...].reshape(n, d)).reshape(o_ref.shape).astype(o_ref.dtype)
    cn_ref[:, 0:(CONV_W - 2) * d] = conv_ref[:, d:(CONV_W - 1) * d]
    cn_ref[:, (CONV_W - 2) * d:] = x


def _rg_sample(mix, proj, layer, conv_state, h0, params, *, tp, ts):
    p_args, p_specs = params
    n_layers, n, d = h0.shape
    tm = tp + ts
    proj3 = proj.reshape(N_TOKEN_TILES, tm, 2 * d)
    s_blk = tp // ts
    full2 = lambda shape: pl.BlockSpec(shape, lambda i: (0, 0))
    kc = (CONV_W - 1) * d
    conv_arg, conv_spec = _layer_full(conv_state.reshape(n_layers, n, kc), layer)
    h0_arg, h0_spec = _layer_full(h0, layer)
    return pl.pallas_call(
        _rg_sample_body,
        grid=(1,),
        in_specs=[
            pl.BlockSpec(memory_space=pl.ANY),
            pl.BlockSpec((N_TOKEN_TILES, ts, d), lambda i: (0, s_blk, 1)),
            pl.BlockSpec((N_TOKEN_TILES, ts, d), lambda i: (0, s_blk, 0)),
            conv_spec, h0_spec,
        ] + p_specs,
        out_specs=[
            pl.BlockSpec((N_TOKEN_TILES, ts, d), lambda i: (0, s_blk, 0)),
            full2((n, d)),
            full2((n, kc)),
        ],
        out_shape=[
            jax.ShapeDtypeStruct(mix.shape, mix.dtype),
            jax.ShapeDtypeStruct((n, d), F32),
            jax.ShapeDtypeStruct((n, kc), F32),
        ],
        scratch_shapes=[pltpu.VMEM((n, d), F32), pltpu.VMEM((n, d), F32)],
        input_output_aliases={0: 0},
        compiler_params=_cparams("arbitrary"),
        name="rg_sample",
    )(mix, proj3, proj3, conv_arg, h0_arg, *p_args)


def _gla_prompt_body(q_ref, k_ref, v_ref, r_ref, g_ref, on_ref, o_ref, s_ref, *, heads, nt, per_tile):
    t = pl.program_id(1)
    tb = q_ref.shape[1]
    c = GLA_CHUNK
    dk = q_ref.shape[2] // heads
    dv = v_ref.shape[2] // heads

    @pl.when(t == 0)
    def _():
        s_ref[...] = jnp.zeros_like(s_ref)

    row = lax.broadcasted_iota(jnp.int32, (tb, tb), 0)
    col = lax.broadcasted_iota(jnp.int32, (tb, tb), 1)
    same_chunk_causal = ((row >= col) & ((row ^ col) < c)).astype(BF16)
    g_rest = g_ref[0]
    bcum = jnp.zeros(g_rest.shape, F32)
    for _ in range(3):
        g_part = g_rest.astype(BF16)
        bcum = bcum + jnp.dot(same_chunk_causal, g_part, preferred_element_type=F32)
        g_rest = g_rest - g_part.astype(F32)
    q_i = (q_ref[0] * jnp.exp(bcum)).astype(BF16)
    k_i = (k_ref[0] * jnp.exp(-bcum)).astype(BF16)
    causal = (lax.broadcasted_iota(jnp.int32, (c, c), 0) >= lax.broadcasted_iota(jnp.int32, (c, c), 1))

    out_base = _prompt_rows_of_tile(o_ref, nt, per_tile, tb).start
    for ci in range(tb // c):
        rows = slice(ci * c, (ci + 1) * c)
        out_rows = pl.ds(pl.multiple_of(out_base + ci * c, c), c)
        b_last = bcum[(ci + 1) * c - 1:(ci + 1) * c, :]
        k_e = (k_ref[0, rows, :] * jnp.exp(b_last - bcum[rows, :])).astype(BF16)
        decay_t = jnp.exp(jnp.broadcast_to(b_last, (SUBLANES, heads * dk))).T
        for h in range(heads):
            kcols = slice(h * dk, (h + 1) * dk)
            vcols = slice(h * dv, (h + 1) * dv)
            qs = q_i[rows, kcols]
            v = v_ref[0, rows, vcols].astype(BF16)
            scores = lax.dot_general(qs, k_i[rows, kcols], (((1,), (1,)), ((), ())),
                                     preferred_element_type=F32)
            scores = jnp.where(causal, scores, 0.0).astype(BF16)
            s = s_ref[0, h]
            o = (jnp.dot(qs, s.astype(BF16), preferred_element_type=F32)
                 + jnp.dot(scores, v, preferred_element_type=F32))
            s_ref[0, h] = decay_t[kcols, 0:1] * s + lax.dot_general(
                k_e[:, kcols], v, (((0,), (0,)), ((), ())), preferred_element_type=F32)
            o = _rms_scale(o) * on_ref[...]
            o_ref[0, out_rows, vcols] = (o * r_ref[0, rows, vcols]).astype(o_ref.dtype)


def _gla_prompt(qkvr, g, layer, onorm, *, batch, seq, heads, dk, dv, tp, ts, tb=256):
    tm = tp + ts
    nt = seq // tb
    per_tile = tp // tb
    hk, hv = heads * dk, heads * dv
    assert hv == 2 * hk
    qkvr3 = qkvr.reshape(N_TOKEN_TILES, tm, qkvr.shape[1])
    g3 = g.reshape(N_TOKEN_TILES, tm, hk)

    def tok(col):
        def index(b, t):
            blk = b * nt + t
            return (blk // per_tile, blk % per_tile, col)
        return index

    return pl.pallas_call(
        functools.partial(_gla_prompt_body, heads=heads, nt=nt, per_tile=per_tile),
        grid=(batch, nt),
        in_specs=[
            pl.BlockSpec((1, tb, hk), tok(0)),
            pl.BlockSpec((1, tb, hk), tok(1)),
            pl.BlockSpec((1, tb, hv), tok(1)),
            pl.BlockSpec((1, tb, hv), tok(2)),
            pl.BlockSpec((1, tb, hk), tok(0)),
            _layer_vector(onorm, layer)[1],
        ],
        out_specs=[
            pl.BlockSpec((1, tm, hv), lambda b, t: ((b * nt + t) // per_tile, 0, 0)),
            pl.BlockSpec((1, heads, dk, dv), lambda b, t: (b, 0, 0, 0)),
        ],
        out_shape=[
            jax.ShapeDtypeStruct((N_TOKEN_TILES, tm, hv), BF16),
            jax.ShapeDtypeStruct((batch, heads, dk, dv), F32),
        ],
        compiler_params=_cparams("arbitrary", "arbitrary"),
        name="gla_prompt",
    )(qkvr3, qkvr3, qkvr3, qkvr3, g3, _layer_vector(onorm, layer)[0])


def _gla_sample_body(mix_ref, q_ref, k_ref, v_ref, r_ref, g_ref, s0_ref, on_ref, o_ref, s_ref):
    del mix_ref
    bb = q_ref.shape[1]
    alpha_t = jnp.exp(g_ref[0]).T
    k_t = k_ref[0].T
    q_t = q_ref[0].T
    v = v_ref[0]
    rows = []
    for b in range(bb):
        s_new = alpha_t[:, b:b + 1] * s0_ref[b, 0] + k_t[:, b:b + 1] * v[b:b + 1, :]
        s_ref[b, 0] = s_new
        rows.append(jnp.sum(q_t[:, b:b + 1] * s_new, axis=0, keepdims=True))
    o = jnp.concatenate(rows, axis=0)
    o = _rms_scale(o) * on_ref[...]
    o_ref[0] = (o * r_ref[0]).astype(o_ref.dtype)


def _gla_sample(mix, qkvr, g, layer, s0, onorm, *, heads, dk, dv, tp, ts):
    tm = tp + ts
    s_blk = tp // ts
    kq = heads
    vq = (2 * heads * dk) // dv
    rq = vq + heads
    qkvr3 = qkvr.reshape(N_TOKEN_TILES, tm, qkvr.shape[1])
    g3 = g.reshape(N_TOKEN_TILES, tm, heads * dk)
    return pl.pallas_call(
        _gla_sample_body,
        grid=(N_TOKEN_TILES, heads),
        in_specs=[
            pl.BlockSpec(memory_space=pl.ANY),
            pl.BlockSpec((1, ts, dk), lambda s, h: (s, s_blk, h)),
            pl.BlockSpec((1, ts, dk), lambda s, h: (s, s_blk, kq + h)),
            pl.BlockSpec((1, ts, dv), lambda s, h: (s, s_blk, vq + h)),
            pl.BlockSpec((1, ts, dv), lambda s, h: (s, s_blk, rq + h)),
            pl.BlockSpec((1, ts, dk), lambda s, h: (s, s_blk, h)),
            _layer_operand(s0, layer, (ts, 1, dk, dv), lambda s, h: (s, h, 0, 0))[1],
            _layer_vector(onorm, layer)[1],
        ],
        out_specs=[
            pl.BlockSpec((1, ts, dv), lambda s, h: (s, s_blk, h)),
            pl.BlockSpec((ts, 1, dk, dv), lambda s, h: (s, h, 0, 0)),
        ],
        out_shape=[
            jax.ShapeDtypeStruct(mix.shape, mix.dtype),
            jax.ShapeDtypeStruct(s0.shape[1:], F32),
        ],
        input_output_aliases={0: 0},
        compiler_params=_cparams("arbitrary", "arbitrary"),
        name="gla_sample",
    )(mix, qkvr3, qkvr3, qkvr3, qkvr3, g3, s0, _layer_vector(onorm, layer)[0])


def kernel(x_prompt, x_sample, state_rglru_h, state_rglru_conv, state_gla_S, ln_ffn1, ffn1_w1, ffn1_w3, ffn1_w2, ln_mix, ln_ffn2, ffn2_w1, ffn2_w3, ffn2_w2, rg_w_y, rg_w_x, rg_conv_w, rg_conv_b, rg_w_a, rg_b_a, rg_w_i, rg_b_i, rg_lambda, rg_w_o, gla_w_q, gla_w_k, gla_w_v, gla_w_g1, gla_w_g2, gla_b_g, gla_w_r, gla_onorm, gla_w_o, ln_final):
    batch, seq, d = x_prompt.shape
    n_dec = x_sample.shape[0]
    depth = ln_ffn1.shape[0]
    heads, dk, dv = state_gla_S.shape[2:]
    tp, ts = _token_tiles(batch * seq, n_dec)
    tm = tp + ts
    tiles = dict(tp=tp, ts=ts)

    x = (x_prompt.reshape(batch * seq, d), x_sample.reshape(n_dec, d))

    p_h, p_conv, p_s, s_h, s_conv, s_s = [], [], [], [], [], []
    for i in range(depth):
        j = i // 2
        x, u = _ffn(x, i, ln_ffn1, ffn1_w1, ffn1_w3, ffn1_w2, g_mix=ln_mix, **tiles)
        if i % 2 == 0:
            proj = _matmul(u, j, [rg_w_y, rg_w_x], [_gelu_tanh, _identity], tm=tm)
            params = _rg_params(j, rg_conv_w, rg_conv_b, rg_w_a, rg_b_a, rg_w_i, rg_b_i, rg_lambda)
            mix, hl, cl = _rg_prompt(proj, params, batch=batch, seq=seq, **tiles)
            mix, hn, cn = _rg_sample(mix, proj, j, state_rglru_conv, state_rglru_h, params, **tiles)
            p_h.append(hl.reshape(batch, d))
            p_conv.append(cl)
            s_h.append(hn)
            s_conv.append(cn.reshape(n_dec, CONV_W - 1, d))
            w_o = rg_w_o
        else:
            scale = dk ** -0.5
            qkvr = _matmul(u, j, [gla_w_q, gla_w_k, gla_w_v, gla_w_r],
                           [lambda y: y * scale, _identity, _identity, _silu], tm=tm)
            rank = gla_w_g1.shape[2]
            w_g1 = jnp.pad(gla_w_g1[j], ((0, 0), (0, LANES - rank)))
            w_g2 = jnp.pad(gla_w_g2[j], ((0, LANES - rank), (0, 0)))
            g = _gla_gate(u, w_g1, w_g2, gla_b_g[j], tm=tm)
            mix, s_fin = _gla_prompt(qkvr, g, j, gla_onorm, batch=batch, seq=seq, heads=heads,
                                     dk=dk, dv=dv, **tiles)
            mix, s_new = _gla_sample(mix, qkvr, g, j, state_gla_S, gla_onorm, heads=heads,
                                     dk=dk, dv=dv, **tiles)
            p_s.append(s_fin)
            s_s.append(s_new)
            w_o = gla_w_o
        x = _matmul(mix.reshape(N_TOKEN_TILES * tm, d), j, [w_o], [_identity], res=x, tm=tm)
        g_final = ln_final.reshape(1, d) if i == depth - 1 else None
        x = _ffn(x, i, ln_ffn2, ffn2_w1, ffn2_w3, ffn2_w2, g_final=g_final, tf=256, **tiles)

    y_prompt = x[0].reshape(batch, seq, d)
    y_sample = x[1].reshape(n_dec, 1, d)
    return (y_prompt, y_sample, jnp.stack(p_h), jnp.stack(p_conv), jnp.stack(p_s),
            jnp.stack(s_h), jnp.stack(s_conv), jnp.stack(s_s))
```

```python
import functools

import jax
import jax.numpy as jnp
from jax import lax
from jax.experimental import pallas as pl
from jax.experimental.pallas import tpu as pltpu

F32 = jnp.float32
BF16 = jnp.bfloat16

RMS_EPS = 1e-6
RG_C = 8.0
CONV_W = 4
GLA_GATE_NORM = 16.0
GLA_CHUNK = 64

V7X_VMEM_LIMIT_BYTES = 60 * 1024 * 1024
SUBLANES = 8
LANES = 128
BF16_SUBLANES = 16
N_TOKEN_TILES = 8


def _cparams(*sem):
    return pltpu.CompilerParams(dimension_semantics=sem, vmem_limit_bytes=V7X_VMEM_LIMIT_BYTES)


def _rms_scale(x):
    return x * lax.rsqrt(jnp.mean(x * x, axis=-1, keepdims=True) + RMS_EPS)


def _silu(x):
    return x * jax.nn.sigmoid(x)


def _gelu_tanh(y):
    return jax.nn.gelu(y, approximate=True)


def _identity(y):
    return y


def _softplus(x):
    return jnp.maximum(x, 0.0) + jnp.log1p(jnp.exp(-jnp.abs(x)))


def _neg_expm1(z):
    t = jnp.tanh(0.5 * z)
    return -2.0 * t / (1.0 - t)


ROW_CHUNK = BF16_SUBLANES


def _for_row_chunks(n_rows, body):
    n = n_rows // ROW_CHUNK
    assert n * ROW_CHUNK == n_rows
    unroll = next(u for u in (8, 5, 4, 3, 2, 1) if n % u == 0)

    def step(c, carry):
        body(pl.multiple_of(c * ROW_CHUNK, ROW_CHUNK))
        return carry

    lax.fori_loop(0, n, step, 0, unroll=unroll)


def _chunk_rows(base, off):
    assert base % ROW_CHUNK == 0
    return pl.ds(pl.multiple_of(base + off, ROW_CHUNK), ROW_CHUNK)


def _layer_operand(stacked, layer, block, index_map):
    spec = pl.BlockSpec((None,) + tuple(block), lambda *g: (layer,) + tuple(index_map(*g)))
    return stacked, spec


def _layer_vector(stacked, layer):
    n_layers, n = stacked.shape
    return _layer_operand(stacked.reshape(n_layers, 1, n), layer, (1, n), lambda *g: (0, 0))


def _layer_full(stacked, layer):
    block = stacked.shape[1:]
    return _layer_operand(stacked, layer, block, lambda *g: (0,) * len(block))


def _token_tiles(n_prompt, n_sample):
    tp, ts = n_prompt // N_TOKEN_TILES, n_sample // N_TOKEN_TILES
    assert tp * N_TOKEN_TILES == n_prompt and ts * N_TOKEN_TILES == n_sample
    assert tp % BF16_SUBLANES == 0 and ts % BF16_SUBLANES == 0 and tp % ts == 0
    return tp, ts


def _cast_body(w_ref, o_ref):
    o_ref[...] = w_ref[...].astype(o_ref.dtype)


def _to_bf16(w, *, rows=512):
    n_layers, r, c = w.shape
    spec = pl.BlockSpec((None, rows, c), lambda l, i: (l, i, 0))
    return pl.pallas_call(
        _cast_body,
        grid=(n_layers, r // rows),
        in_specs=[spec],
        out_specs=spec,
        out_shape=jax.ShapeDtypeStruct(w.shape, BF16),
        compiler_params=_cparams("arbitrary", "arbitrary"),
        name="to_bf16",
    )(w)


def _ffn_body(*refs, tp, ts, split_in, split_out, mix_norm):
    refs = list(refs)
    x_refs = [refs.pop(0) for _ in range(2 if split_in else 1)]
    g_ref, w1_ref, w3_ref, w2_ref = (refs.pop(0) for _ in range(4))
    g2_ref = refs.pop(0) if (split_out or mix_norm) else None
    o_refs = [refs.pop(0) for _ in range(2 if split_out else 1)]
    u_hbm = refs.pop(0) if mix_norm else None
    xn_ref = refs.pop(0)
    acc_ref = refs.pop(0) if split_out else o_refs[0]
    (sem,) = refs
    tm = tp + ts
    i = pl.program_id(0)
    f = pl.program_id(1)

    def segments(blocks):
        if len(blocks) == 2:
            return [(0, tp, blocks[0]), (tp, ts, blocks[1])]
        return [(0, tm, blocks[0])]

    @pl.when(f == 0)
    def _():
        for t0, n_rows, x_ref in segments(x_refs):
            def chunk(off, x_ref=x_ref, t0=t0):
                x = x_ref[_chunk_rows(0, off), :]
                xn_ref[_chunk_rows(t0, off), :] = (_rms_scale(x) * g_ref[...]).astype(BF16)
                acc_ref[_chunk_rows(t0, off), :] = 2.0 * x

            _for_row_chunks(n_rows, chunk)

    xn = xn_ref[...]
    h1 = jnp.dot(xn, w1_ref[...].astype(BF16), preferred_element_type=F32)
    h3 = jnp.dot(xn, w3_ref[...].astype(BF16), preferred_element_type=F32)
    h = (_silu(h1) * h3).astype(BF16)
    acc_ref[...] += jnp.dot(h, w2_ref[...].astype(BF16), preferred_element_type=F32)

    @pl.when(f == pl.num_programs(1) - 1)
    def _():
        for t0, n_rows, o_ref in segments(o_refs):
            def chunk(off, o_ref=o_ref, t0=t0):
                y = 0.5 * acc_ref[_chunk_rows(t0, off), :]
                if split_out:
                    y = _rms_scale(y) * g2_ref[...]
                o_ref[_chunk_rows(0, off), :] = y
                if mix_norm:
                    xn_ref[_chunk_rows(t0, off), :] = (_rms_scale(y) * g2_ref[...]).astype(BF16)

            _for_row_chunks(n_rows, chunk)
        if mix_norm:
            cu = pltpu.make_async_copy(xn_ref, u_hbm.at[pl.ds(i * tm, tm)], sem.at[0])
            cu.start()
            cu.wait()


def _ffn(x, layer, g, w1, w3, w2, *, tp, ts, g_mix=None, g_final=None, tf=512):
    split_in = isinstance(x, tuple)
    split_out = g_final is not None
    mix_norm = g_mix is not None
    assert not (split_out and mix_norm)
    xs = list(x) if split_in else [x]
    _, d, f = w1.shape
    tm = tp + ts
    operands = [
        _layer_vector(g, layer),
        _layer_operand(w1, layer, (d, tf), lambda i, j: (0, j)),
        _layer_operand(w3, layer, (d, tf), lambda i, j: (0, j)),
        _layer_operand(w2, layer, (tf, d), lambda i, j: (j, 0)),
    ]
    if mix_norm:
        operands.append(_layer_vector(g_mix, layer))
    if split_out:
        operands.append(_layer_vector(g_final, 0))
    row_tile = lambda rows: pl.BlockSpec((rows, d), lambda i, j: (i, 0))
    x_specs = [row_tile(tp), row_tile(ts)] if split_in else [row_tile(tm)]
    in_specs = x_specs + [spec for _, spec in operands]
    args = xs + [arr for arr, _ in operands]
    if split_out:
        out_specs = [pl.BlockSpec((tp, d), lambda i, j: (i, 0)), pl.BlockSpec((ts, d), lambda i, j: (i, 0))]
        out_shape = [jax.ShapeDtypeStruct((tp * N_TOKEN_TILES, d), F32),
                     jax.ShapeDtypeStruct((ts * N_TOKEN_TILES, d), F32)]
    else:
        out_specs = [pl.BlockSpec((tm, d), lambda i, j: (i, 0))]
        out_shape = [jax.ShapeDtypeStruct((tm * N_TOKEN_TILES, d), F32)]
    if mix_norm:
        out_specs.append(pl.BlockSpec(memory_space=pl.ANY))
        out_shape.append(jax.ShapeDtypeStruct((tm * N_TOKEN_TILES, d), BF16))
    out = pl.pallas_call(
        functools.partial(_ffn_body, tp=tp, ts=ts, split_in=split_in, split_out=split_out,
                          mix_norm=mix_norm),
        grid=(N_TOKEN_TILES, f // tf),
        in_specs=in_specs,
        out_specs=out_specs,
        out_shape=out_shape,
        scratch_shapes=[pltpu.VMEM((tm, d), BF16)] + ([pltpu.VMEM((tm, d), F32)] if split_out else [])
        + [pltpu.SemaphoreType.DMA((1,))],
        compiler_params=_cparams("arbitrary", "arbitrary"),
        name="ffn",
    )(*args)
    return out if (split_out or mix_norm) else out[0]


def _matmul_body(u_ref, *refs, ranges, has_res):
    w_refs = refs[:len(ranges)]
    rest = refs[len(ranges):]
    if has_res:
        r_ref, o_ref, wb_ref = rest
    else:
        o_ref, wb_ref = rest
    j = pl.program_id(0)
    i = pl.program_id(1)
    for (blo, bhi, act), w_ref in zip(ranges, w_refs):
        in_range = (j >= blo) & (j < bhi)

        @pl.when(in_range & (i == 0))
        def _(w_ref=w_ref):
            wb_ref[...] = w_ref[...].astype(BF16)

        @pl.when(in_range)
        def _(act=act):
            z = act(jnp.dot(u_ref[...], wb_ref[...], preferred_element_type=F32))
            if has_res:
                z = r_ref[...] + z
            o_ref[...] = z


def _matmul(u, layer, ws, acts, res=None, *, tm, tn=512):
    m, d = u.shape
    ranges, lo = [], 0
    for w, act in zip(ws, acts):
        ranges.append((lo // tn, (lo + w.shape[2]) // tn, act))
        lo += w.shape[2]
    n = lo
    w_specs = [
        _layer_operand(w, layer, (d, tn),
                       lambda j, i, blo=blo, bhi=bhi: (0, jnp.clip(j - blo, 0, bhi - blo - 1)))[1]
        for w, (blo, bhi, _) in zip(ws, ranges)
    ]
    io_spec = pl.BlockSpec((tm, tn), lambda j, i: (i, j))
    has_res = res is not None
    return pl.pallas_call(
        functools.partial(_matmul_body, ranges=tuple(ranges), has_res=has_res),
        grid=(n // tn, m // tm),
        in_specs=[pl.BlockSpec((tm, d), lambda j, i: (i, 0))] + w_specs + ([io_spec] if has_res else []),
        out_specs=io_spec,
        out_shape=jax.ShapeDtypeStruct((m, n), F32),
        scratch_shapes=[pltpu.VMEM((d, tn), BF16)],
        compiler_params=_cparams("arbitrary", "arbitrary"),
        name="matmul",
    )(u, *ws, *([res] if has_res else []))


def _gla_gate_body(u_ref, w1_ref, w2_ref, b_ref, o_ref):
    t = jnp.dot(u_ref[...], w1_ref[...].astype(BF16), preferred_element_type=F32)
    logit = jnp.dot(t.astype(BF16), w2_ref[...].astype(BF16), preferred_element_type=F32) + b_ref[...]
    o_ref[...] = -_softplus(-logit) * (1.0 / GLA_GATE_NORM)


def _gla_gate(u, w_g1, w_g2, b_g, *, tm):
    m, d = u.shape
    rank, n = w_g2.shape
    return pl.pallas_call(
        _gla_gate_body,
        grid=(m // tm,),
        in_specs=[
            pl.BlockSpec((tm, d), lambda i: (i, 0)),
            pl.BlockSpec((d, rank), lambda i: (0, 0)),
            pl.BlockSpec((rank, n), lambda i: (0, 0)),
            pl.BlockSpec((1, n), lambda i: (0, 0)),
        ],
        out_specs=pl.BlockSpec((tm, n), lambda i: (i, 0)),
        out_shape=jax.ShapeDtypeStruct((m, n), F32),
        compiler_params=_cparams("arbitrary"),
        name="gla_gate",
    )(u, w_g1, w_g2, b_g.reshape(1, n))


def _prompt_rows_of_tile(o_ref, nt, per_tile, tb):
    blk = pl.program_id(0) * nt + pl.program_id(1)
    k = blk % per_tile
    tp = per_tile * tb

    @pl.when(k == 0)
    def _():
        o_ref[0, tp:, :] = jnp.zeros((o_ref.shape[1] - tp, o_ref.shape[2]), o_ref.dtype)

    return pl.ds(pl.multiple_of(k * tb, tb), tb)


def _rglru_coeffs(xc, wa_ref, ba_ref, wi_ref, bi_ref, lam_ref, store):
    n_blocks, bw, _ = wa_ref.shape
    for n in range(n_blocks):
        cols = slice(n * bw, (n + 1) * bw)
        xb = xc[:, cols]
        xb16 = xb.astype(BF16)
        wa = wa_ref[n].astype(BF16)
        wi = wi_ref[n].astype(BF16)
        r = jax.nn.sigmoid(jnp.dot(xb16, wa, preferred_element_type=F32) + ba_ref[:, cols])
        i = jax.nn.sigmoid(jnp.dot(xb16, wi, preferred_element_type=F32) + bi_ref[:, cols])
        log_a = -RG_C * r * _softplus(-lam_ref[:, cols])
        a = jnp.exp(log_a)
        b = jnp.sqrt(_neg_expm1(2.0 * log_a)) * (i * xb)
        store(n, a, b)


def _rg_prompt_body(xw_ref, gate_ref, cw_ref, cb_ref, wa_ref, ba_ref, wi_ref, bi_ref, lam_ref,
                    o_ref, hl_ref, cl_ref, ext_ref, a_ref, b_ref, h_ref, *, nt, per_tile):
    t = pl.program_id(1)
    tb = xw_ref.shape[1]
    bw = wa_ref.shape[1]
    pad = SUBLANES

    @pl.when(t == 0)
    def _():
        ext_ref[0:pad, :] = jnp.zeros((pad, ext_ref.shape[1]), F32)
        h_ref[...] = jnp.zeros_like(h_ref)

    ext_ref[pad:pad + tb, :] = xw_ref[0]
    xc = cb_ref[...]
    for j in range(CONV_W):
        off = pad - (CONV_W - 1) + j
        xc = xc + ext_ref[off:off + tb, :] * cw_ref[j:j + 1, :]

    def store(n, a, b):
        a_ref[:, n * bw:(n + 1) * bw] = a
        b_ref[:, n * bw:(n + 1) * bw] = b

    _rglru_coeffs(xc, wa_ref, ba_ref, wi_ref, bi_ref, lam_ref, store)

    def step(r, h):
        h = a_ref[pl.ds(r, 1), :] * h + b_ref[pl.ds(r, 1), :]
        b_ref[pl.ds(r, 1), :] = h
        return h

    h = lax.fori_loop(0, tb, step, h_ref[0:1, :], unroll=8)
    h_ref[0:1, :] = h
    out_rows = _prompt_rows_of_tile(o_ref, nt, per_tile, tb)
    o_ref[0, out_rows, :] = (b_ref[...] * gate_ref[0]).astype(o_ref.dtype)
    hl_ref[0] = h
    cl_ref[0] = ext_ref[pad + tb - (CONV_W - 1):pad + tb, :]
    ext_ref[0:pad, :] = ext_ref[tb:tb + pad, :]


def _rg_params(layer, conv_w, conv_b, w_a, b_a, w_i, b_i, lam):
    operands = [_layer_full(conv_w, layer), _layer_vector(conv_b, layer), _layer_full(w_a, layer),
                _layer_vector(b_a, layer), _layer_full(w_i, layer), _layer_vector(b_i, layer),
                _layer_vector(lam, layer)]
    return [arr for arr, _ in operands], [spec for _, spec in operands]


def _rg_prompt(proj, params, *, batch, seq, tp, ts, tb=256):
    p_args, p_specs = params
    m, d2 = proj.shape
    d = d2 // 2
    tm = tp + ts
    nt = seq // tb
    per_tile = tp // tb
    proj3 = proj.reshape(N_TOKEN_TILES, tm, d2)

    def tok(col):
        def index(b, t):
            blk = b * nt + t
            return (blk // per_tile, blk % per_tile, col)
        return index

    mix, hl, cl = pl.pallas_call(
        functools.partial(_rg_prompt_body, nt=nt, per_tile=per_tile),
        grid=(batch, nt),
        in_specs=[pl.BlockSpec((1, tb, d), tok(1)), pl.BlockSpec((1, tb, d), tok(0))] + p_specs,
        out_specs=[
            pl.BlockSpec((1, tm, d), lambda b, t: ((b * nt + t) // per_tile, 0, 0)),
            pl.BlockSpec((1, 1, d), lambda b, t: (b, 0, 0)),
            pl.BlockSpec((1, CONV_W - 1, d), lambda b, t: (b, 0, 0)),
        ],
        out_shape=[
            jax.ShapeDtypeStruct((N_TOKEN_TILES, tm, d), BF16),
            jax.ShapeDtypeStruct((batch, 1, d), F32),
            jax.ShapeDtypeStruct((batch, CONV_W - 1, d), F32),
        ],
        scratch_shapes=[
            pltpu.VMEM((tb + SUBLANES, d), F32),
            pltpu.VMEM((tb, d), F32),
            pltpu.VMEM((tb, d), F32),
            pltpu.VMEM((SUBLANES, d), F32),
        ],
        compiler_params=_cparams("arbitrary", "arbitrary"),
        name="rg_prompt",
    )(proj3, proj3, *p_args)
    return mix, hl, cl


def _rg_sample_body(mix_ref, xw_ref, gate_ref, conv_ref, h0_ref, cw_ref, cb_ref, wa_ref, ba_ref,
                    wi_ref, bi_ref, lam_ref, o_ref, hn_ref, cn_ref, a_ref, b_ref):
    del mix_ref
    n, d = h0_ref.shape
    bw = wa_ref.shape[1]
    x = xw_ref[...].reshape(n, d)
    xc = cb_ref[...]
    for j in range(CONV_W - 1):
        xc = xc + conv_ref[:, j * d:(j + 1) * d] * cw_ref[j:j + 1, :]
    xc = xc + x * cw_ref[CONV_W - 1:CONV_W, :]

    def store(k, a, b):
        a_ref[:, k * bw:(k + 1) * bw] = a
        b_ref[:, k * bw:(k + 1) * bw] = b

    _rglru_coeffs(xc, wa_ref, ba_ref, wi_ref, bi_ref, lam_ref, store)
    h = b_ref[...] + a_ref[...] * h0_ref[...]
    hn_ref[...] = h
    o_ref[...] = (h * gate_ref[...].reshape(n, d)).reshape(o_ref.shape).astype(o_ref.dtype)
    cn_ref[:, 0:(CONV_W - 2) * d] = conv_ref[:, d:(CONV_W - 1) * d]
    cn_ref[:, (CONV_W - 2) * d:] = x


def _rg_sample(mix, proj, layer, conv_state, h0, params, *, tp, ts):
    p_args, p_specs = params
    n_layers, n, d = h0.shape
    tm = tp + ts
    proj3 = proj.reshape(N_TOKEN_TILES, tm, 2 * d)
    s_blk = tp // ts
    full2 = lambda shape: pl.BlockSpec(shape, lambda i: (0, 0))
    kc = (CONV_W - 1) * d
    conv_arg, conv_spec = _layer_full(conv_state.reshape(n_layers, n, kc), layer)
    h0_arg, h0_spec = _layer_full(h0, layer)
    return pl.pallas_call(
        _rg_sample_body,
        grid=(1,),
        in_specs=[
            pl.BlockSpec(memory_space=pl.ANY),
            pl.BlockSpec((N_TOKEN_TILES, ts, d), lambda i: (0, s_blk, 1)),
            pl.BlockSpec((N_TOKEN_TILES, ts, d), lambda i: (0, s_blk, 0)),
            conv_spec, h0_spec,
        ] + p_specs,
        out_specs=[
            pl.BlockSpec((N_TOKEN_TILES, ts, d), lambda i: (0, s_blk, 0)),
            full2((n, d)),
            full2((n, kc)),
        ],
        out_shape=[
            jax.ShapeDtypeStruct(mix.shape, mix.dtype),
            jax.ShapeDtypeStruct((n, d), F32),
            jax.ShapeDtypeStruct((n, kc), F32),
        ],
        scratch_shapes=[pltpu.VMEM((n, d), F32), pltpu.VMEM((n, d), F32)],
        input_output_aliases={0: 0},
        compiler_params=_cparams("arbitrary"),
        name="rg_sample",
    )(mix, proj3, proj3, conv_arg, h0_arg, *p_args)


def _gla_prompt_body(q_ref, k_ref, v_ref, r_ref, g_ref, on_ref, o_ref, s_ref, *, heads, nt, per_tile):
    t = pl.program_id(1)
    tb = q_ref.shape[1]
    c = GLA_CHUNK
    dk = q_ref.shape[2] // heads
    dv = v_ref.shape[2] // heads

    @pl.when(t == 0)
    def _():
        s_ref[...] = jnp.zeros_like(s_ref)

    row = lax.broadcasted_iota(jnp.int32, (tb, tb), 0)
    col = lax.broadcasted_iota(jnp.int32, (tb, tb), 1)
    same_chunk_causal = ((row >= col) & ((row ^ col) < c)).astype(BF16)
    g_rest = g_ref[0]
    bcum = jnp.zeros(g_rest.shape, F32)
    for _ in range(3):
        g_part = g_rest.astype(BF16)
        bcum = bcum + jnp.dot(same_chunk_causal, g_part, preferred_element_type=F32)
        g_rest = g_rest - g_part.astype(F32)
    q_i = (q_ref[0] * jnp.exp(bcum)).astype(BF16)
    k_i = (k_ref[0] * jnp.exp(-bcum)).astype(BF16)
    causal = (lax.broadcasted_iota(jnp.int32, (c, c), 0) >= lax.broadcasted_iota(jnp.int32, (c, c), 1))

    out_base = _prompt_rows_of_tile(o_ref, nt, per_tile, tb).start
    for ci in range(tb // c):
        rows = slice(ci * c, (ci + 1) * c)
        out_rows = pl.ds(pl.multiple_of(out_base + ci * c, c), c)
        b_last = bcum[(ci + 1) * c - 1:(ci + 1) * c, :]
        k_e = (k_ref[0, rows, :] * jnp.exp(b_last - bcum[rows, :])).astype(BF16)
        decay_t = jnp.exp(jnp.broadcast_to(b_last, (SUBLANES, heads * dk))).T
        for h in range(heads):
            kcols = slice(h * dk, (h + 1) * dk)
            vcols = slice(h * dv, (h + 1) * dv)
            qs = q_i[rows, kcols]
            v = v_ref[0, rows, vcols].astype(BF16)
            scores = lax.dot_general(qs, k_i[rows, kcols], (((1,), (1,)), ((), ())),
                                     preferred_element_type=F32)
            scores = jnp.where(causal, scores, 0.0).astype(BF16)
            s = s_ref[0, h]
            o = (jnp.dot(qs, s.astype(BF16), preferred_element_type=F32)
                 + jnp.dot(scores, v, preferred_element_type=F32))
            s_ref[0, h] = decay_t[kcols, 0:1] * s + lax.dot_general(
                k_e[:, kcols], v, (((0,), (0,)), ((), ())), preferred_element_type=F32)
            o = _rms_scale(o) * on_ref[...]
            o_ref[0, out_rows, vcols] = (o * r_ref[0, rows, vcols]).astype(o_ref.dtype)


def _gla_prompt(qkvr, g, layer, onorm, *, batch, seq, heads, dk, dv, tp, ts, tb=256):
    tm = tp + ts
    nt = seq // tb
    per_tile = tp // tb
    hk, hv = heads * dk, heads * dv
    assert hv == 2 * hk
    qkvr3 = qkvr.reshape(N_TOKEN_TILES, tm, qkvr.shape[1])
    g3 = g.reshape(N_TOKEN_TILES, tm, hk)

    def tok(col):
        def index(b, t):
            blk = b * nt + t
            return (blk // per_tile, blk % per_tile, col)
        return index

    return pl.pallas_call(
        functools.partial(_gla_prompt_body, heads=heads, nt=nt, per_tile=per_tile),
        grid=(batch, nt),
        in_specs=[
            pl.BlockSpec((1, tb, hk), tok(0)),
            pl.BlockSpec((1, tb, hk), tok(1)),
            pl.BlockSpec((1, tb, hv), tok(1)),
            pl.BlockSpec((1, tb, hv), tok(2)),
            pl.BlockSpec((1, tb, hk), tok(0)),
            _layer_vector(onorm, layer)[1],
        ],
        out_specs=[
            pl.BlockSpec((1, tm, hv), lambda b, t: ((b * nt + t) // per_tile, 0, 0)),
            pl.BlockSpec((1, heads, dk, dv), lambda b, t: (b, 0, 0, 0)),
        ],
        out_shape=[
            jax.ShapeDtypeStruct((N_TOKEN_TILES, tm, hv), BF16),
            jax.ShapeDtypeStruct((batch, heads, dk, dv), F32),
        ],
        compiler_params=_cparams("arbitrary", "arbitrary"),
        name="gla_prompt",
    )(qkvr3, qkvr3, qkvr3, qkvr3, g3, _layer_vector(onorm, layer)[0])


def _gla_sample_body(mix_ref, q_ref, k_ref, v_ref, r_ref, g_ref, s0_ref, on_ref, o_ref, s_ref):
    del mix_ref
    bb = q_ref.shape[1]
    alpha_t = jnp.exp(g_ref[0]).T
    k_t = k_ref[0].T
    q_t = q_ref[0].T
    v = v_ref[0]
    rows = []
    for b in range(bb):
        s_new = alpha_t[:, b:b + 1] * s0_ref[b, 0] + k_t[:, b:b + 1] * v[b:b + 1, :]
        s_ref[b, 0] = s_new
        rows.append(jnp.sum(q_t[:, b:b + 1] * s_new, axis=0, keepdims=True))
    o = jnp.concatenate(rows, axis=0)
    o = _rms_scale(o) * on_ref[...]
    o_ref[0] = (o * r_ref[0]).astype(o_ref.dtype)


def _gla_sample(mix, qkvr, g, layer, s0, onorm, *, heads, dk, dv, tp, ts):
    tm = tp + ts
    s_blk = tp // ts
    kq = heads
    vq = (2 * heads * dk) // dv
    rq = vq + heads
    qkvr3 = qkvr.reshape(N_TOKEN_TILES, tm, qkvr.shape[1])
    g3 = g.reshape(N_TOKEN_TILES, tm, heads * dk)
    return pl.pallas_call(
        _gla_sample_body,
        grid=(N_TOKEN_TILES, heads),
        in_specs=[
            pl.BlockSpec(memory_space=pl.ANY),
            pl.BlockSpec((1, ts, dk), lambda s, h: (s, s_blk, h)),
            pl.BlockSpec((1, ts, dk), lambda s, h: (s, s_blk, kq + h)),
            pl.BlockSpec((1, ts, dv), lambda s, h: (s, s_blk, vq + h)),
            pl.BlockSpec((1, ts, dv), lambda s, h: (s, s_blk, rq + h)),
            pl.BlockSpec((1, ts, dk), lambda s, h: (s, s_blk, h)),
            _layer_operand(s0, layer, (ts, 1, dk, dv), lambda s, h: (s, h, 0, 0))[1],
            _layer_vector(onorm, layer)[1],
        ],
        out_specs=[
            pl.BlockSpec((1, ts, dv), lambda s, h: (s, s_blk, h)),
            pl.BlockSpec((ts, 1, dk, dv), lambda s, h: (s, h, 0, 0)),
        ],
        out_shape=[
            jax.ShapeDtypeStruct(mix.shape, mix.dtype),
            jax.ShapeDtypeStruct(s0.shape[1:], F32),
        ],
        input_output_aliases={0: 0},
        compiler_params=_cparams("arbitrary", "arbitrary"),
        name="gla_sample",
    )(mix, qkvr3, qkvr3, qkvr3, qkvr3, g3, s0, _layer_vector(onorm, layer)[0])


def kernel(x_prompt, x_sample, state_rglru_h, state_rglru_conv, state_gla_S, ln_ffn1, ffn1_w1, ffn1_w3, ffn1_w2, ln_mix, ln_ffn2, ffn2_w1, ffn2_w3, ffn2_w2, rg_w_y, rg_w_x, rg_conv_w, rg_conv_b, rg_w_a, rg_b_a, rg_w_i, rg_b_i, rg_lambda, rg_w_o, gla_w_q, gla_w_k, gla_w_v, gla_w_g1, gla_w_g2, gla_b_g, gla_w_r, gla_onorm, gla_w_o, ln_final):
    batch, seq, d = x_prompt.shape
    n_dec = x_sample.shape[0]
    depth = ln_ffn1.shape[0]
    heads, dk, dv = state_gla_S.shape[2:]
    tp, ts = _token_tiles(batch * seq, n_dec)
    tm = tp + ts
    tiles = dict(tp=tp, ts=ts)

    x = (x_prompt.reshape(batch * seq, d), x_sample.reshape(n_dec, d))

    ffn1_w = [_to_bf16(w) for w in (ffn1_w1, ffn1_w3, ffn1_w2)]
    ffn2_w = [_to_bf16(w) for w in (ffn2_w1, ffn2_w3, ffn2_w2)]

    p_h, p_conv, p_s, s_h, s_conv, s_s = [], [], [], [], [], []
    for i in range(depth):
        j = i // 2
        x, u = _ffn(x, i, ln_ffn1, *ffn1_w, g_mix=ln_mix, **tiles)
        if i % 2 == 0:
            proj = _matmul(u, j, [rg_w_y, rg_w_x], [_gelu_tanh, _identity], tm=tm)
            params = _rg_params(j, rg_conv_w, rg_conv_b, rg_w_a, rg_b_a, rg_w_i, rg_b_i, rg_lambda)
            mix, hl, cl = _rg_prompt(proj, params, batch=batch, seq=seq, **tiles)
            mix, hn, cn = _rg_sample(mix, proj, j, state_rglru_conv, state_rglru_h, params, **tiles)
            p_h.append(hl.reshape(batch, d))
            p_conv.append(cl)
            s_h.append(hn)
            s_conv.append(cn.reshape(n_dec, CONV_W - 1, d))
            w_o = rg_w_o
        else:
            scale = dk ** -0.5
            qkvr = _matmul(u, j, [gla_w_q, gla_w_k, gla_w_v, gla_w_r],
                           [lambda y: y * scale, _identity, _identity, _silu], tm=tm)
            rank = gla_w_g1.shape[2]
            w_g1 = jnp.pad(gla_w_g1[j], ((0, 0), (0, LANES - rank)))
            w_g2 = jnp.pad(gla_w_g2[j], ((0, LANES - rank), (0, 0)))
            g = _gla_gate(u, w_g1, w_g2, gla_b_g[j], tm=tm)
            mix, s_fin = _gla_prompt(qkvr, g, j, gla_onorm, batch=batch, seq=seq, heads=heads,
                                     dk=dk, dv=dv, **tiles)
            mix, s_new = _gla_sample(mix, qkvr, g, j, state_gla_S, gla_onorm, heads=heads,
                                     dk=dk, dv=dv, **tiles)
            p_s.append(s_fin)
            s_s.append(s_new)
            w_o = gla_w_o
        x = _matmul(mix.reshape(N_TOKEN_TILES * tm, d), j, [w_o], [_identity], res=x, tm=tm)
        g_final = ln_final.reshape(1, d) if i == depth - 1 else None
        tf = 512 if g_final is None else 256
        x = _ffn(x, i, ln_ffn2, *ffn2_w, g_final=g_final, tf=tf, **tiles)

    y_prompt = x[0].reshape(batch, seq, d)
    y_sample = x[1].reshape(n_dec, 1, d)
    return (y_prompt, y_sample, jnp.stack(p_h), jnp.stack(p_conv), jnp.stack(p_s),
            jnp.stack(s_h), jnp.stack(s_conv), jnp.stack(s_s))
```

```python
import functools

import jax
import jax.numpy as jnp
from jax import lax
from jax.experimental import pallas as pl
from jax.experimental.pallas import tpu as pltpu

F32 = jnp.float32
BF16 = jnp.bfloat16

RMS_EPS = 1e-6
RG_C = 8.0
CONV_W = 4
GLA_GATE_NORM = 16.0
GLA_CHUNK = 64

V7X_VMEM_LIMIT_BYTES = 62 * 1024 * 1024
SUBLANES = 8
LANES = 128
BF16_SUBLANES = 16
N_TOKEN_TILES = 8


def _cparams(*sem):
    return pltpu.CompilerParams(dimension_semantics=sem, vmem_limit_bytes=V7X_VMEM_LIMIT_BYTES)


def _rms_scale(x):
    return x * lax.rsqrt(jnp.mean(x * x, axis=-1, keepdims=True) + RMS_EPS)


def _silu(x):
    return x * jax.nn.sigmoid(x)


def _gelu_tanh(y):
    return jax.nn.gelu(y, approximate=True)


def _identity(y):
    return y


def _softplus(x):
    return jnp.maximum(x, 0.0) + jnp.log1p(jnp.exp(-jnp.abs(x)))


def _neg_expm1(z):
    t = jnp.tanh(0.5 * z)
    return -2.0 * t / (1.0 - t)


ROW_CHUNK = BF16_SUBLANES


def _for_row_chunks(n_rows, body):
    n = n_rows // ROW_CHUNK
    assert n * ROW_CHUNK == n_rows
    unroll = next(u for u in (8, 5, 4, 3, 2, 1) if n % u == 0)

    def step(c, carry):
        body(pl.multiple_of(c * ROW_CHUNK, ROW_CHUNK))
        return carry

    lax.fori_loop(0, n, step, 0, unroll=unroll)


def _chunk_rows(base, off):
    assert base % ROW_CHUNK == 0
    return pl.ds(pl.multiple_of(base + off, ROW_CHUNK), ROW_CHUNK)


def _layer_operand(stacked, layer, block, index_map):
    spec = pl.BlockSpec((None,) + tuple(block), lambda *g: (layer,) + tuple(index_map(*g)))
    return stacked, spec


def _layer_vector(stacked, layer):
    n_layers, n = stacked.shape
    return _layer_operand(stacked.reshape(n_layers, 1, n), layer, (1, n), lambda *g: (0, 0))


def _layer_full(stacked, layer):
    block = stacked.shape[1:]
    return _layer_operand(stacked, layer, block, lambda *g: (0,) * len(block))


def _token_tiles(n_prompt, n_sample):
    tp, ts = n_prompt // N_TOKEN_TILES, n_sample // N_TOKEN_TILES
    assert tp * N_TOKEN_TILES == n_prompt and ts * N_TOKEN_TILES == n_sample
    assert tp % BF16_SUBLANES == 0 and ts % BF16_SUBLANES == 0 and tp % ts == 0
    return tp, ts


def _cast_body(w_ref, o_ref):
    o_ref[...] = w_ref[...].astype(o_ref.dtype)


def _to_bf16(w, layer, *, rows=512):
    _, r, c = w.shape
    return pl.pallas_call(
        _cast_body,
        grid=(r // rows,),
        in_specs=[pl.BlockSpec((None, rows, c), lambda i: (layer, i, 0))],
        out_specs=pl.BlockSpec((rows, c), lambda i: (i, 0)),
        out_shape=jax.ShapeDtypeStruct((r, c), BF16),
        compiler_params=_cparams("arbitrary"),
        name="to_bf16",
    )(w)


def _ffn_body(*refs, tp, ts, split_in, split_out, mix_norm, cast_next):
    refs = list(refs)
    x_refs = [refs.pop(0) for _ in range(2 if split_in else 1)]
    g_ref, w1_ref, w3_ref, w2_ref = (refs.pop(0) for _ in range(4))
    g2_ref = refs.pop(0) if (split_out or mix_norm) else None
    next_f32 = [refs.pop(0) for _ in range(3 if cast_next else 0)]
    o_refs = [refs.pop(0) for _ in range(2 if split_out else 1)]
    u_hbm = refs.pop(0) if mix_norm else None
    next_bf16 = [refs.pop(0) for _ in range(3 if cast_next else 0)]
    xn_ref = refs.pop(0)
    acc_ref = refs.pop(0) if split_out else o_refs[0]
    (sem,) = refs
    tm = tp + ts
    i = pl.program_id(0)
    f = pl.program_id(1)

    def segments(blocks):
        if len(blocks) == 2:
            return [(0, tp, blocks[0]), (tp, ts, blocks[1])]
        return [(0, tm, blocks[0])]

    @pl.when(f == 0)
    def _():
        for t0, n_rows, x_ref in segments(x_refs):
            def chunk(off, x_ref=x_ref, t0=t0):
                x = x_ref[_chunk_rows(0, off), :]
                xn_ref[_chunk_rows(t0, off), :] = (_rms_scale(x) * g_ref[...]).astype(BF16)
                acc_ref[_chunk_rows(t0, off), :] = 2.0 * x

            _for_row_chunks(n_rows, chunk)

    xn = xn_ref[...]
    h1 = jnp.dot(xn, w1_ref[...].astype(BF16), preferred_element_type=F32)
    h3 = jnp.dot(xn, w3_ref[...].astype(BF16), preferred_element_type=F32)
    h = (_silu(h1) * h3).astype(BF16)
    acc_ref[...] += jnp.dot(h, w2_ref[...].astype(BF16), preferred_element_type=F32)

    for src_ref, dst_ref in zip(next_f32, next_bf16):
        dst_ref[...] = src_ref[...].astype(BF16)

    @pl.when(f == pl.num_programs(1) - 1)
    def _():
        for t0, n_rows, o_ref in segments(o_refs):
            def chunk(off, o_ref=o_ref, t0=t0):
                y = 0.5 * acc_ref[_chunk_rows(t0, off), :]
                if split_out:
                    y = _rms_scale(y) * g2_ref[...]
                o_ref[_chunk_rows(0, off), :] = y
                if mix_norm:
                    xn_ref[_chunk_rows(t0, off), :] = (_rms_scale(y) * g2_ref[...]).astype(BF16)

            _for_row_chunks(n_rows, chunk)
        if mix_norm:
            cu = pltpu.make_async_copy(xn_ref, u_hbm.at[pl.ds(i * tm, tm)], sem.at[0])
            cu.start()
            cu.wait()


def _ffn(x, layer, g, weights, *, tp, ts, g_mix=None, g_final=None, next_weights=None, tf=512):
    split_in = isinstance(x, tuple)
    split_out = g_final is not None
    mix_norm = g_mix is not None
    cast_next = next_weights is not None
    assert not (split_out and mix_norm)
    xs = list(x) if split_in else [x]
    w1, w3, w2 = weights
    d, f = w1.shape
    tm = tp + ts
    nf = f // tf
    operands = [
        _layer_vector(g, layer),
        (w1, pl.BlockSpec((d, tf), lambda i, j: (0, j))),
        (w3, pl.BlockSpec((d, tf), lambda i, j: (0, j))),
        (w2, pl.BlockSpec((tf, d), lambda i, j: (j, 0))),
    ]
    if mix_norm:
        operands.append(_layer_vector(g_mix, layer))
    if split_out:
        operands.append(_layer_vector(g_final, 0))
    row_tile = lambda rows: pl.BlockSpec((rows, d), lambda i, j: (i, 0))
    x_specs = [row_tile(tp), row_tile(ts)] if split_in else [row_tile(tm)]
    if split_out:
        out_specs = [row_tile(tp), row_tile(ts)]
        out_shape = [jax.ShapeDtypeStruct((tp * N_TOKEN_TILES, d), F32),
                     jax.ShapeDtypeStruct((ts * N_TOKEN_TILES, d), F32)]
    else:
        out_specs = [row_tile(tm)]
        out_shape = [jax.ShapeDtypeStruct((tm * N_TOKEN_TILES, d), F32)]
    if mix_norm:
        out_specs.append(pl.BlockSpec(memory_space=pl.ANY))
        out_shape.append(jax.ShapeDtypeStruct((tm * N_TOKEN_TILES, d), BF16))
    if cast_next:
        (n1, n3, n2), next_layer = next_weights
        db = d // N_TOKEN_TILES
        assert n1.shape[1:] == (d, f) and n2.shape[1:] == (f, d) and db * N_TOKEN_TILES == d
        operands += [
            _layer_operand(n1, next_layer, (db, tf), lambda i, j: (i, j)),
            _layer_operand(n3, next_layer, (db, tf), lambda i, j: (i, j)),
            _layer_operand(n2, next_layer, (tf, db), lambda i, j: (j, i)),
        ]
        out_specs += [pl.BlockSpec((db, tf), lambda i, j: (i, j)), pl.BlockSpec((db, tf), lambda i, j: (i, j)),
                      pl.BlockSpec((tf, db), lambda i, j: (j, i))]
        out_shape += [jax.ShapeDtypeStruct((d, f), BF16), jax.ShapeDtypeStruct((d, f), BF16),
                      jax.ShapeDtypeStruct((f, d), BF16)]
    out = pl.pallas_call(
        functools.partial(_ffn_body, tp=tp, ts=ts, split_in=split_in, split_out=split_out,
                          mix_norm=mix_norm, cast_next=cast_next),
        grid=(N_TOKEN_TILES, nf),
        in_specs=x_specs + [spec for _, spec in operands],
        out_specs=out_specs,
        out_shape=out_shape,
        scratch_shapes=[pltpu.VMEM((tm, d), BF16)] + ([pltpu.VMEM((tm, d), F32)] if split_out else [])
        + [pltpu.SemaphoreType.DMA((1,))],
        compiler_params=_cparams("arbitrary", "arbitrary"),
        name="ffn",
    )(*xs, *[arr for arr, _ in operands])
    out = list(out)
    acts = tuple(out[:2]) if split_out else out[0]
    rest = out[2:] if split_out else out[1:]
    u = rest.pop(0) if mix_norm else None
    return acts, u, (tuple(rest) if cast_next else None)


def _matmul_body(u_ref, *refs, tiles, layer, has_res):
    n_src = 1 + max(src for src, _, _ in tiles)
    w_hbm = refs[:n_src]
    rest = refs[n_src:]
    if has_res:
        r_ref, o_ref, stage_ref, wb_ref, sem = rest
    else:
        o_ref, stage_ref, wb_ref, sem = rest
    tn = stage_ref.shape[1]
    j = pl.program_id(0)
    i = pl.program_id(1)

    def weight_copy(jj):
        src, col0, _ = tiles[jj]
        return pltpu.make_async_copy(w_hbm[src].at[layer, :, pl.ds(col0, tn)], stage_ref, sem.at[0])

    def at_tile(jj, fn):
        pl.when(j == jj)(fn)

    @pl.when(i == 0)
    def _():
        at_tile(0, lambda: weight_copy(0).start())
        for jj in range(len(tiles)):
            at_tile(jj, lambda jj=jj: weight_copy(jj).wait())
        wb_ref[...] = stage_ref[...].astype(BF16)
        for jj in range(len(tiles) - 1):
            at_tile(jj, lambda jj=jj: weight_copy(jj + 1).start())

    z = jnp.dot(u_ref[...], wb_ref[...], preferred_element_type=F32)
    for act in dict.fromkeys(act for _, _, act in tiles):
        uses = functools.reduce(jnp.logical_or, [j == jj for jj, t in enumerate(tiles) if t[2] is act])

        @pl.when(uses)
        def _(act=act):
            y = act(z)
            if has_res:
                y = r_ref[...] + y
            o_ref[...] = y.astype(o_ref.dtype)


def _matmul(u, layer, ws, acts, res=None, *, tm, tn=1024, out_dtype=F32):
    m, d = u.shape
    tiles = []
    for src, (w, act) in enumerate(zip(ws, acts)):
        assert w.shape[2] % tn == 0
        tiles += [(src, col0, act) for col0 in range(0, w.shape[2], tn)]
    n = len(tiles) * tn
    io_spec = lambda: pl.BlockSpec((tm, tn), lambda j, i: (i, j))
    has_res = res is not None
    return pl.pallas_call(
        functools.partial(_matmul_body, tiles=tuple(tiles), layer=layer, has_res=has_res),
        grid=(len(tiles), m // tm),
        in_specs=[pl.BlockSpec((tm, d), lambda j, i: (i, 0))]
        + [pl.BlockSpec(memory_space=pl.ANY)] * len(ws) + ([io_spec()] if has_res else []),
        out_specs=io_spec(),
        out_shape=jax.ShapeDtypeStruct((m, n), out_dtype),
        scratch_shapes=[pltpu.VMEM((d, tn), F32), pltpu.VMEM((d, tn), BF16),
                        pltpu.SemaphoreType.DMA((1,))],
        compiler_params=_cparams("arbitrary", "arbitrary"),
        name="matmul",
    )(u, *ws, *([res] if has_res else []))


def _gla_gate_body(u_ref, w1_ref, w2_ref, b_ref, o_ref):
    t = jnp.dot(u_ref[...], w1_ref[...].astype(BF16), preferred_element_type=F32)
    logit = jnp.dot(t.astype(BF16), w2_ref[...].astype(BF16), preferred_element_type=F32) + b_ref[...]
    o_ref[...] = -_softplus(-logit) * (1.0 / GLA_GATE_NORM)


def _gla_gate(u, w_g1, w_g2, b_g, *, tm):
    m, d = u.shape
    rank, n = w_g2.shape
    return pl.pallas_call(
        _gla_gate_body,
        grid=(m // tm,),
        in_specs=[
            pl.BlockSpec((tm, d), lambda i: (i, 0)),
            pl.BlockSpec((d, rank), lambda i: (0, 0)),
            pl.BlockSpec((rank, n), lambda i: (0, 0)),
            pl.BlockSpec((1, n), lambda i: (0, 0)),
        ],
        out_specs=pl.BlockSpec((tm, n), lambda i: (i, 0)),
        out_shape=jax.ShapeDtypeStruct((m, n), F32),
        compiler_params=_cparams("arbitrary"),
        name="gla_gate",
    )(u, w_g1, w_g2, b_g.reshape(1, n))


def _prompt_rows_of_tile(o_ref, nt, per_tile, tb):
    blk = pl.program_id(0) * nt + pl.program_id(1)
    k = blk % per_tile
    tp = per_tile * tb

    @pl.when(k == 0)
    def _():
        o_ref[0, tp:, :] = jnp.zeros((o_ref.shape[1] - tp, o_ref.shape[2]), o_ref.dtype)

    return pl.ds(pl.multiple_of(k * tb, tb), tb)


def _rglru_coeffs(xc, wa_ref, ba_ref, wi_ref, bi_ref, lam_ref, store):
    n_blocks, bw, _ = wa_ref.shape
    for n in range(n_blocks):
        cols = slice(n * bw, (n + 1) * bw)
        xb = xc[:, cols]
        xb16 = xb.astype(BF16)
        wa = wa_ref[n].astype(BF16)
        wi = wi_ref[n].astype(BF16)
        r = jax.nn.sigmoid(jnp.dot(xb16, wa, preferred_element_type=F32) + ba_ref[:, cols])
        i = jax.nn.sigmoid(jnp.dot(xb16, wi, preferred_element_type=F32) + bi_ref[:, cols])
        log_a = -RG_C * r * _softplus(-lam_ref[:, cols])
        a = jnp.exp(log_a)
        b = jnp.sqrt(_neg_expm1(2.0 * log_a)) * (i * xb)
        store(n, a, b)


def _rg_prompt_body(xw_ref, gate_ref, cw_ref, cb_ref, wa_ref, ba_ref, wi_ref, bi_ref, lam_ref,
                    o_ref, hl_ref, cl_ref, ext_ref, a_ref, b_ref, h_ref, *, nt, per_tile):
    t = pl.program_id(1)
    tb = xw_ref.shape[1]
    bw = wa_ref.shape[1]
    pad = SUBLANES

    @pl.when(t == 0)
    def _():
        ext_ref[0:pad, :] = jnp.zeros((pad, ext_ref.shape[1]), F32)
        h_ref[...] = jnp.zeros_like(h_ref)

    ext_ref[pad:pad + tb, :] = xw_ref[0]
    xc = cb_ref[...]
    for j in range(CONV_W):
        off = pad - (CONV_W - 1) + j
        xc = xc + ext_ref[off:off + tb, :] * cw_ref[j:j + 1, :]

    def store(n, a, b):
        a_ref[:, n * bw:(n + 1) * bw] = a
        b_ref[:, n * bw:(n + 1) * bw] = b

    _rglru_coeffs(xc, wa_ref, ba_ref, wi_ref, bi_ref, lam_ref, store)

    def step(r, h):
        h = a_ref[pl.ds(r, 1), :] * h + b_ref[pl.ds(r, 1), :]
        b_ref[pl.ds(r, 1), :] = h
        return h

    h = lax.fori_loop(0, tb, step, h_ref[0:1, :], unroll=8)
    h_ref[0:1, :] = h
    out_rows = _prompt_rows_of_tile(o_ref, nt, per_tile, tb)
    o_ref[0, out_rows, :] = (b_ref[...] * gate_ref[0]).astype(o_ref.dtype)
    hl_ref[0] = h
    cl_ref[0] = ext_ref[pad + tb - (CONV_W - 1):pad + tb, :]
    ext_ref[0:pad, :] = ext_ref[tb:tb + pad, :]


def _rg_params(layer, conv_w, conv_b, w_a, b_a, w_i, b_i, lam):
    operands = [_layer_full(conv_w, layer), _layer_vector(conv_b, layer), _layer_full(w_a, layer),
                _layer_vector(b_a, layer), _layer_full(w_i, layer), _layer_vector(b_i, layer),
                _layer_vector(lam, layer)]
    return [arr for arr, _ in operands], [spec for _, spec in operands]


def _rg_prompt(proj, params, *, batch, seq, tp, ts, tb=256):
    p_args, p_specs = params
    m, d2 = proj.shape
    d = d2 // 2
    tm = tp + ts
    nt = seq // tb
    per_tile = tp // tb
    proj3 = proj.reshape(N_TOKEN_TILES, tm, d2)

    def tok(col):
        def index(b, t):
            blk = b * nt + t
            return (blk // per_tile, blk % per_tile, col)
        return index

    mix, hl, cl = pl.pallas_call(
        functools.partial(_rg_prompt_body, nt=nt, per_tile=per_tile),
        grid=(batch, nt),
        in_specs=[pl.BlockSpec((1, tb, d), tok(1)), pl.BlockSpec((1, tb, d), tok(0))] + p_specs,
        out_specs=[
            pl.BlockSpec((1, tm, d), lambda b, t: ((b * nt + t) // per_tile, 0, 0)),
            pl.BlockSpec((1, 1, d), lambda b, t: (b, 0, 0)),
            pl.BlockSpec((1, CONV_W - 1, d), lambda b, t: (b, 0, 0)),
        ],
        out_shape=[
            jax.ShapeDtypeStruct((N_TOKEN_TILES, tm, d), BF16),
            jax.ShapeDtypeStruct((batch, 1, d), F32),
            jax.ShapeDtypeStruct((batch, CONV_W - 1, d), F32),
        ],
        scratch_shapes=[
            pltpu.VMEM((tb + SUBLANES, d), F32),
            pltpu.VMEM((tb, d), F32),
            pltpu.VMEM((tb, d), F32),
            pltpu.VMEM((SUBLANES, d), F32),
        ],
        compiler_params=_cparams("arbitrary", "arbitrary"),
        name="rg_prompt",
    )(proj3, proj3, *p_args)
    return mix, hl, cl


def _rg_sample_body(mix_ref, xw_ref, gate_ref, conv_ref, h0_ref, cw_ref, cb_ref, wa_ref, ba_ref,
                    wi_ref, bi_ref, lam_ref, o_ref, hn_ref, cn_ref, a_ref, b_ref):
    del mix_ref
    n, d = h0_ref.shape
    bw = wa_ref.shape[1]
    x = xw_ref[...].reshape(n, d)
    xc = cb_ref[...]
    for j in range(CONV_W - 1):
        xc = xc + conv_ref[:, j * d:(j + 1) * d] * cw_ref[j:j + 1, :]
    xc = xc + x * cw_ref[CONV_W - 1:CONV_W, :]

    def store(k, a, b):
        a_ref[:, k * bw:(k + 1) * bw] = a
        b_ref[:, k * bw:(k + 1) * bw] = b

    _rglru_coeffs(xc, wa_ref, ba_ref, wi_ref, bi_ref, lam_ref, store)
    h = b_ref[...] + a_ref[...] * h0_ref[...]
    hn_ref[...] = h
    o_ref[...] = (h * gate_ref[...].reshape(n, d)).reshape(o_ref.shape).astype(o_ref.dtype)
    cn_ref[:, 0:(CONV_W - 2) * d] = conv_ref[:, d:(CONV_W - 1) * d]
    cn_ref[:, (CONV_W - 2) * d:] = x


def _rg_sample(mix, proj, layer, conv_state, h0, params, *, tp, ts):
    p_args, p_specs = params
    n_layers, n, d = h0.shape
    tm = tp + ts
    proj3 = proj.reshape(N_TOKEN_TILES, tm, 2 * d)
    s_blk = tp // ts
    full2 = lambda shape: pl.BlockSpec(shape, lambda i: (0, 0))
    kc = (CONV_W - 1) * d
    conv_arg, conv_spec = _layer_full(conv_state.reshape(n_layers, n, kc), layer)
    h0_arg, h0_spec = _layer_full(h0, layer)
    return pl.pallas_call(
        _rg_sample_body,
        grid=(1,),
        in_specs=[
            pl.BlockSpec(memory_space=pl.ANY),
            pl.BlockSpec((N_TOKEN_TILES, ts, d), lambda i: (0, s_blk, 1)),
            pl.BlockSpec((N_TOKEN_TILES, ts, d), lambda i: (0, s_blk, 0)),
            conv_spec, h0_spec,
        ] + p_specs,
        out_specs=[
            pl.BlockSpec((N_TOKEN_TILES, ts, d), lambda i: (0, s_blk, 0)),
            full2((n, d)),
            full2((n, kc)),
        ],
        out_shape=[
            jax.ShapeDtypeStruct(mix.shape, mix.dtype),
            jax.ShapeDtypeStruct((n, d), F32),
            jax.ShapeDtypeStruct((n, kc), F32),
        ],
        scratch_shapes=[pltpu.VMEM((n, d), F32), pltpu.VMEM((n, d), F32)],
        input_output_aliases={0: 0},
        compiler_params=_cparams("arbitrary"),
        name="rg_sample",
    )(mix, proj3, proj3, conv_arg, h0_arg, *p_args)


def _gla_prompt_body(q_ref, k_ref, v_ref, r_ref, g_ref, on_ref, o_ref, s_ref, *, heads, nt, per_tile):
    t = pl.program_id(1)
    tb = q_ref.shape[1]
    c = GLA_CHUNK
    dk = q_ref.shape[2] // heads
    dv = v_ref.shape[2] // heads

    @pl.when(t == 0)
    def _():
        s_ref[...] = jnp.zeros_like(s_ref)

    row = lax.broadcasted_iota(jnp.int32, (tb, tb), 0)
    col = lax.broadcasted_iota(jnp.int32, (tb, tb), 1)
    same_chunk_causal = ((row >= col) & ((row ^ col) < c)).astype(BF16)
    g_rest = g_ref[0]
    bcum = jnp.zeros(g_rest.shape, F32)
    for _ in range(3):
        g_part = g_rest.astype(BF16)
        bcum = bcum + jnp.dot(same_chunk_causal, g_part, preferred_element_type=F32)
        g_rest = g_rest - g_part.astype(F32)
    q_i = (q_ref[0].astype(F32) * dk ** -0.5 * jnp.exp(bcum)).astype(BF16)
    k_i = (k_ref[0].astype(F32) * jnp.exp(-bcum)).astype(BF16)
    causal = (lax.broadcasted_iota(jnp.int32, (c, c), 0) >= lax.broadcasted_iota(jnp.int32, (c, c), 1))

    out_base = _prompt_rows_of_tile(o_ref, nt, per_tile, tb).start
    for ci in range(tb // c):
        rows = slice(ci * c, (ci + 1) * c)
        out_rows = pl.ds(pl.multiple_of(out_base + ci * c, c), c)
        b_last = bcum[(ci + 1) * c - 1:(ci + 1) * c, :]
        k_e = (k_ref[0, rows, :].astype(F32) * jnp.exp(b_last - bcum[rows, :])).astype(BF16)
        decay_t = jnp.exp(jnp.broadcast_to(b_last, (SUBLANES, heads * dk))).T
        for h in range(heads):
            kcols = slice(h * dk, (h + 1) * dk)
            vcols = slice(h * dv, (h + 1) * dv)
            qs = q_i[rows, kcols]
            v = v_ref[0, rows, vcols]
            scores = lax.dot_general(qs, k_i[rows, kcols], (((1,), (1,)), ((), ())),
                                     preferred_element_type=F32)
            scores = jnp.where(causal, scores, 0.0).astype(BF16)
            s = s_ref[0, h]
            o = (jnp.dot(qs, s.astype(BF16), preferred_element_type=F32)
                 + jnp.dot(scores, v, preferred_element_type=F32))
            s_ref[0, h] = decay_t[kcols, 0:1] * s + lax.dot_general(
                k_e[:, kcols], v, (((0,), (0,)), ((), ())), preferred_element_type=F32)
            o = _rms_scale(o) * on_ref[...]
            o_ref[0, out_rows, vcols] = (o * r_ref[0, rows, vcols].astype(F32)).astype(o_ref.dtype)


def _gla_prompt(qkvr, g, layer, onorm, *, batch, seq, heads, dk, dv, tp, ts, tb=256):
    tm = tp + ts
    nt = seq // tb
    per_tile = tp // tb
    hk, hv = heads * dk, heads * dv
    assert hv == 2 * hk
    qkvr3 = qkvr.reshape(N_TOKEN_TILES, tm, qkvr.shape[1])
    g3 = g.reshape(N_TOKEN_TILES, tm, hk)

    def tok(col):
        def index(b, t):
            blk = b * nt + t
            return (blk // per_tile, blk % per_tile, col)
        return index

    return pl.pallas_call(
        functools.partial(_gla_prompt_body, heads=heads, nt=nt, per_tile=per_tile),
        grid=(batch, nt),
        in_specs=[
            pl.BlockSpec((1, tb, hk), tok(0)),
            pl.BlockSpec((1, tb, hk), tok(1)),
            pl.BlockSpec((1, tb, hv), tok(1)),
            pl.BlockSpec((1, tb, hv), tok(2)),
            pl.BlockSpec((1, tb, hk), tok(0)),
            _layer_vector(onorm, layer)[1],
        ],
        out_specs=[
            pl.BlockSpec((1, tm, hv), lambda b, t: ((b * nt + t) // per_tile, 0, 0)),
            pl.BlockSpec((1, heads, dk, dv), lambda b, t: (b, 0, 0, 0)),
        ],
        out_shape=[
            jax.ShapeDtypeStruct((N_TOKEN_TILES, tm, hv), BF16),
            jax.ShapeDtypeStruct((batch, heads, dk, dv), F32),
        ],
        compiler_params=_cparams("arbitrary", "arbitrary"),
        name="gla_prompt",
    )(qkvr3, qkvr3, qkvr3, qkvr3, g3, _layer_vector(onorm, layer)[0])


def _gla_sample_body(mix_ref, q_ref, k_ref, v_ref, r_ref, g_ref, s0_ref, on_ref, o_ref, s_ref):
    del mix_ref
    bb, dk = q_ref.shape[1:]
    alpha_t = jnp.exp(g_ref[0]).T
    k_t = k_ref[0].astype(F32).T
    q_t = (q_ref[0].astype(F32) * dk ** -0.5).T
    v = v_ref[0].astype(F32)
    rows = []
    for b in range(bb):
        s_new = alpha_t[:, b:b + 1] * s0_ref[b, 0] + k_t[:, b:b + 1] * v[b:b + 1, :]
        s_ref[b, 0] = s_new
        rows.append(jnp.sum(q_t[:, b:b + 1] * s_new, axis=0, keepdims=True))
    o = jnp.concatenate(rows, axis=0)
    o = _rms_scale(o) * on_ref[...]
    o_ref[0] = (o * r_ref[0].astype(F32)).astype(o_ref.dtype)


def _gla_sample(mix, qkvr, g, layer, s0, onorm, *, heads, dk, dv, tp, ts):
    tm = tp + ts
    s_blk = tp // ts
    kq = heads
    vq = (2 * heads * dk) // dv
    rq = vq + heads
    qkvr3 = qkvr.reshape(N_TOKEN_TILES, tm, qkvr.shape[1])
    g3 = g.reshape(N_TOKEN_TILES, tm, heads * dk)
    return pl.pallas_call(
        _gla_sample_body,
        grid=(N_TOKEN_TILES, heads),
        in_specs=[
            pl.BlockSpec(memory_space=pl.ANY),
            pl.BlockSpec((1, ts, dk), lambda s, h: (s, s_blk, h)),
            pl.BlockSpec((1, ts, dk), lambda s, h: (s, s_blk, kq + h)),
            pl.BlockSpec((1, ts, dv), lambda s, h: (s, s_blk, vq + h)),
            pl.BlockSpec((1, ts, dv), lambda s, h: (s, s_blk, rq + h)),
            pl.BlockSpec((1, ts, dk), lambda s, h: (s, s_blk, h)),
            _layer_operand(s0, layer, (ts, 1, dk, dv), lambda s, h: (s, h, 0, 0))[1],
            _layer_vector(onorm, layer)[1],
        ],
        out_specs=[
            pl.BlockSpec((1, ts, dv), lambda s, h: (s, s_blk, h)),
            pl.BlockSpec((ts, 1, dk, dv), lambda s, h: (s, h, 0, 0)),
        ],
        out_shape=[
            jax.ShapeDtypeStruct(mix.shape, mix.dtype),
            jax.ShapeDtypeStruct(s0.shape[1:], F32),
        ],
        input_output_aliases={0: 0},
        compiler_params=_cparams("arbitrary", "arbitrary"),
        name="gla_sample",
    )(mix, qkvr3, qkvr3, qkvr3, qkvr3, g3, s0, _layer_vector(onorm, layer)[0])


def kernel(x_prompt, x_sample, state_rglru_h, state_rglru_conv, state_gla_S, ln_ffn1, ffn1_w1, ffn1_w3, ffn1_w2, ln_mix, ln_ffn2, ffn2_w1, ffn2_w3, ffn2_w2, rg_w_y, rg_w_x, rg_conv_w, rg_conv_b, rg_w_a, rg_b_a, rg_w_i, rg_b_i, rg_lambda, rg_w_o, gla_w_q, gla_w_k, gla_w_v, gla_w_g1, gla_w_g2, gla_b_g, gla_w_r, gla_onorm, gla_w_o, ln_final):
    batch, seq, d = x_prompt.shape
    n_dec = x_sample.shape[0]
    depth = ln_ffn1.shape[0]
    heads, dk, dv = state_gla_S.shape[2:]
    tp, ts = _token_tiles(batch * seq, n_dec)
    tm = tp + ts
    tiles = dict(tp=tp, ts=ts)

    x = (x_prompt.reshape(batch * seq, d), x_sample.reshape(n_dec, d))

    ffn_f32 = []
    for i in range(depth):
        ffn_f32 += [((ffn1_w1, ffn1_w3, ffn1_w2), i), ((ffn2_w1, ffn2_w3, ffn2_w2), i)]
    ffn_f32.append(None)
    ffn_w = tuple(_to_bf16(w, 0) for w in ffn_f32[0][0])

    p_h, p_conv, p_s, s_h, s_conv, s_s = [], [], [], [], [], []
    for i in range(depth):
        j = i // 2
        x, u, ffn_w = _ffn(x, i, ln_ffn1, ffn_w, g_mix=ln_mix, next_weights=ffn_f32[2 * i + 1], **tiles)
        if i % 2 == 0:
            proj = _matmul(u, j, [rg_w_y, rg_w_x], [_gelu_tanh, _identity], tm=tm)
            params = _rg_params(j, rg_conv_w, rg_conv_b, rg_w_a, rg_b_a, rg_w_i, rg_b_i, rg_lambda)
            mix, hl, cl = _rg_prompt(proj, params, batch=batch, seq=seq, **tiles)
            mix, hn, cn = _rg_sample(mix, proj, j, state_rglru_conv, state_rglru_h, params, **tiles)
            p_h.append(hl.reshape(batch, d))
            p_conv.append(cl)
            s_h.append(hn)
            s_conv.append(cn.reshape(n_dec, CONV_W - 1, d))
            w_o = rg_w_o
        else:
            qkvr = _matmul(u, j, [gla_w_q, gla_w_k, gla_w_v, gla_w_r],
                           [_identity, _identity, _identity, _silu], tm=tm, out_dtype=BF16)
            rank = gla_w_g1.shape[2]
            w_g1 = jnp.pad(gla_w_g1[j], ((0, 0), (0, LANES - rank)))
            w_g2 = jnp.pad(gla_w_g2[j], ((0, LANES - rank), (0, 0)))
            g = _gla_gate(u, w_g1, w_g2, gla_b_g[j], tm=tm)
            mix, s_fin = _gla_prompt(qkvr, g, j, gla_onorm, batch=batch, seq=seq, heads=heads,
                                     dk=dk, dv=dv, **tiles)
            mix, s_new = _gla_sample(mix, qkvr, g, j, state_gla_S, gla_onorm, heads=heads,
                                     dk=dk, dv=dv, **tiles)
            p_s.append(s_fin)
            s_s.append(s_new)
            w_o = gla_w_o
        x = _matmul(mix.reshape(N_TOKEN_TILES * tm, d), j, [w_o], [_identity], res=x, tm=tm)
        g_final = ln_final.reshape(1, d) if i == depth - 1 else None
        tf = 512 if g_final is None else 256
        x, _, ffn_w = _ffn(x, i, ln_ffn2, ffn_w, g_final=g_final, next_weights=ffn_f32[2 * i + 2], tf=tf,
                           **tiles)

    y_prompt = x[0].reshape(batch, seq, d)
    y_sample = x[1].reshape(n_dec, 1, d)
    return (y_prompt, y_sample, jnp.stack(p_h), jnp.stack(p_conv), jnp.stack(p_s),
            jnp.stack(s_h), jnp.stack(s_conv), jnp.stack(s_s))
```

```python
import functools

import jax
import jax.numpy as jnp
from jax import lax
from jax.experimental import pallas as pl
from jax.experimental.pallas import tpu as pltpu

F32 = jnp.float32
BF16 = jnp.bfloat16

RMS_EPS = 1e-6
RG_C = 8.0
CONV_W = 4
GLA_GATE_NORM = 16.0
GLA_CHUNK = 64

V7X_VMEM_LIMIT_BYTES = 62 * 1024 * 1024
SUBLANES = 8
LANES = 128
BF16_SUBLANES = 16
N_TOKEN_TILES = 8


def _cparams(*sem):
    return pltpu.CompilerParams(dimension_semantics=sem, vmem_limit_bytes=V7X_VMEM_LIMIT_BYTES)


def _rms_scale(x):
    return x * lax.rsqrt(jnp.mean(x * x, axis=-1, keepdims=True) + RMS_EPS)


def _silu(x):
    return x * jax.nn.sigmoid(x)


def _gelu_tanh(y):
    return jax.nn.gelu(y, approximate=True)


def _identity(y):
    return y


def _softplus(x):
    return jnp.maximum(x, 0.0) + jnp.log1p(jnp.exp(-jnp.abs(x)))


ROW_CHUNK = BF16_SUBLANES


def _for_row_chunks(n_rows, body):
    n = n_rows // ROW_CHUNK
    assert n * ROW_CHUNK == n_rows
    unroll = next(u for u in (8, 5, 4, 3, 2, 1) if n % u == 0)

    def step(c, carry):
        body(pl.multiple_of(c * ROW_CHUNK, ROW_CHUNK))
        return carry

    lax.fori_loop(0, n, step, 0, unroll=unroll)


def _chunk_rows(base, off):
    assert base % ROW_CHUNK == 0
    return pl.ds(pl.multiple_of(base + off, ROW_CHUNK), ROW_CHUNK)


def _layer_operand(stacked, layer, block, index_map):
    spec = pl.BlockSpec((None,) + tuple(block), lambda *g: (layer,) + tuple(index_map(*g)))
    return stacked, spec


def _layer_vector(stacked, layer):
    n_layers, n = stacked.shape
    return _layer_operand(stacked.reshape(n_layers, 1, n), layer, (1, n), lambda *g: (0, 0))


def _layer_full(stacked, layer):
    block = stacked.shape[1:]
    return _layer_operand(stacked, layer, block, lambda *g: (0,) * len(block))


def _token_tiles(n_prompt, n_sample):
    tp, ts = n_prompt // N_TOKEN_TILES, n_sample // N_TOKEN_TILES
    assert tp * N_TOKEN_TILES == n_prompt and ts * N_TOKEN_TILES == n_sample
    assert tp % BF16_SUBLANES == 0 and ts % BF16_SUBLANES == 0 and tp % ts == 0
    return tp, ts


def _cast_body(w_ref, o_ref):
    o_ref[...] = w_ref[...].astype(o_ref.dtype)


def _to_bf16(w, layer, *, rows=512):
    _, r, c = w.shape
    return pl.pallas_call(
        _cast_body,
        grid=(r // rows,),
        in_specs=[pl.BlockSpec((None, rows, c), lambda i: (layer, i, 0))],
        out_specs=pl.BlockSpec((rows, c), lambda i: (i, 0)),
        out_shape=jax.ShapeDtypeStruct((r, c), BF16),
        compiler_params=_cparams("arbitrary"),
        name="to_bf16",
    )(w)


def _ffn_body(*refs, tp, ts, split_in, split_out, mix_norm, cast_next):
    refs = list(refs)
    x_refs = [refs.pop(0) for _ in range(2 if split_in else 1)]
    g_ref, w1_ref, w3_ref, w2_ref = (refs.pop(0) for _ in range(4))
    g2_ref = refs.pop(0) if (split_out or mix_norm) else None
    next_f32 = [refs.pop(0) for _ in range(3 if cast_next else 0)]
    o_refs = [refs.pop(0) for _ in range(2 if split_out else 1)]
    u_hbm = refs.pop(0) if mix_norm else None
    next_bf16 = [refs.pop(0) for _ in range(3 if cast_next else 0)]
    xn_ref = refs.pop(0)
    acc_ref = refs.pop(0) if split_out else o_refs[0]
    (sem,) = refs
    tm = tp + ts
    i = pl.program_id(0)
    f = pl.program_id(1)

    def segments(blocks):
        if len(blocks) == 2:
            return [(0, tp, blocks[0]), (tp, ts, blocks[1])]
        return [(0, tm, blocks[0])]

    @pl.when(f == 0)
    def _():
        for t0, n_rows, x_ref in segments(x_refs):
            def chunk(off, x_ref=x_ref, t0=t0):
                x = x_ref[_chunk_rows(0, off), :]
                xn_ref[_chunk_rows(t0, off), :] = (_rms_scale(x) * g_ref[...]).astype(BF16)
                acc_ref[_chunk_rows(t0, off), :] = 2.0 * x

            _for_row_chunks(n_rows, chunk)

    xn = xn_ref[...]
    h1 = jnp.dot(xn, w1_ref[...].astype(BF16), preferred_element_type=F32)
    h3 = jnp.dot(xn, w3_ref[...].astype(BF16), preferred_element_type=F32)
    h = (_silu(h1) * h3).astype(BF16)
    acc_ref[...] += jnp.dot(h, w2_ref[...].astype(BF16), preferred_element_type=F32)

    for src_ref, dst_ref in zip(next_f32, next_bf16):
        dst_ref[...] = src_ref[...].astype(BF16)

    @pl.when(f == pl.num_programs(1) - 1)
    def _():
        for t0, n_rows, o_ref in segments(o_refs):
            def chunk(off, o_ref=o_ref, t0=t0):
                y = 0.5 * acc_ref[_chunk_rows(t0, off), :]
                if split_out:
                    y = _rms_scale(y) * g2_ref[...]
                o_ref[_chunk_rows(0, off), :] = y
                if mix_norm:
                    xn_ref[_chunk_rows(t0, off), :] = (_rms_scale(y) * g2_ref[...]).astype(BF16)

            _for_row_chunks(n_rows, chunk)
        if mix_norm:
            cu = pltpu.make_async_copy(xn_ref, u_hbm.at[pl.ds(i * tm, tm)], sem.at[0])
            cu.start()
            cu.wait()


def _ffn(x, layer, g, weights, *, tp, ts, g_mix=None, g_final=None, next_weights=None, tf=512):
    split_in = isinstance(x, tuple)
    split_out = g_final is not None
    mix_norm = g_mix is not None
    cast_next = next_weights is not None
    assert not (split_out and mix_norm)
    xs = list(x) if split_in else [x]
    w1, w3, w2 = weights
    d, f = w1.shape
    tm = tp + ts
    nf = f // tf
    operands = [
        _layer_vector(g, layer),
        (w1, pl.BlockSpec((d, tf), lambda i, j: (0, j))),
        (w3, pl.BlockSpec((d, tf), lambda i, j: (0, j))),
        (w2, pl.BlockSpec((tf, d), lambda i, j: (j, 0))),
    ]
    if mix_norm:
        operands.append(_layer_vector(g_mix, layer))
    if split_out:
        operands.append(_layer_vector(g_final, 0))
    row_tile = lambda rows: pl.BlockSpec((rows, d), lambda i, j: (i, 0))
    x_specs = [row_tile(tp), row_tile(ts)] if split_in else [row_tile(tm)]
    if split_out:
        out_specs = [row_tile(tp), row_tile(ts)]
        out_shape = [jax.ShapeDtypeStruct((tp * N_TOKEN_TILES, d), F32),
                     jax.ShapeDtypeStruct((ts * N_TOKEN_TILES, d), F32)]
    else:
        out_specs = [row_tile(tm)]
        out_shape = [jax.ShapeDtypeStruct((tm * N_TOKEN_TILES, d), F32)]
    if mix_norm:
        out_specs.append(pl.BlockSpec(memory_space=pl.ANY))
        out_shape.append(jax.ShapeDtypeStruct((tm * N_TOKEN_TILES, d), BF16))
    if cast_next:
        (n1, n3, n2), next_layer = next_weights
        db = d // N_TOKEN_TILES
        assert n1.shape[1:] == (d, f) and n2.shape[1:] == (f, d) and db * N_TOKEN_TILES == d
        operands += [
            _layer_operand(n1, next_layer, (db, tf), lambda i, j: (i, j)),
            _layer_operand(n3, next_layer, (db, tf), lambda i, j: (i, j)),
            _layer_operand(n2, next_layer, (tf, db), lambda i, j: (j, i)),
        ]
        out_specs += [pl.BlockSpec((db, tf), lambda i, j: (i, j)), pl.BlockSpec((db, tf), lambda i, j: (i, j)),
                      pl.BlockSpec((tf, db), lambda i, j: (j, i))]
        out_shape += [jax.ShapeDtypeStruct((d, f), BF16), jax.ShapeDtypeStruct((d, f), BF16),
                      jax.ShapeDtypeStruct((f, d), BF16)]
    out = pl.pallas_call(
        functools.partial(_ffn_body, tp=tp, ts=ts, split_in=split_in, split_out=split_out,
                          mix_norm=mix_norm, cast_next=cast_next),
        grid=(N_TOKEN_TILES, nf),
        in_specs=x_specs + [spec for _, spec in operands],
        out_specs=out_specs,
        out_shape=out_shape,
        scratch_shapes=[pltpu.VMEM((tm, d), BF16)] + ([pltpu.VMEM((tm, d), F32)] if split_out else [])
        + [pltpu.SemaphoreType.DMA((1,))],
        compiler_params=_cparams("arbitrary", "arbitrary"),
        name="ffn",
    )(*xs, *[arr for arr, _ in operands])
    out = list(out)
    acts = tuple(out[:2]) if split_out else out[0]
    rest = out[2:] if split_out else out[1:]
    u = rest.pop(0) if mix_norm else None
    return acts, u, (tuple(rest) if cast_next else None)


def _matmul_body(u_ref, *refs, tiles, layer, has_res):
    n_src = 1 + max(src for src, _, _ in tiles)
    w_hbm = refs[:n_src]
    rest = refs[n_src:]
    if has_res:
        r_ref, o_ref, stage_ref, wb_ref, sem = rest
    else:
        o_ref, stage_ref, wb_ref, sem = rest
    tn = stage_ref.shape[1]
    j = pl.program_id(0)
    i = pl.program_id(1)

    def weight_copy(jj):
        src, col0, _ = tiles[jj]
        return pltpu.make_async_copy(w_hbm[src].at[layer, :, pl.ds(col0, tn)], stage_ref, sem.at[0])

    def at_tile(jj, fn):
        pl.when(j == jj)(fn)

    @pl.when(i == 0)
    def _():
        at_tile(0, lambda: weight_copy(0).start())
        for jj in range(len(tiles)):
            at_tile(jj, lambda jj=jj: weight_copy(jj).wait())
        wb_ref[...] = stage_ref[...].astype(BF16)
        for jj in range(len(tiles) - 1):
            at_tile(jj, lambda jj=jj: weight_copy(jj + 1).start())

    for act in dict.fromkeys(act for _, _, act in tiles):
        uses = functools.reduce(jnp.logical_or, [j == jj for jj, t in enumerate(tiles) if t[2] is act])

        @pl.when(uses)
        def _(act=act):
            y = act(jnp.dot(u_ref[...], wb_ref[...], preferred_element_type=F32))
            if has_res:
                y = r_ref[...] + y
            o_ref[...] = y.astype(o_ref.dtype)


def _matmul(u, layer, ws, acts, res=None, *, tm, tn=1024, out_dtype=F32):
    m, d = u.shape
    tiles = []
    for src, (w, act) in enumerate(zip(ws, acts)):
        assert w.shape[2] % tn == 0
        tiles += [(src, col0, act) for col0 in range(0, w.shape[2], tn)]
    n = len(tiles) * tn
    io_spec = lambda: pl.BlockSpec((tm, tn), lambda j, i: (i, j))
    has_res = res is not None
    return pl.pallas_call(
        functools.partial(_matmul_body, tiles=tuple(tiles), layer=layer, has_res=has_res),
        grid=(len(tiles), m // tm),
        in_specs=[pl.BlockSpec((tm, d), lambda j, i: (i, 0))]
        + [pl.BlockSpec(memory_space=pl.ANY)] * len(ws) + ([io_spec()] if has_res else []),
        out_specs=io_spec(),
        out_shape=jax.ShapeDtypeStruct((m, n), out_dtype),
        scratch_shapes=[pltpu.VMEM((d, tn), F32), pltpu.VMEM((d, tn), BF16),
                        pltpu.SemaphoreType.DMA((1,))],
        compiler_params=_cparams("arbitrary", "arbitrary"),
        name="matmul",
    )(u, *ws, *([res] if has_res else []))


def _gla_gate_body(u_ref, w1_ref, w2_ref, b_ref, o_ref):
    t = jnp.dot(u_ref[...], w1_ref[...].astype(BF16), preferred_element_type=F32)
    logit = jnp.dot(t.astype(BF16), w2_ref[...].astype(BF16), preferred_element_type=F32) + b_ref[...]
    o_ref[...] = -_softplus(-logit) * (1.0 / GLA_GATE_NORM)


def _gla_gate(u, w_g1, w_g2, b_g, *, tm):
    m, d = u.shape
    rank, n = w_g2.shape
    return pl.pallas_call(
        _gla_gate_body,
        grid=(m // tm,),
        in_specs=[
            pl.BlockSpec((tm, d), lambda i: (i, 0)),
            pl.BlockSpec((d, rank), lambda i: (0, 0)),
            pl.BlockSpec((rank, n), lambda i: (0, 0)),
            pl.BlockSpec((1, n), lambda i: (0, 0)),
        ],
        out_specs=pl.BlockSpec((tm, n), lambda i: (i, 0)),
        out_shape=jax.ShapeDtypeStruct((m, n), F32),
        compiler_params=_cparams("arbitrary"),
        name="gla_gate",
    )(u, w_g1, w_g2, b_g.reshape(1, n))


def _prompt_rows_of_tile(o_ref, nt, per_tile, tb):
    blk = pl.program_id(0) * nt + pl.program_id(1)
    k = blk % per_tile
    tp = per_tile * tb

    @pl.when(k == 0)
    def _():
        o_ref[0, tp:, :] = jnp.zeros((o_ref.shape[1] - tp, o_ref.shape[2]), o_ref.dtype)

    return pl.ds(pl.multiple_of(k * tb, tb), tb)


def _sigmoid(x):
    return 0.5 * jnp.tanh(0.5 * x) + 0.5


def _rglru_coeffs(conv_out, wa_ref, ba_ref, wi_ref, bi_ref, lam_ref, store):
    n_blocks, bw, _ = wa_ref.shape
    for n in range(n_blocks):
        cols = slice(n * bw, (n + 1) * bw)
        xb = conv_out(cols)
        xb16 = xb.astype(BF16)
        wa = wa_ref[n].astype(BF16)
        wi = wi_ref[n].astype(BF16)
        r = _sigmoid(jnp.dot(xb16, wa, preferred_element_type=F32) + ba_ref[:, cols])
        i = _sigmoid(jnp.dot(xb16, wi, preferred_element_type=F32) + bi_ref[:, cols])
        log_a = -RG_C * r * _softplus(-lam_ref[:, cols])
        a = jnp.exp(log_a)
        b = jnp.sqrt(1.0 - a * a) * (i * xb)
        store(n, a, b)


def _rg_prompt_body(xw_ref, gate_ref, cw_ref, cb_ref, wa_ref, ba_ref, wi_ref, bi_ref, lam_ref,
                    o_ref, hl_ref, cl_ref, ext_ref, a_ref, b_ref, h_ref, *, nt, per_tile):
    t = pl.program_id(1)
    tb = xw_ref.shape[1]
    bw = wa_ref.shape[1]
    pad = SUBLANES

    @pl.when(t == 0)
    def _():
        ext_ref[0:pad, :] = jnp.zeros((pad, ext_ref.shape[1]), F32)
        h_ref[...] = jnp.zeros_like(h_ref)

    ext_ref[pad:pad + tb, :] = xw_ref[0]

    def conv_out(cols):
        ext = ext_ref[:, cols]
        y = ext * cw_ref[0:1, cols]
        for j in range(1, CONV_W):
            y = ext * cw_ref[j:j + 1, cols] + pltpu.roll(y, 1, axis=0)
        return cb_ref[:, cols] + y[pad:, :]

    def store(n, a, b):
        a_ref[:, n * bw:(n + 1) * bw] = a
        b_ref[:, n * bw:(n + 1) * bw] = b

    _rglru_coeffs(conv_out, wa_ref, ba_ref, wi_ref, bi_ref, lam_ref, store)

    def step(r, h):
        h = a_ref[pl.ds(r, 1), :] * h + b_ref[pl.ds(r, 1), :]
        b_ref[pl.ds(r, 1), :] = h
        return h

    h = lax.fori_loop(0, tb, step, h_ref[0:1, :], unroll=8)
    h_ref[0:1, :] = h
    out_rows = _prompt_rows_of_tile(o_ref, nt, per_tile, tb)
    o_ref[0, out_rows, :] = (b_ref[...] * gate_ref[0]).astype(o_ref.dtype)
    hl_ref[0] = h
    cl_ref[0] = ext_ref[pad + tb - (CONV_W - 1):pad + tb, :]
    ext_ref[0:pad, :] = ext_ref[tb:tb + pad, :]


def _rg_params(layer, conv_w, conv_b, w_a, b_a, w_i, b_i, lam):
    operands = [_layer_full(conv_w, layer), _layer_vector(conv_b, layer), _layer_full(w_a, layer),
                _layer_vector(b_a, layer), _layer_full(w_i, layer), _layer_vector(b_i, layer),
                _layer_vector(lam, layer)]
    return [arr for arr, _ in operands], [spec for _, spec in operands]


def _rg_prompt(proj, params, *, batch, seq, tp, ts, tb=256):
    p_args, p_specs = params
    m, d2 = proj.shape
    d = d2 // 2
    tm = tp + ts
    nt = seq // tb
    per_tile = tp // tb
    proj3 = proj.reshape(N_TOKEN_TILES, tm, d2)

    def tok(col):
        def index(b, t):
            blk = b * nt + t
            return (blk // per_tile, blk % per_tile, col)
        return index

    mix, hl, cl = pl.pallas_call(
        functools.partial(_rg_prompt_body, nt=nt, per_tile=per_tile),
        grid=(batch, nt),
        in_specs=[pl.BlockSpec((1, tb, d), tok(1)), pl.BlockSpec((1, tb, d), tok(0))] + p_specs,
        out_specs=[
            pl.BlockSpec((1, tm, d), lambda b, t: ((b * nt + t) // per_tile, 0, 0)),
            pl.BlockSpec((1, 1, d), lambda b, t: (b, 0, 0)),
            pl.BlockSpec((1, CONV_W - 1, d), lambda b, t: (b, 0, 0)),
        ],
        out_shape=[
            jax.ShapeDtypeStruct((N_TOKEN_TILES, tm, d), BF16),
            jax.ShapeDtypeStruct((batch, 1, d), F32),
            jax.ShapeDtypeStruct((batch, CONV_W - 1, d), F32),
        ],
        scratch_shapes=[
            pltpu.VMEM((tb + SUBLANES, d), F32),
            pltpu.VMEM((tb, d), F32),
            pltpu.VMEM((tb, d), F32),
            pltpu.VMEM((SUBLANES, d), F32),
        ],
        compiler_params=_cparams("arbitrary", "arbitrary"),
        name="rg_prompt",
    )(proj3, proj3, *p_args)
    return mix, hl, cl


def _rg_sample_body(mix_ref, xw_ref, gate_ref, conv_ref, h0_ref, cw_ref, cb_ref, wa_ref, ba_ref,
                    wi_ref, bi_ref, lam_ref, o_ref, hn_ref, cn_ref, a_ref, b_ref):
    del mix_ref
    n, d = h0_ref.shape
    bw = wa_ref.shape[1]
    x = xw_ref[...].reshape(n, d)
    xc = cb_ref[...]
    for j in range(CONV_W - 1):
        xc = xc + conv_ref[:, j * d:(j + 1) * d] * cw_ref[j:j + 1, :]
    xc = xc + x * cw_ref[CONV_W - 1:CONV_W, :]

    def store(k, a, b):
        a_ref[:, k * bw:(k + 1) * bw] = a
        b_ref[:, k * bw:(k + 1) * bw] = b

    _rglru_coeffs(lambda cols: xc[:, cols], wa_ref, ba_ref, wi_ref, bi_ref, lam_ref, store)
    h = b_ref[...] + a_ref[...] * h0_ref[...]
    hn_ref[...] = h
    o_ref[...] = (h * gate_ref[...].reshape(n, d)).reshape(o_ref.shape).astype(o_ref.dtype)
    cn_ref[:, 0:(CONV_W - 2) * d] = conv_ref[:, d:(CONV_W - 1) * d]
    cn_ref[:, (CONV_W - 2) * d:] = x


def _rg_sample(mix, proj, layer, conv_state, h0, params, *, tp, ts):
    p_args, p_specs = params
    n_layers, n, d = h0.shape
    tm = tp + ts
    proj3 = proj.reshape(N_TOKEN_TILES, tm, 2 * d)
    s_blk = tp // ts
    full2 = lambda shape: pl.BlockSpec(shape, lambda i: (0, 0))
    kc = (CONV_W - 1) * d
    conv_arg, conv_spec = _layer_full(conv_state.reshape(n_layers, n, kc), layer)
    h0_arg, h0_spec = _layer_full(h0, layer)
    return pl.pallas_call(
        _rg_sample_body,
        grid=(1,),
        in_specs=[
            pl.BlockSpec(memory_space=pl.ANY),
            pl.BlockSpec((N_TOKEN_TILES, ts, d), lambda i: (0, s_blk, 1)),
            pl.BlockSpec((N_TOKEN_TILES, ts, d), lambda i: (0, s_blk, 0)),
            conv_spec, h0_spec,
        ] + p_specs,
        out_specs=[
            pl.BlockSpec((N_TOKEN_TILES, ts, d), lambda i: (0, s_blk, 0)),
            full2((n, d)),
            full2((n, kc)),
        ],
        out_shape=[
            jax.ShapeDtypeStruct(mix.shape, mix.dtype),
            jax.ShapeDtypeStruct((n, d), F32),
            jax.ShapeDtypeStruct((n, kc), F32),
        ],
        scratch_shapes=[pltpu.VMEM((n, d), F32), pltpu.VMEM((n, d), F32)],
        input_output_aliases={0: 0},
        compiler_params=_cparams("arbitrary"),
        name="rg_sample",
    )(mix, proj3, proj3, conv_arg, h0_arg, *p_args)


def _gla_prompt_body(q_ref, k_ref, v_ref, r_ref, g_ref, on_ref, o_ref, s_ref, *, heads, nt, per_tile):
    t = pl.program_id(1)
    tb = q_ref.shape[1]
    c = GLA_CHUNK
    dk = q_ref.shape[2] // heads
    dv = v_ref.shape[2] // heads

    @pl.when(t == 0)
    def _():
        s_ref[...] = jnp.zeros_like(s_ref)

    row = lax.broadcasted_iota(jnp.int32, (tb, tb), 0)
    col = lax.broadcasted_iota(jnp.int32, (tb, tb), 1)
    same_chunk_causal = ((row >= col) & ((row ^ col) < c)).astype(BF16)
    g_rest = g_ref[0]
    bcum = jnp.zeros(g_rest.shape, F32)
    for _ in range(3):
        g_part = g_rest.astype(BF16)
        bcum = bcum + jnp.dot(same_chunk_causal, g_part, preferred_element_type=F32)
        g_rest = g_rest - g_part.astype(F32)
    q_i = (q_ref[0].astype(F32) * dk ** -0.5 * jnp.exp(bcum)).astype(BF16)
    k_i = (k_ref[0].astype(F32) * jnp.exp(-bcum)).astype(BF16)
    causal = (lax.broadcasted_iota(jnp.int32, (c, c), 0) >= lax.broadcasted_iota(jnp.int32, (c, c), 1))

    out_base = _prompt_rows_of_tile(o_ref, nt, per_tile, tb).start
    for ci in range(tb // c):
        rows = slice(ci * c, (ci + 1) * c)
        out_rows = pl.ds(pl.multiple_of(out_base + ci * c, c), c)
        b_last = bcum[(ci + 1) * c - 1:(ci + 1) * c, :]
        k_e = (k_ref[0, rows, :].astype(F32) * jnp.exp(b_last - bcum[rows, :])).astype(BF16)
        decay_t = jnp.exp(jnp.broadcast_to(b_last, (SUBLANES, heads * dk))).T
        for h in range(heads):
            kcols = slice(h * dk, (h + 1) * dk)
            vcols = slice(h * dv, (h + 1) * dv)
            qs = q_i[rows, kcols]
            v = v_ref[0, rows, vcols]
            scores = lax.dot_general(qs, k_i[rows, kcols], (((1,), (1,)), ((), ())),
                                     preferred_element_type=F32)
            scores = jnp.where(causal, scores, 0.0).astype(BF16)
            s = s_ref[0, h]
            o = (jnp.dot(qs, s.astype(BF16), preferred_element_type=F32)
                 + jnp.dot(scores, v, preferred_element_type=F32))
            s_ref[0, h] = decay_t[kcols, 0:1] * s + lax.dot_general(
                k_e[:, kcols], v, (((0,), (0,)), ((), ())), preferred_element_type=F32)
            o = _rms_scale(o) * on_ref[...]
            o_ref[0, out_rows, vcols] = (o * r_ref[0, rows, vcols].astype(F32)).astype(o_ref.dtype)


def _gla_prompt(qkvr, g, layer, onorm, *, batch, seq, heads, dk, dv, tp, ts, tb=256):
    tm = tp + ts
    nt = seq // tb
    per_tile = tp // tb
    hk, hv = heads * dk, heads * dv
    assert hv == 2 * hk
    qkvr3 = qkvr.reshape(N_TOKEN_TILES, tm, qkvr.shape[1])
    g3 = g.reshape(N_TOKEN_TILES, tm, hk)

    def tok(col):
        def index(b, t):
            blk = b * nt + t
            return (blk // per_tile, blk % per_tile, col)
        return index

    return pl.pallas_call(
        functools.partial(_gla_prompt_body, heads=heads, nt=nt, per_tile=per_tile),
        grid=(batch, nt),
        in_specs=[
            pl.BlockSpec((1, tb, hk), tok(0)),
            pl.BlockSpec((1, tb, hk), tok(1)),
            pl.BlockSpec((1, tb, hv), tok(1)),
            pl.BlockSpec((1, tb, hv), tok(2)),
            pl.BlockSpec((1, tb, hk), tok(0)),
            _layer_vector(onorm, layer)[1],
        ],
        out_specs=[
            pl.BlockSpec((1, tm, hv), lambda b, t: ((b * nt + t) // per_tile, 0, 0)),
            pl.BlockSpec((1, heads, dk, dv), lambda b, t: (b, 0, 0, 0)),
        ],
        out_shape=[
            jax.ShapeDtypeStruct((N_TOKEN_TILES, tm, hv), BF16),
            jax.ShapeDtypeStruct((batch, heads, dk, dv), F32),
        ],
        compiler_params=_cparams("arbitrary", "arbitrary"),
        name="gla_prompt",
    )(qkvr3, qkvr3, qkvr3, qkvr3, g3, _layer_vector(onorm, layer)[0])


def _gla_sample_body(mix_ref, q_ref, k_ref, v_ref, r_ref, g_ref, s0_ref, on_ref, o_ref, s_ref):
    del mix_ref
    bb, dk = q_ref.shape[1:]
    alpha_t = jnp.exp(g_ref[0]).T
    k_t = k_ref[0].astype(F32).T
    q_t = (q_ref[0].astype(F32) * dk ** -0.5).T
    v = v_ref[0].astype(F32)
    rows = []
    for b in range(bb):
        s_new = alpha_t[:, b:b + 1] * s0_ref[b, 0] + k_t[:, b:b + 1] * v[b:b + 1, :]
        s_ref[b, 0] = s_new
        rows.append(jnp.sum(q_t[:, b:b + 1] * s_new, axis=0, keepdims=True))
    o = jnp.concatenate(rows, axis=0)
    o = _rms_scale(o) * on_ref[...]
    o_ref[0] = (o * r_ref[0].astype(F32)).astype(o_ref.dtype)


def _gla_sample(mix, qkvr, g, layer, s0, onorm, *, heads, dk, dv, tp, ts):
    tm = tp + ts
    s_blk = tp // ts
    kq = heads
    vq = (2 * heads * dk) // dv
    rq = vq + heads
    qkvr3 = qkvr.reshape(N_TOKEN_TILES, tm, qkvr.shape[1])
    g3 = g.reshape(N_TOKEN_TILES, tm, heads * dk)
    return pl.pallas_call(
        _gla_sample_body,
        grid=(N_TOKEN_TILES, heads),
        in_specs=[
            pl.BlockSpec(memory_space=pl.ANY),
            pl.BlockSpec((1, ts, dk), lambda s, h: (s, s_blk, h)),
            pl.BlockSpec((1, ts, dk), lambda s, h: (s, s_blk, kq + h)),
            pl.BlockSpec((1, ts, dv), lambda s, h: (s, s_blk, vq + h)),
            pl.BlockSpec((1, ts, dv), lambda s, h: (s, s_blk, rq + h)),
            pl.BlockSpec((1, ts, dk), lambda s, h: (s, s_blk, h)),
            _layer_operand(s0, layer, (ts, 1, dk, dv), lambda s, h: (s, h, 0, 0))[1],
            _layer_vector(onorm, layer)[1],
        ],
        out_specs=[
            pl.BlockSpec((1, ts, dv), lambda s, h: (s, s_blk, h)),
            pl.BlockSpec((ts, 1, dk, dv), lambda s, h: (s, h, 0, 0)),
        ],
        out_shape=[
            jax.ShapeDtypeStruct(mix.shape, mix.dtype),
            jax.ShapeDtypeStruct(s0.shape[1:], F32),
        ],
        input_output_aliases={0: 0},
        compiler_params=_cparams("arbitrary", "arbitrary"),
        name="gla_sample",
    )(mix, qkvr3, qkvr3, qkvr3, qkvr3, g3, s0, _layer_vector(onorm, layer)[0])


def kernel(x_prompt, x_sample, state_rglru_h, state_rglru_conv, state_gla_S, ln_ffn1, ffn1_w1, ffn1_w3, ffn1_w2, ln_mix, ln_ffn2, ffn2_w1, ffn2_w3, ffn2_w2, rg_w_y, rg_w_x, rg_conv_w, rg_conv_b, rg_w_a, rg_b_a, rg_w_i, rg_b_i, rg_lambda, rg_w_o, gla_w_q, gla_w_k, gla_w_v, gla_w_g1, gla_w_g2, gla_b_g, gla_w_r, gla_onorm, gla_w_o, ln_final):
    batch, seq, d = x_prompt.shape
    n_dec = x_sample.shape[0]
    depth = ln_ffn1.shape[0]
    heads, dk, dv = state_gla_S.shape[2:]
    tp, ts = _token_tiles(batch * seq, n_dec)
    tm = tp + ts
    tiles = dict(tp=tp, ts=ts)

    x = (x_prompt.reshape(batch * seq, d), x_sample.reshape(n_dec, d))

    ffn_f32 = []
    for i in range(depth):
        ffn_f32 += [((ffn1_w1, ffn1_w3, ffn1_w2), i), ((ffn2_w1, ffn2_w3, ffn2_w2), i)]
    ffn_f32.append(None)
    ffn_w = tuple(_to_bf16(w, 0) for w in ffn_f32[0][0])

    p_h, p_conv, p_s, s_h, s_conv, s_s = [], [], [], [], [], []
    for i in range(depth):
        j = i // 2
        x, u, ffn_w = _ffn(x, i, ln_ffn1, ffn_w, g_mix=ln_mix, next_weights=ffn_f32[2 * i + 1], **tiles)
        if i % 2 == 0:
            proj = _matmul(u, j, [rg_w_y, rg_w_x], [_gelu_tanh, _identity], tm=tm)
            params = _rg_params(j, rg_conv_w, rg_conv_b, rg_w_a, rg_b_a, rg_w_i, rg_b_i, rg_lambda)
            mix, hl, cl = _rg_prompt(proj, params, batch=batch, seq=seq, **tiles)
            mix, hn, cn = _rg_sample(mix, proj, j, state_rglru_conv, state_rglru_h, params, **tiles)
            p_h.append(hl.reshape(batch, d))
            p_conv.append(cl)
            s_h.append(hn)
            s_conv.append(cn.reshape(n_dec, CONV_W - 1, d))
            w_o = rg_w_o
        else:
            qkvr = _matmul(u, j, [gla_w_q, gla_w_k, gla_w_v, gla_w_r],
                           [_identity, _identity, _identity, _silu], tm=tm, out_dtype=BF16)
            rank = gla_w_g1.shape[2]
            w_g1 = jnp.pad(gla_w_g1[j], ((0, 0), (0, LANES - rank)))
            w_g2 = jnp.pad(gla_w_g2[j], ((0, LANES - rank), (0, 0)))
            g = _gla_gate(u, w_g1, w_g2, gla_b_g[j], tm=tm)
            mix, s_fin = _gla_prompt(qkvr, g, j, gla_onorm, batch=batch, seq=seq, heads=heads,
                                     dk=dk, dv=dv, **tiles)
            mix, s_new = _gla_sample(mix, qkvr, g, j, state_gla_S, gla_onorm, heads=heads,
                                     dk=dk, dv=dv, **tiles)
            p_s.append(s_fin)
            s_s.append(s_new)
            w_o = gla_w_o
        x = _matmul(mix.reshape(N_TOKEN_TILES * tm, d), j, [w_o], [_identity], res=x, tm=tm)
        g_final = ln_final.reshape(1, d) if i == depth - 1 else None
        tf = 512 if g_final is None else 256
        x, _, ffn_w = _ffn(x, i, ln_ffn2, ffn_w, g_final=g_final, next_weights=ffn_f32[2 * i + 2], tf=tf,
                           **tiles)

    y_prompt = x[0].reshape(batch, seq, d)
    y_sample = x[1].reshape(n_dec, 1, d)
    return (y_prompt, y_sample, jnp.stack(p_h), jnp.stack(p_conv), jnp.stack(p_s),
            jnp.stack(s_h), jnp.stack(s_conv), jnp.stack(s_s))
```

```python
import functools

import jax
import jax.numpy as jnp
from jax import lax
from jax.experimental import pallas as pl
from jax.experimental.pallas import tpu as pltpu

F32 = jnp.float32
BF16 = jnp.bfloat16

RMS_EPS = 1e-6
RG_C = 8.0
CONV_W = 4
GLA_GATE_NORM = 16.0
GLA_CHUNK = 64

V7X_VMEM_LIMIT_BYTES = 62 * 1024 * 1024
SUBLANES = 8
LANES = 128
BF16_SUBLANES = 16
N_TOKEN_TILES = 8


def _cparams(*sem):
    return pltpu.CompilerParams(dimension_semantics=sem, vmem_limit_bytes=V7X_VMEM_LIMIT_BYTES)


def _rms_scale(x):
    return x * lax.rsqrt(jnp.mean(x * x, axis=-1, keepdims=True) + RMS_EPS)


def _silu(x):
    return x * jax.nn.sigmoid(x)


def _gelu_tanh(y):
    return jax.nn.gelu(y, approximate=True)


def _identity(y):
    return y


def _softplus(x):
    return jnp.maximum(x, 0.0) + jnp.log1p(jnp.exp(-jnp.abs(x)))


ROW_CHUNK = BF16_SUBLANES


def _for_row_chunks(n_rows, body):
    n = n_rows // ROW_CHUNK
    assert n * ROW_CHUNK == n_rows
    unroll = next(u for u in (8, 5, 4, 3, 2, 1) if n % u == 0)

    def step(c, carry):
        body(pl.multiple_of(c * ROW_CHUNK, ROW_CHUNK))
        return carry

    lax.fori_loop(0, n, step, 0, unroll=unroll)


def _chunk_rows(base, off):
    assert base % ROW_CHUNK == 0
    return pl.ds(pl.multiple_of(base + off, ROW_CHUNK), ROW_CHUNK)


def _layer_operand(stacked, layer, block, index_map):
    spec = pl.BlockSpec((None,) + tuple(block), lambda *g: (layer,) + tuple(index_map(*g)))
    return stacked, spec


def _layer_vector(stacked, layer):
    n_layers, n = stacked.shape
    return _layer_operand(stacked.reshape(n_layers, 1, n), layer, (1, n), lambda *g: (0, 0))


def _layer_full(stacked, layer):
    block = stacked.shape[1:]
    return _layer_operand(stacked, layer, block, lambda *g: (0,) * len(block))


def _token_tiles(n_prompt, n_sample):
    tp, ts = n_prompt // N_TOKEN_TILES, n_sample // N_TOKEN_TILES
    assert tp * N_TOKEN_TILES == n_prompt and ts * N_TOKEN_TILES == n_sample
    assert tp % BF16_SUBLANES == 0 and ts % BF16_SUBLANES == 0 and tp % ts == 0
    return tp, ts


def _cast_body(w_ref, o_ref):
    o_ref[...] = w_ref[...].astype(o_ref.dtype)


def _to_bf16(w, layer, *, rows=512):
    _, r, c = w.shape
    return pl.pallas_call(
        _cast_body,
        grid=(r // rows,),
        in_specs=[pl.BlockSpec((None, rows, c), lambda i: (layer, i, 0))],
        out_specs=pl.BlockSpec((rows, c), lambda i: (i, 0)),
        out_shape=jax.ShapeDtypeStruct((r, c), BF16),
        compiler_params=_cparams("arbitrary"),
        name="to_bf16",
    )(w)


def _ffn_body(*refs, tp, ts, split_in, split_out, mix_norm, cast_next):
    refs = list(refs)
    x_refs = [refs.pop(0) for _ in range(2 if split_in else 1)]
    g_ref, w1_ref, w3_ref, w2_ref = (refs.pop(0) for _ in range(4))
    g2_ref = refs.pop(0) if (split_out or mix_norm) else None
    next_f32 = [refs.pop(0) for _ in range(3 if cast_next else 0)]
    o_refs = [refs.pop(0) for _ in range(2 if split_out else 1)]
    u_hbm = refs.pop(0) if mix_norm else None
    next_bf16 = [refs.pop(0) for _ in range(3 if cast_next else 0)]
    xn_ref = refs.pop(0)
    acc_ref = refs.pop(0) if split_out else o_refs[0]
    (sem,) = refs
    tm = tp + ts
    i = pl.program_id(0)
    f = pl.program_id(1)

    def segments(blocks):
        if len(blocks) == 2:
            return [(0, tp, blocks[0]), (tp, ts, blocks[1])]
        return [(0, tm, blocks[0])]

    @pl.when(f == 0)
    def _():
        for t0, n_rows, x_ref in segments(x_refs):
            def chunk(off, x_ref=x_ref, t0=t0):
                x = x_ref[_chunk_rows(0, off), :]
                xn_ref[_chunk_rows(t0, off), :] = (_rms_scale(x) * g_ref[...]).astype(BF16)
                acc_ref[_chunk_rows(t0, off), :] = 2.0 * x

            _for_row_chunks(n_rows, chunk)

    xn = xn_ref[...]
    h1 = jnp.dot(xn, w1_ref[...].astype(BF16), preferred_element_type=F32)
    h3 = jnp.dot(xn, w3_ref[...].astype(BF16), preferred_element_type=F32)
    h = (_silu(h1) * h3).astype(BF16)
    acc_ref[...] += jnp.dot(h, w2_ref[...].astype(BF16), preferred_element_type=F32)

    for src_ref, dst_ref in zip(next_f32, next_bf16):
        dst_ref[...] = src_ref[...].astype(BF16)

    @pl.when(f == pl.num_programs(1) - 1)
    def _():
        for t0, n_rows, o_ref in segments(o_refs):
            def chunk(off, o_ref=o_ref, t0=t0):
                y = 0.5 * acc_ref[_chunk_rows(t0, off), :]
                if split_out:
                    y = _rms_scale(y) * g2_ref[...]
                o_ref[_chunk_rows(0, off), :] = y
                if mix_norm:
                    xn_ref[_chunk_rows(t0, off), :] = (_rms_scale(y) * g2_ref[...]).astype(BF16)

            _for_row_chunks(n_rows, chunk)
        if mix_norm:
            cu = pltpu.make_async_copy(xn_ref, u_hbm.at[pl.ds(i * tm, tm)], sem.at[0])
            cu.start()
            cu.wait()


def _ffn(x, layer, g, weights, *, tp, ts, g_mix=None, g_final=None, next_weights=None, tf=512):
    split_in = isinstance(x, tuple)
    split_out = g_final is not None
    mix_norm = g_mix is not None
    cast_next = next_weights is not None
    assert not (split_out and mix_norm)
    xs = list(x) if split_in else [x]
    w1, w3, w2 = weights
    d, f = w1.shape
    tm = tp + ts
    nf = f // tf
    operands = [
        _layer_vector(g, layer),
        (w1, pl.BlockSpec((d, tf), lambda i, j: (0, j))),
        (w3, pl.BlockSpec((d, tf), lambda i, j: (0, j))),
        (w2, pl.BlockSpec((tf, d), lambda i, j: (j, 0))),
    ]
    if mix_norm:
        operands.append(_layer_vector(g_mix, layer))
    if split_out:
        operands.append(_layer_vector(g_final, 0))
    row_tile = lambda rows: pl.BlockSpec((rows, d), lambda i, j: (i, 0))
    x_specs = [row_tile(tp), row_tile(ts)] if split_in else [row_tile(tm)]
    if split_out:
        out_specs = [row_tile(tp), row_tile(ts)]
        out_shape = [jax.ShapeDtypeStruct((tp * N_TOKEN_TILES, d), F32),
                     jax.ShapeDtypeStruct((ts * N_TOKEN_TILES, d), F32)]
    else:
        out_specs = [row_tile(tm)]
        out_shape = [jax.ShapeDtypeStruct((tm * N_TOKEN_TILES, d), F32)]
    if mix_norm:
        out_specs.append(pl.BlockSpec(memory_space=pl.ANY))
        out_shape.append(jax.ShapeDtypeStruct((tm * N_TOKEN_TILES, d), BF16))
    if cast_next:
        (n1, n3, n2), next_layer = next_weights
        db = d // N_TOKEN_TILES
        assert n1.shape[1:] == (d, f) and n2.shape[1:] == (f, d) and db * N_TOKEN_TILES == d
        operands += [
            _layer_operand(n1, next_layer, (db, tf), lambda i, j: (i, j)),
            _layer_operand(n3, next_layer, (db, tf), lambda i, j: (i, j)),
            _layer_operand(n2, next_layer, (tf, db), lambda i, j: (j, i)),
        ]
        out_specs += [pl.BlockSpec((db, tf), lambda i, j: (i, j)), pl.BlockSpec((db, tf), lambda i, j: (i, j)),
                      pl.BlockSpec((tf, db), lambda i, j: (j, i))]
        out_shape += [jax.ShapeDtypeStruct((d, f), BF16), jax.ShapeDtypeStruct((d, f), BF16),
                      jax.ShapeDtypeStruct((f, d), BF16)]
    out = pl.pallas_call(
        functools.partial(_ffn_body, tp=tp, ts=ts, split_in=split_in, split_out=split_out,
                          mix_norm=mix_norm, cast_next=cast_next),
        grid=(N_TOKEN_TILES, nf),
        in_specs=x_specs + [spec for _, spec in operands],
        out_specs=out_specs,
        out_shape=out_shape,
        scratch_shapes=[pltpu.VMEM((tm, d), BF16)] + ([pltpu.VMEM((tm, d), F32)] if split_out else [])
        + [pltpu.SemaphoreType.DMA((1,))],
        compiler_params=_cparams("arbitrary", "arbitrary"),
        name="ffn",
    )(*xs, *[arr for arr, _ in operands])
    out = list(out)
    acts = tuple(out[:2]) if split_out else out[0]
    rest = out[2:] if split_out else out[1:]
    u = rest.pop(0) if mix_norm else None
    return acts, u, (tuple(rest) if cast_next else None)


def _matmul_body(u_ref, *refs, tiles, layer, has_res):
    n_src = 1 + max(src for src, _, _ in tiles)
    w_hbm = refs[:n_src]
    rest = refs[n_src:]
    if has_res:
        r_ref, o_ref, stage_ref, wb_ref, sem = rest
    else:
        o_ref, stage_ref, wb_ref, sem = rest
    tn = stage_ref.shape[1]
    j = pl.program_id(0)
    i = pl.program_id(1)

    def weight_copy(jj):
        src, col0, _ = tiles[jj]
        return pltpu.make_async_copy(w_hbm[src].at[layer, :, pl.ds(col0, tn)], stage_ref, sem.at[0])

    def at_tile(jj, fn):
        pl.when(j == jj)(fn)

    @pl.when(i == 0)
    def _():
        at_tile(0, lambda: weight_copy(0).start())
        for jj in range(len(tiles)):
            at_tile(jj, lambda jj=jj: weight_copy(jj).wait())
        wb_ref[...] = stage_ref[...].astype(BF16)
        for jj in range(len(tiles) - 1):
            at_tile(jj, lambda jj=jj: weight_copy(jj + 1).start())

    for act in dict.fromkeys(act for _, _, act in tiles):
        uses = functools.reduce(jnp.logical_or, [j == jj for jj, t in enumerate(tiles) if t[2] is act])

        @pl.when(uses)
        def _(act=act):
            y = act(jnp.dot(u_ref[...], wb_ref[...], preferred_element_type=F32))
            if has_res:
                y = r_ref[...] + y
            o_ref[...] = y.astype(o_ref.dtype)


def _matmul(u, layer, ws, acts, res=None, *, tm, tn=1024, out_dtype=F32):
    m, d = u.shape
    tiles = []
    for src, (w, act) in enumerate(zip(ws, acts)):
        assert w.shape[2] % tn == 0
        tiles += [(src, col0, act) for col0 in range(0, w.shape[2], tn)]
    n = len(tiles) * tn
    io_spec = lambda: pl.BlockSpec((tm, tn), lambda j, i: (i, j))
    has_res = res is not None
    return pl.pallas_call(
        functools.partial(_matmul_body, tiles=tuple(tiles), layer=layer, has_res=has_res),
        grid=(len(tiles), m // tm),
        in_specs=[pl.BlockSpec((tm, d), lambda j, i: (i, 0))]
        + [pl.BlockSpec(memory_space=pl.ANY)] * len(ws) + ([io_spec()] if has_res else []),
        out_specs=io_spec(),
        out_shape=jax.ShapeDtypeStruct((m, n), out_dtype),
        scratch_shapes=[pltpu.VMEM((d, tn), F32), pltpu.VMEM((d, tn), BF16),
                        pltpu.SemaphoreType.DMA((1,))],
        compiler_params=_cparams("arbitrary", "arbitrary"),
        name="matmul",
    )(u, *ws, *([res] if has_res else []))


def _gla_gate_body(u_ref, w1_ref, w2_ref, b_ref, o_ref):
    t = jnp.dot(u_ref[...], w1_ref[...].astype(BF16), preferred_element_type=F32)
    logit = jnp.dot(t.astype(BF16), w2_ref[...].astype(BF16), preferred_element_type=F32) + b_ref[...]
    o_ref[...] = -_softplus(-logit) * (1.0 / GLA_GATE_NORM)


def _gla_gate(u, w_g1, w_g2, b_g, *, tm):
    m, d = u.shape
    rank, n = w_g2.shape
    return pl.pallas_call(
        _gla_gate_body,
        grid=(m // tm,),
        in_specs=[
            pl.BlockSpec((tm, d), lambda i: (i, 0)),
            pl.BlockSpec((d, rank), lambda i: (0, 0)),
            pl.BlockSpec((rank, n), lambda i: (0, 0)),
            pl.BlockSpec((1, n), lambda i: (0, 0)),
        ],
        out_specs=pl.BlockSpec((tm, n), lambda i: (i, 0)),
        out_shape=jax.ShapeDtypeStruct((m, n), F32),
        compiler_params=_cparams("arbitrary"),
        name="gla_gate",
    )(u, w_g1, w_g2, b_g.reshape(1, n))


def _prompt_rows_of_tile(o_ref, nt, per_tile, tb):
    blk = pl.program_id(0) * nt + pl.program_id(1)
    k = blk % per_tile
    tp = per_tile * tb

    @pl.when(k == 0)
    def _():
        o_ref[0, tp:, :] = jnp.zeros((o_ref.shape[1] - tp, o_ref.shape[2]), o_ref.dtype)

    return pl.ds(pl.multiple_of(k * tb, tb), tb)


def _sigmoid(x):
    return 0.5 * jnp.tanh(0.5 * x) + 0.5


def _rglru_coeffs(conv_out, wa_ref, ba_ref, wi_ref, bi_ref, lam_ref, store):
    n_blocks, bw, _ = wa_ref.shape
    for n in range(n_blocks):
        cols = slice(n * bw, (n + 1) * bw)
        xb = conv_out(cols)
        xb16 = xb.astype(BF16)
        wa = wa_ref[n].astype(BF16)
        wi = wi_ref[n].astype(BF16)
        r = _sigmoid(jnp.dot(xb16, wa, preferred_element_type=F32) + ba_ref[:, cols])
        i = _sigmoid(jnp.dot(xb16, wi, preferred_element_type=F32) + bi_ref[:, cols])
        log_a = -RG_C * r * _softplus(-lam_ref[:, cols])
        a = jnp.exp(log_a)
        b = jnp.sqrt(1.0 - a * a) * (i * xb)
        store(n, a, b)


def _rg_prompt_body(xw_ref, gate_ref, cw_ref, cb_ref, wa_ref, ba_ref, wi_ref, bi_ref, lam_ref,
                    o_ref, hl_ref, cl_ref, ext_ref, a_ref, b_ref, h_ref, *, nt, per_tile):
    t = pl.program_id(1)
    tb = xw_ref.shape[1]
    bw = wa_ref.shape[1]
    pad = SUBLANES

    @pl.when(t == 0)
    def _():
        ext_ref[0:pad, :] = jnp.zeros((pad, ext_ref.shape[1]), F32)
        h_ref[...] = jnp.zeros_like(h_ref)

    ext_ref[pad:pad + tb, :] = xw_ref[0]

    def conv_out(cols):
        ext = ext_ref[:, cols]
        y = ext * cw_ref[0:1, cols]
        for j in range(1, CONV_W):
            y = ext * cw_ref[j:j + 1, cols] + pltpu.roll(y, 1, axis=0)
        return cb_ref[:, cols] + y[pad:, :]

    def store(n, a, b):
        a_ref[:, n * bw:(n + 1) * bw] = a
        b_ref[:, n * bw:(n + 1) * bw] = b

    _rglru_coeffs(conv_out, wa_ref, ba_ref, wi_ref, bi_ref, lam_ref, store)

    def step(r, h):
        h = a_ref[pl.ds(r, 1), :] * h + b_ref[pl.ds(r, 1), :]
        b_ref[pl.ds(r, 1), :] = h
        return h

    h = lax.fori_loop(0, tb, step, h_ref[0:1, :], unroll=8)
    h_ref[0:1, :] = h
    out_rows = _prompt_rows_of_tile(o_ref, nt, per_tile, tb)
    o_ref[0, out_rows, :] = (b_ref[...] * gate_ref[0]).astype(o_ref.dtype)
    hl_ref[0] = h
    cl_ref[0] = ext_ref[pad + tb - (CONV_W - 1):pad + tb, :]
    ext_ref[0:pad, :] = ext_ref[tb:tb + pad, :]


def _rg_params(layer, conv_w, conv_b, w_a, b_a, w_i, b_i, lam):
    operands = [_layer_full(conv_w, layer), _layer_vector(conv_b, layer), _layer_full(w_a, layer),
                _layer_vector(b_a, layer), _layer_full(w_i, layer), _layer_vector(b_i, layer),
                _layer_vector(lam, layer)]
    return [arr for arr, _ in operands], [spec for _, spec in operands]


def _rg_prompt(proj, params, *, batch, seq, tp, ts, tb=256):
    p_args, p_specs = params
    m, d2 = proj.shape
    d = d2 // 2
    tm = tp + ts
    nt = seq // tb
    per_tile = tp // tb
    proj3 = proj.reshape(N_TOKEN_TILES, tm, d2)

    def tok(col):
        def index(b, t):
            blk = b * nt + t
            return (blk // per_tile, blk % per_tile, col)
        return index

    mix, hl, cl = pl.pallas_call(
        functools.partial(_rg_prompt_body, nt=nt, per_tile=per_tile),
        grid=(batch, nt),
        in_specs=[pl.BlockSpec((1, tb, d), tok(1)), pl.BlockSpec((1, tb, d), tok(0))] + p_specs,
        out_specs=[
            pl.BlockSpec((1, tm, d), lambda b, t: ((b * nt + t) // per_tile, 0, 0)),
            pl.BlockSpec((1, 1, d), lambda b, t: (b, 0, 0)),
            pl.BlockSpec((1, CONV_W - 1, d), lambda b, t: (b, 0, 0)),
        ],
        out_shape=[
            jax.ShapeDtypeStruct((N_TOKEN_TILES, tm, d), BF16),
            jax.ShapeDtypeStruct((batch, 1, d), F32),
            jax.ShapeDtypeStruct((batch, CONV_W - 1, d), F32),
        ],
        scratch_shapes=[
            pltpu.VMEM((tb + SUBLANES, d), F32),
            pltpu.VMEM((tb, d), F32),
            pltpu.VMEM((tb, d), F32),
            pltpu.VMEM((SUBLANES, d), F32),
        ],
        compiler_params=_cparams("arbitrary", "arbitrary"),
        name="rg_prompt",
    )(proj3, proj3, *p_args)
    return mix, hl, cl


def _rg_sample_body(mix_ref, xw_ref, gate_ref, conv_ref, h0_ref, cw_ref, cb_ref, wa_ref, ba_ref,
                    wi_ref, bi_ref, lam_ref, o_ref, hn_ref, cn_ref, a_ref, b_ref):
    del mix_ref
    n, d = h0_ref.shape
    bw = wa_ref.shape[1]
    x = xw_ref[...].reshape(n, d)
    xc = cb_ref[...]
    for j in range(CONV_W - 1):
        xc = xc + conv_ref[:, j * d:(j + 1) * d] * cw_ref[j:j + 1, :]
    xc = xc + x * cw_ref[CONV_W - 1:CONV_W, :]

    def store(k, a, b):
        a_ref[:, k * bw:(k + 1) * bw] = a
        b_ref[:, k * bw:(k + 1) * bw] = b

    _rglru_coeffs(lambda cols: xc[:, cols], wa_ref, ba_ref, wi_ref, bi_ref, lam_ref, store)
    h = b_ref[...] + a_ref[...] * h0_ref[...]
    hn_ref[...] = h
    o_ref[...] = (h * gate_ref[...].reshape(n, d)).reshape(o_ref.shape).astype(o_ref.dtype)
    cn_ref[:, 0:(CONV_W - 2) * d] = conv_ref[:, d:(CONV_W - 1) * d]
    cn_ref[:, (CONV_W - 2) * d:] = x


def _rg_sample(mix, proj, layer, conv_state, h0, params, *, tp, ts):
    p_args, p_specs = params
    n_layers, n, d = h0.shape
    tm = tp + ts
    proj3 = proj.reshape(N_TOKEN_TILES, tm, 2 * d)
    s_blk = tp // ts
    full2 = lambda shape: pl.BlockSpec(shape, lambda i: (0, 0))
    kc = (CONV_W - 1) * d
    conv_arg, conv_spec = _layer_full(conv_state.reshape(n_layers, n, kc), layer)
    h0_arg, h0_spec = _layer_full(h0, layer)
    return pl.pallas_call(
        _rg_sample_body,
        grid=(1,),
        in_specs=[
            pl.BlockSpec(memory_space=pl.ANY),
            pl.BlockSpec((N_TOKEN_TILES, ts, d), lambda i: (0, s_blk, 1)),
            pl.BlockSpec((N_TOKEN_TILES, ts, d), lambda i: (0, s_blk, 0)),
            conv_spec, h0_spec,
        ] + p_specs,
        out_specs=[
            pl.BlockSpec((N_TOKEN_TILES, ts, d), lambda i: (0, s_blk, 0)),
            full2((n, d)),
            full2((n, kc)),
        ],
        out_shape=[
            jax.ShapeDtypeStruct(mix.shape, mix.dtype),
            jax.ShapeDtypeStruct((n, d), F32),
            jax.ShapeDtypeStruct((n, kc), F32),
        ],
        scratch_shapes=[pltpu.VMEM((n, d), F32), pltpu.VMEM((n, d), F32)],
        input_output_aliases={0: 0},
        compiler_params=_cparams("arbitrary"),
        name="rg_sample",
    )(mix, proj3, proj3, conv_arg, h0_arg, *p_args)


def _gla_sample_update(q_ref, k_ref, v_ref, r_ref, g_ref, s0_ref, on_ref, s_ref):
    bb, dk = q_ref.shape[1:]
    alpha_t = jnp.exp(g_ref[0]).T
    k_t = k_ref[0].astype(F32).T
    q_t = (q_ref[0].astype(F32) * dk ** -0.5).T
    v = v_ref[0].astype(F32)
    rows = []
    for b in range(bb):
        s_new = alpha_t[:, b:b + 1] * s0_ref[b, 0] + k_t[:, b:b + 1] * v[b:b + 1, :]
        s_ref[b, 0] = s_new
        rows.append(jnp.sum(q_t[:, b:b + 1] * s_new, axis=0, keepdims=True))
    o = jnp.concatenate(rows, axis=0)
    o = _rms_scale(o) * on_ref[...]
    return o * r_ref[0].astype(F32)


def _gla_body(q_ref, k_ref, v_ref, r_ref, g_ref, on_ref, sq_ref, sk_ref, sv_ref, sr_ref, sg_ref,
              s0_ref, o_ref, s_ref, ss_ref, *, heads, nt, per_tile):
    t = pl.program_id(1)
    tb = q_ref.shape[1]
    tp = per_tile * tb
    c = GLA_CHUNK
    dk = q_ref.shape[2] // heads
    dv = v_ref.shape[2] // heads

    @pl.when(t == 0)
    def _():
        s_ref[...] = jnp.zeros_like(s_ref)

    o_s = _gla_sample_update(sq_ref, sk_ref, sv_ref, sr_ref, sg_ref, s0_ref, on_ref, ss_ref)
    o_s = o_s.astype(o_ref.dtype)

    row = lax.broadcasted_iota(jnp.int32, (tb, tb), 0)
    col = lax.broadcasted_iota(jnp.int32, (tb, tb), 1)
    same_chunk_causal = ((row >= col) & ((row ^ col) < c)).astype(BF16)
    g_rest = g_ref[0]
    bcum = jnp.zeros(g_rest.shape, F32)
    for _ in range(3):
        g_part = g_rest.astype(BF16)
        bcum = bcum + jnp.dot(same_chunk_causal, g_part, preferred_element_type=F32)
        g_rest = g_rest - g_part.astype(F32)
    q_i = (q_ref[0].astype(F32) * dk ** -0.5 * jnp.exp(bcum)).astype(BF16)
    k_i = (k_ref[0].astype(F32) * jnp.exp(-bcum)).astype(BF16)
    causal = (lax.broadcasted_iota(jnp.int32, (c, c), 0) >= lax.broadcasted_iota(jnp.int32, (c, c), 1))

    blk = pl.program_id(0) * nt + t
    out_base = (blk % per_tile) * tb
    for ci in range(tb // c):
        rows = slice(ci * c, (ci + 1) * c)
        out_rows = pl.ds(pl.multiple_of(out_base + ci * c, c), c)
        b_last = bcum[(ci + 1) * c - 1:(ci + 1) * c, :]
        k_e = (k_ref[0, rows, :].astype(F32) * jnp.exp(b_last - bcum[rows, :])).astype(BF16)
        decay_t = jnp.exp(jnp.broadcast_to(b_last, (SUBLANES, heads * dk))).T
        for h in range(heads):
            kcols = slice(h * dk, (h + 1) * dk)
            vcols = slice(h * dv, (h + 1) * dv)
            qs = q_i[rows, kcols]
            v = v_ref[0, rows, vcols]
            scores = lax.dot_general(qs, k_i[rows, kcols], (((1,), (1,)), ((), ())),
                                     preferred_element_type=F32)
            scores = jnp.where(causal, scores, 0.0).astype(BF16)
            s = s_ref[0, h]
            o = (jnp.dot(qs, s.astype(BF16), preferred_element_type=F32)
                 + jnp.dot(scores, v, preferred_element_type=F32))
            s_ref[0, h] = decay_t[kcols, 0:1] * s + lax.dot_general(
                k_e[:, kcols], v, (((0,), (0,)), ((), ())), preferred_element_type=F32)
            o = _rms_scale(o) * on_ref[...]
            o_ref[0, out_rows, vcols] = (o * r_ref[0, rows, vcols].astype(F32)).astype(o_ref.dtype)

    head_s = blk % heads
    for h in range(heads):
        @pl.when(head_s == h)
        def _(h=h):
            o_ref[0, tp:, h * dv:(h + 1) * dv] = o_s


def _gla(qkvr, g, layer, s0, onorm, *, batch, seq, heads, dk, dv, tp, ts, tb=256):
    tm = tp + ts
    nt = seq // tb
    per_tile = tp // tb
    s_blk = tp // ts
    hk, hv = heads * dk, heads * dv
    assert hv == 2 * hk
    assert batch * nt == N_TOKEN_TILES * heads and per_tile == heads
    qkvr3 = qkvr.reshape(N_TOKEN_TILES, tm, qkvr.shape[1])
    g3 = g.reshape(N_TOKEN_TILES, tm, hk)
    kq, vq, rq = heads, (2 * hk) // dv, (2 * hk) // dv + heads

    def tok(col):
        def index(b, t):
            blk = b * nt + t
            return (blk // per_tile, blk % per_tile, col)
        return index

    def smp(col0):
        def index(b, t):
            blk = b * nt + t
            return (blk // heads, s_blk, col0 + blk % heads)
        return index

    state = lambda b, t: ((b * nt + t) // heads, (b * nt + t) % heads, 0, 0)
    on_arg, on_spec = _layer_vector(onorm, layer)
    return pl.pallas_call(
        functools.partial(_gla_body, heads=heads, nt=nt, per_tile=per_tile),
        grid=(batch, nt),
        in_specs=[
            pl.BlockSpec((1, tb, hk), tok(0)),
            pl.BlockSpec((1, tb, hk), tok(1)),
            pl.BlockSpec((1, tb, hv), tok(1)),
            pl.BlockSpec((1, tb, hv), tok(2)),
            pl.BlockSpec((1, tb, hk), tok(0)),
            on_spec,
            pl.BlockSpec((1, ts, dk), smp(0)),
            pl.BlockSpec((1, ts, dk), smp(kq)),
            pl.BlockSpec((1, ts, dv), smp(vq)),
            pl.BlockSpec((1, ts, dv), smp(rq)),
            pl.BlockSpec((1, ts, dk), smp(0)),
            _layer_operand(s0, layer, (ts, 1, dk, dv), state)[1],
        ],
        out_specs=[
            pl.BlockSpec((1, tm, hv), lambda b, t: ((b * nt + t) // per_tile, 0, 0)),
            pl.BlockSpec((1, heads, dk, dv), lambda b, t: (b, 0, 0, 0)),
            pl.BlockSpec((ts, 1, dk, dv), state),
        ],
        out_shape=[
            jax.ShapeDtypeStruct((N_TOKEN_TILES, tm, hv), BF16),
            jax.ShapeDtypeStruct((batch, heads, dk, dv), F32),
            jax.ShapeDtypeStruct(s0.shape[1:], F32),
        ],
        compiler_params=_cparams("arbitrary", "arbitrary"),
        name="gla",
    )(qkvr3, qkvr3, qkvr3, qkvr3, g3, on_arg, qkvr3, qkvr3, qkvr3, qkvr3, g3, s0)


def kernel(x_prompt, x_sample, state_rglru_h, state_rglru_conv, state_gla_S, ln_ffn1, ffn1_w1, ffn1_w3, ffn1_w2, ln_mix, ln_ffn2, ffn2_w1, ffn2_w3, ffn2_w2, rg_w_y, rg_w_x, rg_conv_w, rg_conv_b, rg_w_a, rg_b_a, rg_w_i, rg_b_i, rg_lambda, rg_w_o, gla_w_q, gla_w_k, gla_w_v, gla_w_g1, gla_w_g2, gla_b_g, gla_w_r, gla_onorm, gla_w_o, ln_final):
    batch, seq, d = x_prompt.shape
    n_dec = x_sample.shape[0]
    depth = ln_ffn1.shape[0]
    heads, dk, dv = state_gla_S.shape[2:]
    tp, ts = _token_tiles(batch * seq, n_dec)
    tm = tp + ts
    tiles = dict(tp=tp, ts=ts)

    x = (x_prompt.reshape(batch * seq, d), x_sample.reshape(n_dec, d))

    ffn_f32 = []
    for i in range(depth):
        ffn_f32 += [((ffn1_w1, ffn1_w3, ffn1_w2), i), ((ffn2_w1, ffn2_w3, ffn2_w2), i)]
    ffn_f32.append(None)
    ffn_w = tuple(_to_bf16(w, 0) for w in ffn_f32[0][0])

    p_h, p_conv, p_s, s_h, s_conv, s_s = [], [], [], [], [], []
    for i in range(depth):
        j = i // 2
        x, u, ffn_w = _ffn(x, i, ln_ffn1, ffn_w, g_mix=ln_mix, next_weights=ffn_f32[2 * i + 1], **tiles)
        if i % 2 == 0:
            proj = _matmul(u, j, [rg_w_y, rg_w_x], [_gelu_tanh, _identity], tm=tm)
            params = _rg_params(j, rg_conv_w, rg_conv_b, rg_w_a, rg_b_a, rg_w_i, rg_b_i, rg_lambda)
            mix, hl, cl = _rg_prompt(proj, params, batch=batch, seq=seq, **tiles)
            mix, hn, cn = _rg_sample(mix, proj, j, state_rglru_conv, state_rglru_h, params, **tiles)
            p_h.append(hl.reshape(batch, d))
            p_conv.append(cl)
            s_h.append(hn)
            s_conv.append(cn.reshape(n_dec, CONV_W - 1, d))
            w_o = rg_w_o
        else:
            qkvr = _matmul(u, j, [gla_w_q, gla_w_k, gla_w_v, gla_w_r],
                           [_identity, _identity, _identity, _silu], tm=tm, out_dtype=BF16)
            rank = gla_w_g1.shape[2]
            w_g1 = jnp.pad(gla_w_g1[j], ((0, 0), (0, LANES - rank)))
            w_g2 = jnp.pad(gla_w_g2[j], ((0, LANES - rank), (0, 0)))
            g = _gla_gate(u, w_g1, w_g2, gla_b_g[j], tm=tm)
            mix, s_fin, s_new = _gla(qkvr, g, j, state_gla_S, gla_onorm, batch=batch, seq=seq,
                                     heads=heads, dk=dk, dv=dv, **tiles)
            p_s.append(s_fin)
            s_s.append(s_new)
            w_o = gla_w_o
        x = _matmul(mix.reshape(N_TOKEN_TILES * tm, d), j, [w_o], [_identity], res=x, tm=tm)
        g_final = ln_final.reshape(1, d) if i == depth - 1 else None
        tf = 512 if g_final is None else 256
        x, _, ffn_w = _ffn(x, i, ln_ffn2, ffn_w, g_final=g_final, next_weights=ffn_f32[2 * i + 2], tf=tf,
                           **tiles)

    y_prompt = x[0].reshape(batch, seq, d)
    y_sample = x[1].reshape(n_dec, 1, d)
    return (y_prompt, y_sample, jnp.stack(p_h), jnp.stack(p_conv), jnp.stack(p_s),
            jnp.stack(s_h), jnp.stack(s_conv), jnp.stack(s_s))
```

```python
import functools

import jax
import jax.numpy as jnp
from jax import lax
from jax.experimental import pallas as pl
from jax.experimental.pallas import tpu as pltpu

F32 = jnp.float32
BF16 = jnp.bfloat16

RMS_EPS = 1e-6
RG_C = 8.0
CONV_W = 4
GLA_GATE_NORM = 16.0
GLA_CHUNK = 64

V7X_VMEM_LIMIT_BYTES = 62 * 1024 * 1024
SUBLANES = 8
LANES = 128
BF16_SUBLANES = 16
N_TOKEN_TILES = 8


def _cparams(*sem):
    return pltpu.CompilerParams(dimension_semantics=sem, vmem_limit_bytes=V7X_VMEM_LIMIT_BYTES)


def _rms_scale(x):
    return x * lax.rsqrt(jnp.mean(x * x, axis=-1, keepdims=True) + RMS_EPS)


def _silu(x):
    return x * jax.nn.sigmoid(x)


def _gelu_tanh(y):
    return jax.nn.gelu(y, approximate=True)


def _identity(y):
    return y


def _softplus(x):
    return jnp.maximum(x, 0.0) + jnp.log1p(jnp.exp(-jnp.abs(x)))


ROW_CHUNK = BF16_SUBLANES


def _for_row_chunks(n_rows, body):
    n = n_rows // ROW_CHUNK
    assert n * ROW_CHUNK == n_rows
    unroll = next(u for u in (8, 5, 4, 3, 2, 1) if n % u == 0)

    def step(c, carry):
        body(pl.multiple_of(c * ROW_CHUNK, ROW_CHUNK))
        return carry

    lax.fori_loop(0, n, step, 0, unroll=unroll)


def _chunk_rows(base, off):
    assert base % ROW_CHUNK == 0
    return pl.ds(pl.multiple_of(base + off, ROW_CHUNK), ROW_CHUNK)


def _layer_operand(stacked, layer, block, index_map):
    spec = pl.BlockSpec((None,) + tuple(block), lambda *g: (layer,) + tuple(index_map(*g)))
    return stacked, spec


def _layer_vector(stacked, layer):
    n_layers, n = stacked.shape
    return _layer_operand(stacked.reshape(n_layers, 1, n), layer, (1, n), lambda *g: (0, 0))


def _layer_full(stacked, layer):
    block = stacked.shape[1:]
    return _layer_operand(stacked, layer, block, lambda *g: (0,) * len(block))


def _token_tiles(n_prompt, n_sample):
    tp, ts = n_prompt // N_TOKEN_TILES, n_sample // N_TOKEN_TILES
    assert tp * N_TOKEN_TILES == n_prompt and ts * N_TOKEN_TILES == n_sample
    assert tp % BF16_SUBLANES == 0 and ts % BF16_SUBLANES == 0 and tp % ts == 0
    return tp, ts


def _cast_body(w_ref, o_ref):
    o_ref[...] = w_ref[...].astype(o_ref.dtype)


def _to_bf16(w, layer, *, rows=512):
    _, r, c = w.shape
    return pl.pallas_call(
        _cast_body,
        grid=(r // rows,),
        in_specs=[pl.BlockSpec((None, rows, c), lambda i: (layer, i, 0))],
        out_specs=pl.BlockSpec((rows, c), lambda i: (i, 0)),
        out_shape=jax.ShapeDtypeStruct((r, c), BF16),
        compiler_params=_cparams("arbitrary"),
        name="to_bf16",
    )(w)


def _ffn_body(*refs, tp, ts, split_in, split_out, mix_norm, cast_next):
    refs = list(refs)
    x_refs = [refs.pop(0) for _ in range(2 if split_in else 1)]
    g_ref, w1_ref, w3_ref, w2_ref = (refs.pop(0) for _ in range(4))
    g2_ref = refs.pop(0) if (split_out or mix_norm) else None
    next_f32 = [refs.pop(0) for _ in range(3 if cast_next else 0)]
    o_refs = [refs.pop(0) for _ in range(2 if split_out else 1)]
    u_hbm = refs.pop(0) if mix_norm else None
    next_bf16 = [refs.pop(0) for _ in range(3 if cast_next else 0)]
    xn_ref = refs.pop(0)
    acc_ref = refs.pop(0) if split_out else o_refs[0]
    (sem,) = refs
    tm = tp + ts
    i = pl.program_id(0)
    f = pl.program_id(1)

    def segments(blocks):
        if len(blocks) == 2:
            return [(0, tp, blocks[0]), (tp, ts, blocks[1])]
        return [(0, tm, blocks[0])]

    @pl.when(f == 0)
    def _():
        for t0, n_rows, x_ref in segments(x_refs):
            def chunk(off, x_ref=x_ref, t0=t0):
                x = x_ref[_chunk_rows(0, off), :]
                xn_ref[_chunk_rows(t0, off), :] = (_rms_scale(x) * g_ref[...]).astype(BF16)
                acc_ref[_chunk_rows(t0, off), :] = 2.0 * x

            _for_row_chunks(n_rows, chunk)

    xn = xn_ref[...]
    h1 = jnp.dot(xn, w1_ref[...].astype(BF16), preferred_element_type=F32)
    h3 = jnp.dot(xn, w3_ref[...].astype(BF16), preferred_element_type=F32)
    h = (_silu(h1) * h3).astype(BF16)
    acc_ref[...] += jnp.dot(h, w2_ref[...].astype(BF16), preferred_element_type=F32)

    for src_ref, dst_ref in zip(next_f32, next_bf16):
        dst_ref[...] = src_ref[...].astype(BF16)

    @pl.when(f == pl.num_programs(1) - 1)
    def _():
        for t0, n_rows, o_ref in segments(o_refs):
            def chunk(off, o_ref=o_ref, t0=t0):
                y = 0.5 * acc_ref[_chunk_rows(t0, off), :]
                if split_out:
                    y = _rms_scale(y) * g2_ref[...]
                o_ref[_chunk_rows(0, off), :] = y
                if mix_norm:
                    xn_ref[_chunk_rows(t0, off), :] = (_rms_scale(y) * g2_ref[...]).astype(BF16)

            _for_row_chunks(n_rows, chunk)
        if mix_norm:
            cu = pltpu.make_async_copy(xn_ref, u_hbm.at[pl.ds(i * tm, tm)], sem.at[0])
            cu.start()
            cu.wait()


def _ffn(x, layer, g, weights, *, tp, ts, g_mix=None, g_final=None, next_weights=None, tf=512):
    split_in = isinstance(x, tuple)
    split_out = g_final is not None
    mix_norm = g_mix is not None
    cast_next = next_weights is not None
    assert not (split_out and mix_norm)
    xs = list(x) if split_in else [x]
    w1, w3, w2 = weights
    d, f = w1.shape
    tm = tp + ts
    nf = f // tf
    operands = [
        _layer_vector(g, layer),
        (w1, pl.BlockSpec((d, tf), lambda i, j: (0, j))),
        (w3, pl.BlockSpec((d, tf), lambda i, j: (0, j))),
        (w2, pl.BlockSpec((tf, d), lambda i, j: (j, 0))),
    ]
    if mix_norm:
        operands.append(_layer_vector(g_mix, layer))
    if split_out:
        operands.append(_layer_vector(g_final, 0))
    row_tile = lambda rows: pl.BlockSpec((rows, d), lambda i, j: (i, 0))
    x_specs = [row_tile(tp), row_tile(ts)] if split_in else [row_tile(tm)]
    if split_out:
        out_specs = [row_tile(tp), row_tile(ts)]
        out_shape = [jax.ShapeDtypeStruct((tp * N_TOKEN_TILES, d), F32),
                     jax.ShapeDtypeStruct((ts * N_TOKEN_TILES, d), F32)]
    else:
        out_specs = [row_tile(tm)]
        out_shape = [jax.ShapeDtypeStruct((tm * N_TOKEN_TILES, d), F32)]
    if mix_norm:
        out_specs.append(pl.BlockSpec(memory_space=pl.ANY))
        out_shape.append(jax.ShapeDtypeStruct((tm * N_TOKEN_TILES, d), BF16))
    if cast_next:
        (n1, n3, n2), next_layer = next_weights
        db = d // N_TOKEN_TILES
        assert n1.shape[1:] == (d, f) and n2.shape[1:] == (f, d) and db * N_TOKEN_TILES == d
        operands += [
            _layer_operand(n1, next_layer, (db, tf), lambda i, j: (i, j)),
            _layer_operand(n3, next_layer, (db, tf), lambda i, j: (i, j)),
            _layer_operand(n2, next_layer, (tf, db), lambda i, j: (j, i)),
        ]
        out_specs += [pl.BlockSpec((db, tf), lambda i, j: (i, j)), pl.BlockSpec((db, tf), lambda i, j: (i, j)),
                      pl.BlockSpec((tf, db), lambda i, j: (j, i))]
        out_shape += [jax.ShapeDtypeStruct((d, f), BF16), jax.ShapeDtypeStruct((d, f), BF16),
                      jax.ShapeDtypeStruct((f, d), BF16)]
    out = pl.pallas_call(
        functools.partial(_ffn_body, tp=tp, ts=ts, split_in=split_in, split_out=split_out,
                          mix_norm=mix_norm, cast_next=cast_next),
        grid=(N_TOKEN_TILES, nf),
        in_specs=x_specs + [spec for _, spec in operands],
        out_specs=out_specs,
        out_shape=out_shape,
        scratch_shapes=[pltpu.VMEM((tm, d), BF16)] + ([pltpu.VMEM((tm, d), F32)] if split_out else [])
        + [pltpu.SemaphoreType.DMA((1,))],
        compiler_params=_cparams("arbitrary", "arbitrary"),
        name="ffn",
    )(*xs, *[arr for arr, _ in operands])
    out = list(out)
    acts = tuple(out[:2]) if split_out else out[0]
    rest = out[2:] if split_out else out[1:]
    u = rest.pop(0) if mix_norm else None
    return acts, u, (tuple(rest) if cast_next else None)


def _matmul_body(u_ref, *refs, tiles, layer, has_res):
    n_src = 1 + max(src for src, _, _ in tiles)
    w_hbm = refs[:n_src]
    rest = refs[n_src:]
    if has_res:
        r_ref, o_ref, stage_ref, wb_ref, sem = rest
    else:
        o_ref, stage_ref, wb_ref, sem = rest
    tn = stage_ref.shape[1]
    j = pl.program_id(0)
    i = pl.program_id(1)

    def weight_copy(jj):
        src, col0, _ = tiles[jj]
        return pltpu.make_async_copy(w_hbm[src].at[layer, :, pl.ds(col0, tn)], stage_ref, sem.at[0])

    def at_tile(jj, fn):
        pl.when(j == jj)(fn)

    @pl.when(i == 0)
    def _():
        at_tile(0, lambda: weight_copy(0).start())
        for jj in range(len(tiles)):
            at_tile(jj, lambda jj=jj: weight_copy(jj).wait())
        wb_ref[...] = stage_ref[...].astype(BF16)
        for jj in range(len(tiles) - 1):
            at_tile(jj, lambda jj=jj: weight_copy(jj + 1).start())

    for act in dict.fromkeys(act for _, _, act in tiles):
        uses = functools.reduce(jnp.logical_or, [j == jj for jj, t in enumerate(tiles) if t[2] is act])

        @pl.when(uses)
        def _(act=act):
            tm = u_ref.shape[0]
            cuts = [0] + [tm - tm // div // BF16_SUBLANES * BF16_SUBLANES for div in (2, 8)] + [tm]
            for rows in (slice(lo, hi) for lo, hi in zip(cuts[:-1], cuts[1:])):
                y = act(jnp.dot(u_ref[rows, :], wb_ref[...], preferred_element_type=F32))
                if has_res:
                    y = r_ref[rows, :] + y
                o_ref[rows, :] = y.astype(o_ref.dtype)


def _matmul(u, layer, ws, acts, res=None, *, tm, tn=1024, out_dtype=F32):
    m, d = u.shape
    tiles = []
    for src, (w, act) in enumerate(zip(ws, acts)):
        assert w.shape[2] % tn == 0
        tiles += [(src, col0, act) for col0 in range(0, w.shape[2], tn)]
    n = len(tiles) * tn
    io_spec = lambda: pl.BlockSpec((tm, tn), lambda j, i: (i, j))
    has_res = res is not None
    return pl.pallas_call(
        functools.partial(_matmul_body, tiles=tuple(tiles), layer=layer, has_res=has_res),
        grid=(len(tiles), m // tm),
        in_specs=[pl.BlockSpec((tm, d), lambda j, i: (i, 0))]
        + [pl.BlockSpec(memory_space=pl.ANY)] * len(ws) + ([io_spec()] if has_res else []),
        out_specs=io_spec(),
        out_shape=jax.ShapeDtypeStruct((m, n), out_dtype),
        scratch_shapes=[pltpu.VMEM((d, tn), F32), pltpu.VMEM((d, tn), BF16),
                        pltpu.SemaphoreType.DMA((1,))],
        compiler_params=_cparams("arbitrary", "arbitrary"),
        name="matmul",
    )(u, *ws, *([res] if has_res else []))


def _gla_gate_body(u_ref, w1_ref, w2_ref, b_ref, o_ref):
    t = jnp.dot(u_ref[...], w1_ref[...].astype(BF16), preferred_element_type=F32)
    logit = jnp.dot(t.astype(BF16), w2_ref[...].astype(BF16), preferred_element_type=F32) + b_ref[...]
    o_ref[...] = -_softplus(-logit) * (1.0 / GLA_GATE_NORM)


def _gla_gate(u, w_g1, w_g2, b_g, *, tm):
    m, d = u.shape
    rank, n = w_g2.shape
    return pl.pallas_call(
        _gla_gate_body,
        grid=(m // tm,),
        in_specs=[
            pl.BlockSpec((tm, d), lambda i: (i, 0)),
            pl.BlockSpec((d, rank), lambda i: (0, 0)),
            pl.BlockSpec((rank, n), lambda i: (0, 0)),
            pl.BlockSpec((1, n), lambda i: (0, 0)),
        ],
        out_specs=pl.BlockSpec((tm, n), lambda i: (i, 0)),
        out_shape=jax.ShapeDtypeStruct((m, n), F32),
        compiler_params=_cparams("arbitrary"),
        name="gla_gate",
    )(u, w_g1, w_g2, b_g.reshape(1, n))


def _prompt_rows_of_tile(o_ref, nt, per_tile, tb):
    blk = pl.program_id(0) * nt + pl.program_id(1)
    k = blk % per_tile
    tp = per_tile * tb

    @pl.when(k == 0)
    def _():
        o_ref[0, tp:, :] = jnp.zeros((o_ref.shape[1] - tp, o_ref.shape[2]), o_ref.dtype)

    return pl.ds(pl.multiple_of(k * tb, tb), tb)


def _sigmoid(x):
    return 0.5 * jnp.tanh(0.5 * x) + 0.5


def _rglru_coeffs(conv_out, wa_ref, ba_ref, wi_ref, bi_ref, lam_ref, store):
    n_blocks, bw, _ = wa_ref.shape
    for n in range(n_blocks):
        cols = slice(n * bw, (n + 1) * bw)
        xb = conv_out(cols)
        xb16 = xb.astype(BF16)
        wa = wa_ref[n].astype(BF16)
        wi = wi_ref[n].astype(BF16)
        r = _sigmoid(jnp.dot(xb16, wa, preferred_element_type=F32) + ba_ref[:, cols])
        i = _sigmoid(jnp.dot(xb16, wi, preferred_element_type=F32) + bi_ref[:, cols])
        log_a = -RG_C * r * _softplus(-lam_ref[:, cols])
        a = jnp.exp(log_a)
        b = jnp.sqrt(1.0 - a * a) * (i * xb)
        store(n, a, b)


def _rg_prompt_body(xw_ref, gate_ref, cw_ref, cb_ref, wa_ref, ba_ref, wi_ref, bi_ref, lam_ref,
                    o_ref, hl_ref, cl_ref, ext_ref, a_ref, b_ref, h_ref, *, nt, per_tile):
    t = pl.program_id(1)
    tb = xw_ref.shape[1]
    bw = wa_ref.shape[1]
    pad = SUBLANES

    @pl.when(t == 0)
    def _():
        ext_ref[0:pad, :] = jnp.zeros((pad, ext_ref.shape[1]), F32)
        h_ref[...] = jnp.zeros_like(h_ref)

    ext_ref[pad:pad + tb, :] = xw_ref[0]

    def conv_out(cols):
        ext = ext_ref[:, cols]
        y = ext * cw_ref[0:1, cols]
        for j in range(1, CONV_W):
            y = ext * cw_ref[j:j + 1, cols] + pltpu.roll(y, 1, axis=0)
        return cb_ref[:, cols] + y[pad:, :]

    def store(n, a, b):
        a_ref[:, n * bw:(n + 1) * bw] = a
        b_ref[:, n * bw:(n + 1) * bw] = b

    _rglru_coeffs(conv_out, wa_ref, ba_ref, wi_ref, bi_ref, lam_ref, store)

    def step(r, h):
        h = a_ref[pl.ds(r, 1), :] * h + b_ref[pl.ds(r, 1), :]
        b_ref[pl.ds(r, 1), :] = h
        return h

    h = lax.fori_loop(0, tb, step, h_ref[0:1, :], unroll=8)
    h_ref[0:1, :] = h
    out_rows = _prompt_rows_of_tile(o_ref, nt, per_tile, tb)
    o_ref[0, out_rows, :] = (b_ref[...] * gate_ref[0]).astype(o_ref.dtype)
    hl_ref[0] = h
    cl_ref[0] = ext_ref[pad + tb - (CONV_W - 1):pad + tb, :]
    ext_ref[0:pad, :] = ext_ref[tb:tb + pad, :]


def _rg_params(layer, conv_w, conv_b, w_a, b_a, w_i, b_i, lam):
    operands = [_layer_full(conv_w, layer), _layer_vector(conv_b, layer), _layer_full(w_a, layer),
                _layer_vector(b_a, layer), _layer_full(w_i, layer), _layer_vector(b_i, layer),
                _layer_vector(lam, layer)]
    return [arr for arr, _ in operands], [spec for _, spec in operands]


def _rg_prompt(proj, params, *, batch, seq, tp, ts, tb=256):
    p_args, p_specs = params
    m, d2 = proj.shape
    d = d2 // 2
    tm = tp + ts
    nt = seq // tb
    per_tile = tp // tb
    proj3 = proj.reshape(N_TOKEN_TILES, tm, d2)

    def tok(col):
        def index(b, t):
            blk = b * nt + t
            return (blk // per_tile, blk % per_tile, col)
        return index

    mix, hl, cl = pl.pallas_call(
        functools.partial(_rg_prompt_body, nt=nt, per_tile=per_tile),
        grid=(batch, nt),
        in_specs=[pl.BlockSpec((1, tb, d), tok(1)), pl.BlockSpec((1, tb, d), tok(0))] + p_specs,
        out_specs=[
            pl.BlockSpec((1, tm, d), lambda b, t: ((b * nt + t) // per_tile, 0, 0)),
            pl.BlockSpec((1, 1, d), lambda b, t: (b, 0, 0)),
            pl.BlockSpec((1, CONV_W - 1, d), lambda b, t: (b, 0, 0)),
        ],
        out_shape=[
            jax.ShapeDtypeStruct((N_TOKEN_TILES, tm, d), BF16),
            jax.ShapeDtypeStruct((batch, 1, d), F32),
            jax.ShapeDtypeStruct((batch, CONV_W - 1, d), F32),
        ],
        scratch_shapes=[
            pltpu.VMEM((tb + SUBLANES, d), F32),
            pltpu.VMEM((tb, d), F32),
            pltpu.VMEM((tb, d), F32),
            pltpu.VMEM((SUBLANES, d), F32),
        ],
        compiler_params=_cparams("arbitrary", "arbitrary"),
        name="rg_prompt",
    )(proj3, proj3, *p_args)
    return mix, hl, cl


def _rg_sample_body(mix_ref, xw_ref, gate_ref, conv_ref, h0_ref, cw_ref, cb_ref, wa_ref, ba_ref,
                    wi_ref, bi_ref, lam_ref, o_ref, hn_ref, cn_ref, a_ref, b_ref):
    del mix_ref
    n, d = h0_ref.shape
    bw = wa_ref.shape[1]
    x = xw_ref[...].reshape(n, d)
    xc = cb_ref[...]
    for j in range(CONV_W - 1):
        xc = xc + conv_ref[:, j * d:(j + 1) * d] * cw_ref[j:j + 1, :]
    xc = xc + x * cw_ref[CONV_W - 1:CONV_W, :]

    def store(k, a, b):
        a_ref[:, k * bw:(k + 1) * bw] = a
        b_ref[:, k * bw:(k + 1) * bw] = b

    _rglru_coeffs(lambda cols: xc[:, cols], wa_ref, ba_ref, wi_ref, bi_ref, lam_ref, store)
    h = b_ref[...] + a_ref[...] * h0_ref[...]
    hn_ref[...] = h
    o_ref[...] = (h * gate_ref[...].reshape(n, d)).reshape(o_ref.shape).astype(o_ref.dtype)
    cn_ref[:, 0:(CONV_W - 2) * d] = conv_ref[:, d:(CONV_W - 1) * d]
    cn_ref[:, (CONV_W - 2) * d:] = x


def _rg_sample(mix, proj, layer, conv_state, h0, params, *, tp, ts):
    p_args, p_specs = params
    n_layers, n, d = h0.shape
    tm = tp + ts
    proj3 = proj.reshape(N_TOKEN_TILES, tm, 2 * d)
    s_blk = tp // ts
    full2 = lambda shape: pl.BlockSpec(shape, lambda i: (0, 0))
    kc = (CONV_W - 1) * d
    conv_arg, conv_spec = _layer_full(conv_state.reshape(n_layers, n, kc), layer)
    h0_arg, h0_spec = _layer_full(h0, layer)
    return pl.pallas_call(
        _rg_sample_body,
        grid=(1,),
        in_specs=[
            pl.BlockSpec(memory_space=pl.ANY),
            pl.BlockSpec((N_TOKEN_TILES, ts, d), lambda i: (0, s_blk, 1)),
            pl.BlockSpec((N_TOKEN_TILES, ts, d), lambda i: (0, s_blk, 0)),
            conv_spec, h0_spec,
        ] + p_specs,
        out_specs=[
            pl.BlockSpec((N_TOKEN_TILES, ts, d), lambda i: (0, s_blk, 0)),
            full2((n, d)),
            full2((n, kc)),
        ],
        out_shape=[
            jax.ShapeDtypeStruct(mix.shape, mix.dtype),
            jax.ShapeDtypeStruct((n, d), F32),
            jax.ShapeDtypeStruct((n, kc), F32),
        ],
        scratch_shapes=[pltpu.VMEM((n, d), F32), pltpu.VMEM((n, d), F32)],
        input_output_aliases={0: 0},
        compiler_params=_cparams("arbitrary"),
        name="rg_sample",
    )(mix, proj3, proj3, conv_arg, h0_arg, *p_args)


def _gla_sample_update(q_ref, k_ref, v_ref, r_ref, g_ref, s0_ref, on_ref, s_ref):
    bb, dk = q_ref.shape[1:]
    alpha_t = jnp.exp(g_ref[0]).T
    k_t = k_ref[0].astype(F32).T
    q_t = (q_ref[0].astype(F32) * dk ** -0.5).T
    v = v_ref[0].astype(F32)
    rows = []
    for b in range(bb):
        s_new = alpha_t[:, b:b + 1] * s0_ref[b, 0] + k_t[:, b:b + 1] * v[b:b + 1, :]
        s_ref[b, 0] = s_new
        rows.append(jnp.sum(q_t[:, b:b + 1] * s_new, axis=0, keepdims=True))
    o = jnp.concatenate(rows, axis=0)
    o = _rms_scale(o) * on_ref[...]
    return o * r_ref[0].astype(F32)


def _gla_body(q_ref, k_ref, v_ref, r_ref, g_ref, on_ref, sq_ref, sk_ref, sv_ref, sr_ref, sg_ref,
              s0_ref, o_ref, s_ref, ss_ref, *, heads, nt, per_tile):
    t = pl.program_id(1)
    tb = q_ref.shape[1]
    tp = per_tile * tb
    c = GLA_CHUNK
    dk = q_ref.shape[2] // heads
    dv = v_ref.shape[2] // heads

    @pl.when(t == 0)
    def _():
        s_ref[...] = jnp.zeros_like(s_ref)

    o_s = _gla_sample_update(sq_ref, sk_ref, sv_ref, sr_ref, sg_ref, s0_ref, on_ref, ss_ref)
    o_s = o_s.astype(o_ref.dtype)

    row = lax.broadcasted_iota(jnp.int32, (tb, tb), 0)
    col = lax.broadcasted_iota(jnp.int32, (tb, tb), 1)
    same_chunk_causal = ((row >= col) & ((row ^ col) < c)).astype(BF16)
    g_rest = g_ref[0]
    bcum = jnp.zeros(g_rest.shape, F32)
    for _ in range(3):
        g_part = g_rest.astype(BF16)
        bcum = bcum + jnp.dot(same_chunk_causal, g_part, preferred_element_type=F32)
        g_rest = g_rest - g_part.astype(F32)
    nc = tb // c
    chunk_rows = [slice(ci * c, (ci + 1) * c) for ci in range(nc)]
    q = q_ref[0].astype(F32) * dk ** -0.5
    k = k_ref[0].astype(F32)
    q_i = q * jnp.exp(bcum)
    k_i = (k * jnp.exp(-bcum)).astype(BF16)
    g_tot = [bcum[(ci + 1) * c - 1:(ci + 1) * c, :] for ci in range(nc)]
    g_before = [jnp.zeros_like(g_tot[0])]
    for ci in range(nc):
        g_before.append(g_before[-1] + g_tot[ci])
    k_e = [k[rows, :] * jnp.exp(g_tot[ci] - bcum[rows, :]) for ci, rows in enumerate(chunk_rows)]
    q_in = jnp.concatenate([(q_i[rows, :] * jnp.exp(g_before[ci])).astype(BF16)
                            for ci, rows in enumerate(chunk_rows)], axis=0)
    k_out = jnp.concatenate([(k_e[ci] * jnp.exp(g_before[nc] - g_before[ci + 1])).astype(BF16)
                             for ci in range(nc)], axis=0)
    decay_t = jnp.exp(jnp.broadcast_to(g_before[nc], (SUBLANES, heads * dk))).T
    q_i = q_i.astype(BF16)

    blk = pl.program_id(0) * nt + t
    out_base = (blk % per_tile) * tb
    o_inter = []
    for h in range(heads):
        kcols = slice(h * dk, (h + 1) * dk)
        vcols = slice(h * dv, (h + 1) * dv)
        s = s_ref[0, h]
        o_inter.append(jnp.dot(q_in[:, kcols], s.astype(BF16), preferred_element_type=F32))
        s_ref[0, h] = decay_t[kcols, 0:1] * s + lax.dot_general(
            k_out[:, kcols], v_ref[0, :, vcols], (((0,), (0,)), ((), ())), preferred_element_type=F32)

    for ci, rows in enumerate(chunk_rows):
        out_rows = pl.ds(pl.multiple_of(out_base + ci * c, c), c)
        n_keys = (ci + 1) * c
        keys = jnp.concatenate(
            [(k_e[cj] * jnp.exp(g_before[ci] - g_before[cj + 1])).astype(BF16) for cj in range(ci)]
            + [k_i[rows, :]], axis=0)
        visible = (lax.broadcasted_iota(jnp.int32, (c, n_keys), 1)
                   <= lax.broadcasted_iota(jnp.int32, (c, n_keys), 0) + ci * c)
        for h in range(heads):
            kcols = slice(h * dk, (h + 1) * dk)
            vcols = slice(h * dv, (h + 1) * dv)
            scores = lax.dot_general(q_i[rows, kcols], keys[:, kcols], (((1,), (1,)), ((), ())),
                                     preferred_element_type=F32)
            scores = jnp.where(visible, scores, 0.0).astype(BF16)
            o = o_inter[h][rows, :] + jnp.dot(scores, v_ref[0, 0:n_keys, vcols],
                                              preferred_element_type=F32)
            o = _rms_scale(o) * on_ref[...]
            o_ref[0, out_rows, vcols] = (o * r_ref[0, rows, vcols].astype(F32)).astype(o_ref.dtype)

    head_s = blk % heads
    for h in range(heads):
        @pl.when(head_s == h)
        def _(h=h):
            o_ref[0, tp:, h * dv:(h + 1) * dv] = o_s


def _gla(qkvr, g, layer, s0, onorm, *, batch, seq, heads, dk, dv, tp, ts, tb=256):
    tm = tp + ts
    nt = seq // tb
    per_tile = tp // tb
    s_blk = tp // ts
    hk, hv = heads * dk, heads * dv
    assert hv == 2 * hk
    assert batch * nt == N_TOKEN_TILES * heads and per_tile == heads
    qkvr3 = qkvr.reshape(N_TOKEN_TILES, tm, qkvr.shape[1])
    g3 = g.reshape(N_TOKEN_TILES, tm, hk)
    kq, vq, rq = heads, (2 * hk) // dv, (2 * hk) // dv + heads

    def tok(col):
        def index(b, t):
            blk = b * nt + t
            return (blk // per_tile, blk % per_tile, col)
        return index

    def smp(col0):
        def index(b, t):
            blk = b * nt + t
            return (blk // heads, s_blk, col0 + blk % heads)
        return index

    state = lambda b, t: ((b * nt + t) // heads, (b * nt + t) % heads, 0, 0)
    on_arg, on_spec = _layer_vector(onorm, layer)
    return pl.pallas_call(
        functools.partial(_gla_body, heads=heads, nt=nt, per_tile=per_tile),
        grid=(batch, nt),
        in_specs=[
            pl.BlockSpec((1, tb, hk), tok(0)),
            pl.BlockSpec((1, tb, hk), tok(1)),
            pl.BlockSpec((1, tb, hv), tok(1)),
            pl.BlockSpec((1, tb, hv), tok(2)),
            pl.BlockSpec((1, tb, hk), tok(0)),
            on_spec,
            pl.BlockSpec((1, ts, dk), smp(0)),
            pl.BlockSpec((1, ts, dk), smp(kq)),
            pl.BlockSpec((1, ts, dv), smp(vq)),
            pl.BlockSpec((1, ts, dv), smp(rq)),
            pl.BlockSpec((1, ts, dk), smp(0)),
            _layer_operand(s0, layer, (ts, 1, dk, dv), state)[1],
        ],
        out_specs=[
            pl.BlockSpec((1, tm, hv), lambda b, t: ((b * nt + t) // per_tile, 0, 0)),
            pl.BlockSpec((1, heads, dk, dv), lambda b, t: (b, 0, 0, 0)),
            pl.BlockSpec((ts, 1, dk, dv), state),
        ],
        out_shape=[
            jax.ShapeDtypeStruct((N_TOKEN_TILES, tm, hv), BF16),
            jax.ShapeDtypeStruct((batch, heads, dk, dv), F32),
            jax.ShapeDtypeStruct(s0.shape[1:], F32),
        ],
        compiler_params=_cparams("arbitrary", "arbitrary"),
        name="gla",
    )(qkvr3, qkvr3, qkvr3, qkvr3, g3, on_arg, qkvr3, qkvr3, qkvr3, qkvr3, g3, s0)


def kernel(x_prompt, x_sample, state_rglru_h, state_rglru_conv, state_gla_S, ln_ffn1, ffn1_w1, ffn1_w3, ffn1_w2, ln_mix, ln_ffn2, ffn2_w1, ffn2_w3, ffn2_w2, rg_w_y, rg_w_x, rg_conv_w, rg_conv_b, rg_w_a, rg_b_a, rg_w_i, rg_b_i, rg_lambda, rg_w_o, gla_w_q, gla_w_k, gla_w_v, gla_w_g1, gla_w_g2, gla_b_g, gla_w_r, gla_onorm, gla_w_o, ln_final):
    batch, seq, d = x_prompt.shape
    n_dec = x_sample.shape[0]
    depth = ln_ffn1.shape[0]
    heads, dk, dv = state_gla_S.shape[2:]
    tp, ts = _token_tiles(batch * seq, n_dec)
    tm = tp + ts
    tiles = dict(tp=tp, ts=ts)

    x = (x_prompt.reshape(batch * seq, d), x_sample.reshape(n_dec, d))

    ffn_f32 = []
    for i in range(depth):
        ffn_f32 += [((ffn1_w1, ffn1_w3, ffn1_w2), i), ((ffn2_w1, ffn2_w3, ffn2_w2), i)]
    ffn_f32.append(None)
    ffn_w = tuple(_to_bf16(w, 0) for w in ffn_f32[0][0])

    p_h, p_conv, p_s, s_h, s_conv, s_s = [], [], [], [], [], []
    for i in range(depth):
        j = i // 2
        x, u, ffn_w = _ffn(x, i, ln_ffn1, ffn_w, g_mix=ln_mix, next_weights=ffn_f32[2 * i + 1], **tiles)
        if i % 2 == 0:
            proj = _matmul(u, j, [rg_w_y, rg_w_x], [_gelu_tanh, _identity], tm=tm)
            params = _rg_params(j, rg_conv_w, rg_conv_b, rg_w_a, rg_b_a, rg_w_i, rg_b_i, rg_lambda)
            mix, hl, cl = _rg_prompt(proj, params, batch=batch, seq=seq, **tiles)
            mix, hn, cn = _rg_sample(mix, proj, j, state_rglru_conv, state_rglru_h, params, **tiles)
            p_h.append(hl.reshape(batch, d))
            p_conv.append(cl)
            s_h.append(hn)
            s_conv.append(cn.reshape(n_dec, CONV_W - 1, d))
            w_o = rg_w_o
        else:
            qkvr = _matmul(u, j, [gla_w_q, gla_w_k, gla_w_v, gla_w_r],
                           [_identity, _identity, _identity, _silu], tm=tm, out_dtype=BF16)
            rank = gla_w_g1.shape[2]
            w_g1 = jnp.pad(gla_w_g1[j], ((0, 0), (0, LANES - rank)))
            w_g2 = jnp.pad(gla_w_g2[j], ((0, LANES - rank), (0, 0)))
            g = _gla_gate(u, w_g1, w_g2, gla_b_g[j], tm=tm)
            mix, s_fin, s_new = _gla(qkvr, g, j, state_gla_S, gla_onorm, batch=batch, seq=seq,
                                     heads=heads, dk=dk, dv=dv, **tiles)
            p_s.append(s_fin)
            s_s.append(s_new)
            w_o = gla_w_o
        x = _matmul(mix.reshape(N_TOKEN_TILES * tm, d), j, [w_o], [_identity], res=x, tm=tm)
        g_final = ln_final.reshape(1, d) if i == depth - 1 else None
        tf = 512 if g_final is None else 256
        x, _, ffn_w = _ffn(x, i, ln_ffn2, ffn_w, g_final=g_final, next_weights=ffn_f32[2 * i + 2], tf=tf,
                           **tiles)

    y_prompt = x[0].reshape(batch, seq, d)
    y_sample = x[1].reshape(n_dec, 1, d)
    return (y_prompt, y_sample, jnp.stack(p_h), jnp.stack(p_conv), jnp.stack(p_s),
            jnp.stack(s_h), jnp.stack(s_conv), jnp.stack(s_s))
```

```python
import functools

import jax
import jax.numpy as jnp
from jax import lax
from jax.experimental import pallas as pl
from jax.experimental.pallas import tpu as pltpu

F32 = jnp.float32
BF16 = jnp.bfloat16

RMS_EPS = 1e-6
RG_C = 8.0
CONV_W = 4
GLA_GATE_NORM = 16.0
GLA_CHUNK = 64

V7X_VMEM_LIMIT_BYTES = 63 * 1024 * 1024
SUBLANES = 8
LANES = 128
BF16_SUBLANES = 16
N_TOKEN_TILES = 8


def _cparams(*sem):
    return pltpu.CompilerParams(dimension_semantics=sem, vmem_limit_bytes=V7X_VMEM_LIMIT_BYTES)


def _rms_scale(x):
    return x * lax.rsqrt(jnp.mean(x * x, axis=-1, keepdims=True) + RMS_EPS)


def _silu(x):
    return x * jax.nn.sigmoid(x)


def _gelu_tanh(y):
    return jax.nn.gelu(y, approximate=True)


def _identity(y):
    return y


def _softplus(x):
    return jnp.maximum(x, 0.0) + jnp.log1p(jnp.exp(-jnp.abs(x)))


ROW_CHUNK = BF16_SUBLANES


def _for_row_chunks(n_rows, body):
    n = n_rows // ROW_CHUNK
    assert n * ROW_CHUNK == n_rows
    unroll = next(u for u in (8, 5, 4, 3, 2, 1) if n % u == 0)

    def step(c, carry):
        body(pl.multiple_of(c * ROW_CHUNK, ROW_CHUNK))
        return carry

    lax.fori_loop(0, n, step, 0, unroll=unroll)


def _chunk_rows(base, off):
    assert base % ROW_CHUNK == 0
    return pl.ds(pl.multiple_of(base + off, ROW_CHUNK), ROW_CHUNK)


def _layer_operand(stacked, layer, block, index_map):
    spec = pl.BlockSpec((None,) + tuple(block), lambda *g: (layer,) + tuple(index_map(*g)))
    return stacked, spec


def _layer_vector(stacked, layer):
    n_layers, n = stacked.shape
    return _layer_operand(stacked.reshape(n_layers, 1, n), layer, (1, n), lambda *g: (0, 0))


def _layer_full(stacked, layer):
    block = stacked.shape[1:]
    return _layer_operand(stacked, layer, block, lambda *g: (0,) * len(block))


def _token_tiles(n_prompt, n_sample):
    tp, ts = n_prompt // N_TOKEN_TILES, n_sample // N_TOKEN_TILES
    assert tp * N_TOKEN_TILES == n_prompt and ts * N_TOKEN_TILES == n_sample
    assert tp % BF16_SUBLANES == 0 and ts % BF16_SUBLANES == 0 and tp % ts == 0
    return tp, ts


def _cast_body(w_ref, o_ref):
    o_ref[...] = w_ref[...].astype(o_ref.dtype)


def _to_bf16(w, layer, *, rows=512):
    _, r, c = w.shape
    return pl.pallas_call(
        _cast_body,
        grid=(r // rows,),
        in_specs=[pl.BlockSpec((None, rows, c), lambda i: (layer, i, 0))],
        out_specs=pl.BlockSpec((rows, c), lambda i: (i, 0)),
        out_shape=jax.ShapeDtypeStruct((r, c), BF16),
        compiler_params=_cparams("arbitrary"),
        name="to_bf16",
    )(w)


def _ffn_body(*refs, tp, ts, split_in, split_out, mix_norm, cast_next):
    refs = list(refs)
    x_refs = [refs.pop(0) for _ in range(2 if split_in else 1)]
    g_ref, w1_ref, w3_ref, w2_ref = (refs.pop(0) for _ in range(4))
    g2_ref = refs.pop(0) if (split_out or mix_norm) else None
    next_f32 = [refs.pop(0) for _ in range(3 if cast_next else 0)]
    o_refs = [refs.pop(0) for _ in range(2 if split_out else 1)]
    u_hbm = refs.pop(0) if mix_norm else None
    next_bf16 = [refs.pop(0) for _ in range(3 if cast_next else 0)]
    xn_ref = refs.pop(0)
    acc_ref = refs.pop(0) if split_out else o_refs[0]
    (sem,) = refs
    tm = tp + ts
    i = pl.program_id(0)
    f = pl.program_id(1)

    def segments(blocks):
        if len(blocks) == 2:
            return [(0, tp, blocks[0]), (tp, ts, blocks[1])]
        return [(0, tm, blocks[0])]

    @pl.when(f == 0)
    def _():
        for t0, n_rows, x_ref in segments(x_refs):
            def chunk(off, x_ref=x_ref, t0=t0):
                x = x_ref[_chunk_rows(0, off), :]
                xn_ref[_chunk_rows(t0, off), :] = (_rms_scale(x) * g_ref[...]).astype(BF16)
                acc_ref[_chunk_rows(t0, off), :] = 2.0 * x

            _for_row_chunks(n_rows, chunk)

    xn = xn_ref[...]
    h1 = jnp.dot(xn, w1_ref[...].astype(BF16), preferred_element_type=F32)
    h3 = jnp.dot(xn, w3_ref[...].astype(BF16), preferred_element_type=F32)
    h = (_silu(h1) * h3).astype(BF16)
    acc_ref[...] += jnp.dot(h, w2_ref[...].astype(BF16), preferred_element_type=F32)

    for src_ref, dst_ref in zip(next_f32, next_bf16):
        dst_ref[...] = src_ref[...].astype(BF16)

    @pl.when(f == pl.num_programs(1) - 1)
    def _():
        for t0, n_rows, o_ref in segments(o_refs):
            def chunk(off, o_ref=o_ref, t0=t0):
                y = 0.5 * acc_ref[_chunk_rows(t0, off), :]
                if split_out:
                    y = _rms_scale(y) * g2_ref[...]
                o_ref[_chunk_rows(0, off), :] = y
                if mix_norm:
                    xn_ref[_chunk_rows(t0, off), :] = (_rms_scale(y) * g2_ref[...]).astype(BF16)

            _for_row_chunks(n_rows, chunk)
        if mix_norm:
            cu = pltpu.make_async_copy(xn_ref, u_hbm.at[pl.ds(i * tm, tm)], sem.at[0])
            cu.start()
            cu.wait()


def _ffn(x, layer, g, weights, *, tp, ts, g_mix=None, g_final=None, next_weights=None, tf=512):
    split_in = isinstance(x, tuple)
    split_out = g_final is not None
    mix_norm = g_mix is not None
    cast_next = next_weights is not None
    assert not (split_out and mix_norm)
    xs = list(x) if split_in else [x]
    w1, w3, w2 = weights
    d, f = w1.shape
    tm = tp + ts
    nf = f // tf
    operands = [
        _layer_vector(g, layer),
        (w1, pl.BlockSpec((d, tf), lambda i, j: (0, j))),
        (w3, pl.BlockSpec((d, tf), lambda i, j: (0, j))),
        (w2, pl.BlockSpec((tf, d), lambda i, j: (j, 0))),
    ]
    if mix_norm:
        operands.append(_layer_vector(g_mix, layer))
    if split_out:
        operands.append(_layer_vector(g_final, 0))
    row_tile = lambda rows: pl.BlockSpec((rows, d), lambda i, j: (i, 0))
    x_specs = [row_tile(tp), row_tile(ts)] if split_in else [row_tile(tm)]
    if split_out:
        out_specs = [row_tile(tp), row_tile(ts)]
        out_shape = [jax.ShapeDtypeStruct((tp * N_TOKEN_TILES, d), F32),
                     jax.ShapeDtypeStruct((ts * N_TOKEN_TILES, d), F32)]
    else:
        out_specs = [row_tile(tm)]
        out_shape = [jax.ShapeDtypeStruct((tm * N_TOKEN_TILES, d), F32)]
    if mix_norm:
        out_specs.append(pl.BlockSpec(memory_space=pl.ANY))
        out_shape.append(jax.ShapeDtypeStruct((tm * N_TOKEN_TILES, d), BF16))
    if cast_next:
        (n1, n3, n2), next_layer = next_weights
        db = d // N_TOKEN_TILES
        assert n1.shape[1:] == (d, f) and n2.shape[1:] == (f, d) and db * N_TOKEN_TILES == d
        operands += [
            _layer_operand(n1, next_layer, (db, tf), lambda i, j: (i, j)),
            _layer_operand(n3, next_layer, (db, tf), lambda i, j: (i, j)),
            _layer_operand(n2, next_layer, (tf, db), lambda i, j: (j, i)),
        ]
        out_specs += [pl.BlockSpec((db, tf), lambda i, j: (i, j)), pl.BlockSpec((db, tf), lambda i, j: (i, j)),
                      pl.BlockSpec((tf, db), lambda i, j: (j, i))]
        out_shape += [jax.ShapeDtypeStruct((d, f), BF16), jax.ShapeDtypeStruct((d, f), BF16),
                      jax.ShapeDtypeStruct((f, d), BF16)]
    out = pl.pallas_call(
        functools.partial(_ffn_body, tp=tp, ts=ts, split_in=split_in, split_out=split_out,
                          mix_norm=mix_norm, cast_next=cast_next),
        grid=(N_TOKEN_TILES, nf),
        in_specs=x_specs + [spec for _, spec in operands],
        out_specs=out_specs,
        out_shape=out_shape,
        scratch_shapes=[pltpu.VMEM((tm, d), BF16)] + ([pltpu.VMEM((tm, d), F32)] if split_out else [])
        + [pltpu.SemaphoreType.DMA((1,))],
        compiler_params=_cparams("arbitrary", "arbitrary"),
        name="ffn",
    )(*xs, *[arr for arr, _ in operands])
    out = list(out)
    acts = tuple(out[:2]) if split_out else out[0]
    rest = out[2:] if split_out else out[1:]
    u = rest.pop(0) if mix_norm else None
    return acts, u, (tuple(rest) if cast_next else None)


def _matmul_body(u_ref, *refs, tiles, layer, has_res):
    n_src = 1 + max(src for src, _, _ in tiles)
    w_hbm = refs[:n_src]
    rest = refs[n_src:]
    if has_res:
        r_ref, o_ref, stage_ref, wb_ref, sem = rest
    else:
        o_ref, stage_ref, wb_ref, sem = rest
    tn = stage_ref.shape[1]
    j = pl.program_id(0)
    i = pl.program_id(1)

    def weight_copy(jj):
        src, col0, _ = tiles[jj]
        return pltpu.make_async_copy(w_hbm[src].at[layer, :, pl.ds(col0, tn)], stage_ref, sem.at[0])

    def at_tile(jj, fn):
        pl.when(j == jj)(fn)

    @pl.when(i == 0)
    def _():
        at_tile(0, lambda: weight_copy(0).start())
        for jj in range(len(tiles)):
            at_tile(jj, lambda jj=jj: weight_copy(jj).wait())
        wb_ref[...] = stage_ref[...].astype(BF16)
        for jj in range(len(tiles) - 1):
            at_tile(jj, lambda jj=jj: weight_copy(jj + 1).start())

    for act in dict.fromkeys(act for _, _, act in tiles):
        uses = functools.reduce(jnp.logical_or, [j == jj for jj, t in enumerate(tiles) if t[2] is act])

        @pl.when(uses)
        def _(act=act):
            tm = u_ref.shape[0]
            cuts = [0] + [tm - tm // div // BF16_SUBLANES * BF16_SUBLANES for div in (2, 8)] + [tm]
            for rows in (slice(lo, hi) for lo, hi in zip(cuts[:-1], cuts[1:])):
                y = act(jnp.dot(u_ref[rows, :], wb_ref[...], preferred_element_type=F32))
                if has_res:
                    y = r_ref[rows, :] + y
                o_ref[rows, :] = y.astype(o_ref.dtype)


def _matmul(u, layer, ws, acts, res=None, *, tm, tn=1024, out_dtype=F32):
    m, d = u.shape
    tiles = []
    for src, (w, act) in enumerate(zip(ws, acts)):
        assert w.shape[2] % tn == 0
        tiles += [(src, col0, act) for col0 in range(0, w.shape[2], tn)]
    n = len(tiles) * tn
    io_spec = lambda: pl.BlockSpec((tm, tn), lambda j, i: (i, j))
    has_res = res is not None
    return pl.pallas_call(
        functools.partial(_matmul_body, tiles=tuple(tiles), layer=layer, has_res=has_res),
        grid=(len(tiles), m // tm),
        in_specs=[pl.BlockSpec((tm, d), lambda j, i: (i, 0))]
        + [pl.BlockSpec(memory_space=pl.ANY)] * len(ws) + ([io_spec()] if has_res else []),
        out_specs=io_spec(),
        out_shape=jax.ShapeDtypeStruct((m, n), out_dtype),
        scratch_shapes=[pltpu.VMEM((d, tn), F32), pltpu.VMEM((d, tn), BF16),
                        pltpu.SemaphoreType.DMA((1,))],
        compiler_params=_cparams("arbitrary", "arbitrary"),
        name="matmul",
    )(u, *ws, *([res] if has_res else []))


def _prompt_rows_of_tile(o_ref, nt, per_tile, tb):
    blk = pl.program_id(0) * nt + pl.program_id(1)
    k = blk % per_tile
    tp = per_tile * tb

    @pl.when(k == 0)
    def _():
        o_ref[0, tp:, :] = jnp.zeros((o_ref.shape[1] - tp, o_ref.shape[2]), o_ref.dtype)

    return pl.ds(pl.multiple_of(k * tb, tb), tb)


def _sigmoid(x):
    return 0.5 * jnp.tanh(0.5 * x) + 0.5


def _rglru_coeffs(conv_out, wa_ref, ba_ref, wi_ref, bi_ref, lam_ref, store):
    n_blocks, bw, _ = wa_ref.shape
    for n in range(n_blocks):
        cols = slice(n * bw, (n + 1) * bw)
        xb = conv_out(cols)
        xb16 = xb.astype(BF16)
        wa = wa_ref[n].astype(BF16)
        wi = wi_ref[n].astype(BF16)
        r = _sigmoid(jnp.dot(xb16, wa, preferred_element_type=F32) + ba_ref[:, cols])
        i = _sigmoid(jnp.dot(xb16, wi, preferred_element_type=F32) + bi_ref[:, cols])
        log_a = -RG_C * r * _softplus(-lam_ref[:, cols])
        a = jnp.exp(log_a)
        b = jnp.sqrt(1.0 - a * a) * (i * xb)
        store(n, a, b)


def _rg_prompt_body(xw_ref, gate_ref, cw_ref, cb_ref, wa_ref, ba_ref, wi_ref, bi_ref, lam_ref,
                    o_ref, hl_ref, cl_ref, ext_ref, a_ref, b_ref, h_ref, *, nt, per_tile):
    t = pl.program_id(1)
    tb = xw_ref.shape[1]
    bw = wa_ref.shape[1]
    pad = SUBLANES

    @pl.when(t == 0)
    def _():
        ext_ref[0:pad, :] = jnp.zeros((pad, ext_ref.shape[1]), F32)
        h_ref[...] = jnp.zeros_like(h_ref)

    ext_ref[pad:pad + tb, :] = xw_ref[0]

    def conv_out(cols):
        ext = ext_ref[:, cols]
        y = ext * cw_ref[0:1, cols]
        for j in range(1, CONV_W):
            y = ext * cw_ref[j:j + 1, cols] + pltpu.roll(y, 1, axis=0)
        return cb_ref[:, cols] + y[pad:, :]

    def store(n, a, b):
        a_ref[:, n * bw:(n + 1) * bw] = a
        b_ref[:, n * bw:(n + 1) * bw] = b

    _rglru_coeffs(conv_out, wa_ref, ba_ref, wi_ref, bi_ref, lam_ref, store)

    def step(r, h):
        h = a_ref[pl.ds(r, 1), :] * h + b_ref[pl.ds(r, 1), :]
        b_ref[pl.ds(r, 1), :] = h
        return h

    h = lax.fori_loop(0, tb, step, h_ref[0:1, :], unroll=8)
    h_ref[0:1, :] = h
    out_rows = _prompt_rows_of_tile(o_ref, nt, per_tile, tb)
    o_ref[0, out_rows, :] = (b_ref[...] * gate_ref[0]).astype(o_ref.dtype)
    hl_ref[0] = h
    cl_ref[0] = ext_ref[pad + tb - (CONV_W - 1):pad + tb, :]
    ext_ref[0:pad, :] = ext_ref[tb:tb + pad, :]


def _rg_params(layer, conv_w, conv_b, w_a, b_a, w_i, b_i, lam):
    operands = [_layer_full(conv_w, layer), _layer_vector(conv_b, layer), _layer_full(w_a, layer),
                _layer_vector(b_a, layer), _layer_full(w_i, layer), _layer_vector(b_i, layer),
                _layer_vector(lam, layer)]
    return [arr for arr, _ in operands], [spec for _, spec in operands]


def _rg_prompt(proj, params, *, batch, seq, tp, ts, tb=256):
    p_args, p_specs = params
    m, d2 = proj.shape
    d = d2 // 2
    tm = tp + ts
    nt = seq // tb
    per_tile = tp // tb
    proj3 = proj.reshape(N_TOKEN_TILES, tm, d2)

    def tok(col):
        def index(b, t):
            blk = b * nt + t
            return (blk // per_tile, blk % per_tile, col)
        return index

    mix, hl, cl = pl.pallas_call(
        functools.partial(_rg_prompt_body, nt=nt, per_tile=per_tile),
        grid=(batch, nt),
        in_specs=[pl.BlockSpec((1, tb, d), tok(1)), pl.BlockSpec((1, tb, d), tok(0))] + p_specs,
        out_specs=[
            pl.BlockSpec((1, tm, d), lambda b, t: ((b * nt + t) // per_tile, 0, 0)),
            pl.BlockSpec((1, 1, d), lambda b, t: (b, 0, 0)),
            pl.BlockSpec((1, CONV_W - 1, d), lambda b, t: (b, 0, 0)),
        ],
        out_shape=[
            jax.ShapeDtypeStruct((N_TOKEN_TILES, tm, d), BF16),
            jax.ShapeDtypeStruct((batch, 1, d), F32),
            jax.ShapeDtypeStruct((batch, CONV_W - 1, d), F32),
        ],
        scratch_shapes=[
            pltpu.VMEM((tb + SUBLANES, d), F32),
            pltpu.VMEM((tb, d), F32),
            pltpu.VMEM((tb, d), F32),
            pltpu.VMEM((SUBLANES, d), F32),
        ],
        compiler_params=_cparams("arbitrary", "arbitrary"),
        name="rg_prompt",
    )(proj3, proj3, *p_args)
    return mix, hl, cl


def _rg_sample_body(mix_ref, xw_ref, gate_ref, conv_ref, h0_ref, cw_ref, cb_ref, wa_ref, ba_ref,
                    wi_ref, bi_ref, lam_ref, o_ref, hn_ref, cn_ref, a_ref, b_ref):
    del mix_ref
    n, d = h0_ref.shape
    bw = wa_ref.shape[1]
    x = xw_ref[...].reshape(n, d)
    xc = cb_ref[...]
    for j in range(CONV_W - 1):
        xc = xc + conv_ref[:, j * d:(j + 1) * d] * cw_ref[j:j + 1, :]
    xc = xc + x * cw_ref[CONV_W - 1:CONV_W, :]

    def store(k, a, b):
        a_ref[:, k * bw:(k + 1) * bw] = a
        b_ref[:, k * bw:(k + 1) * bw] = b

    _rglru_coeffs(lambda cols: xc[:, cols], wa_ref, ba_ref, wi_ref, bi_ref, lam_ref, store)
    h = b_ref[...] + a_ref[...] * h0_ref[...]
    hn_ref[...] = h
    o_ref[...] = (h * gate_ref[...].reshape(n, d)).reshape(o_ref.shape).astype(o_ref.dtype)
    cn_ref[:, 0:(CONV_W - 2) * d] = conv_ref[:, d:(CONV_W - 1) * d]
    cn_ref[:, (CONV_W - 2) * d:] = x


def _rg_sample(mix, proj, layer, conv_state, h0, params, *, tp, ts):
    p_args, p_specs = params
    n_layers, n, d = h0.shape
    tm = tp + ts
    proj3 = proj.reshape(N_TOKEN_TILES, tm, 2 * d)
    s_blk = tp // ts
    full2 = lambda shape: pl.BlockSpec(shape, lambda i: (0, 0))
    kc = (CONV_W - 1) * d
    conv_arg, conv_spec = _layer_full(conv_state.reshape(n_layers, n, kc), layer)
    h0_arg, h0_spec = _layer_full(h0, layer)
    return pl.pallas_call(
        _rg_sample_body,
        grid=(1,),
        in_specs=[
            pl.BlockSpec(memory_space=pl.ANY),
            pl.BlockSpec((N_TOKEN_TILES, ts, d), lambda i: (0, s_blk, 1)),
            pl.BlockSpec((N_TOKEN_TILES, ts, d), lambda i: (0, s_blk, 0)),
            conv_spec, h0_spec,
        ] + p_specs,
        out_specs=[
            pl.BlockSpec((N_TOKEN_TILES, ts, d), lambda i: (0, s_blk, 0)),
            full2((n, d)),
            full2((n, kc)),
        ],
        out_shape=[
            jax.ShapeDtypeStruct(mix.shape, mix.dtype),
            jax.ShapeDtypeStruct((n, d), F32),
            jax.ShapeDtypeStruct((n, kc), F32),
        ],
        scratch_shapes=[pltpu.VMEM((n, d), F32), pltpu.VMEM((n, d), F32)],
        input_output_aliases={0: 0},
        compiler_params=_cparams("arbitrary"),
        name="rg_sample",
    )(mix, proj3, proj3, conv_arg, h0_arg, *p_args)


def _gla_log_decay(u, w1_ref, w2_ref, b_ref):
    t = jnp.dot(u, w1_ref[...].astype(BF16), preferred_element_type=F32)
    logit = jnp.dot(t.astype(BF16), w2_ref[...].astype(BF16), preferred_element_type=F32) + b_ref[...]
    return -_softplus(-logit) * (1.0 / GLA_GATE_NORM)


def _gla_sample_update(q_ref, k_ref, v_ref, r_ref, g, s0_ref, on_ref, s_ref):
    bb, dk = q_ref.shape[1:]
    alpha_t = jnp.exp(g).T
    k_t = k_ref[0].astype(F32).T
    q_t = (q_ref[0].astype(F32) * dk ** -0.5).T
    v = v_ref[0].astype(F32)
    rows = []
    for b in range(bb):
        s_new = alpha_t[:, b:b + 1] * s0_ref[b, 0] + k_t[:, b:b + 1] * v[b:b + 1, :]
        s_ref[b, 0] = s_new
        rows.append(jnp.sum(q_t[:, b:b + 1] * s_new, axis=0, keepdims=True))
    o = jnp.concatenate(rows, axis=0)
    o = _rms_scale(o) * on_ref[...]
    return o * r_ref[0].astype(F32)


def _gla_body(q_ref, k_ref, v_ref, r_ref, u_ref, on_ref, wg1_ref, wg2_ref, bg_ref,
              sq_ref, sk_ref, sv_ref, sr_ref, su_ref, s0_ref, o_ref, s_ref, ss_ref, *,
              heads, nt, per_tile):
    t = pl.program_id(1)
    tb = q_ref.shape[1]
    tp = per_tile * tb
    c = GLA_CHUNK
    dk = q_ref.shape[2] // heads
    dv = v_ref.shape[2] // heads

    @pl.when(t == 0)
    def _():
        s_ref[...] = jnp.zeros_like(s_ref)

    blk = pl.program_id(0) * nt + t
    head_s = blk % heads
    g_s_all = _gla_log_decay(su_ref[0], wg1_ref, wg2_ref, bg_ref)
    g_s = jnp.zeros((g_s_all.shape[0], dk), F32)
    for h in range(heads):
        g_s = jnp.where(head_s == h, g_s_all[:, h * dk:(h + 1) * dk], g_s)
    o_s = _gla_sample_update(sq_ref, sk_ref, sv_ref, sr_ref, g_s, s0_ref, on_ref, ss_ref)
    o_s = o_s.astype(o_ref.dtype)

    row = lax.broadcasted_iota(jnp.int32, (tb, tb), 0)
    col = lax.broadcasted_iota(jnp.int32, (tb, tb), 1)
    same_chunk_causal = ((row >= col) & ((row ^ col) < c)).astype(BF16)
    g_rest = _gla_log_decay(u_ref[0], wg1_ref, wg2_ref, bg_ref)
    bcum = jnp.zeros(g_rest.shape, F32)
    for _ in range(3):
        g_part = g_rest.astype(BF16)
        bcum = bcum + jnp.dot(same_chunk_causal, g_part, preferred_element_type=F32)
        g_rest = g_rest - g_part.astype(F32)
    nc = tb // c
    chunk_rows = [slice(ci * c, (ci + 1) * c) for ci in range(nc)]
    q = q_ref[0].astype(F32) * dk ** -0.5
    k = k_ref[0].astype(F32)
    q_i = q * jnp.exp(bcum)
    k_i = (k * jnp.exp(-bcum)).astype(BF16)
    g_tot = [bcum[(ci + 1) * c - 1:(ci + 1) * c, :] for ci in range(nc)]
    g_before = [jnp.zeros_like(g_tot[0])]
    for ci in range(nc):
        g_before.append(g_before[-1] + g_tot[ci])
    k_e = [k[rows, :] * jnp.exp(g_tot[ci] - bcum[rows, :]) for ci, rows in enumerate(chunk_rows)]
    q_in = jnp.concatenate([(q_i[rows, :] * jnp.exp(g_before[ci])).astype(BF16)
                            for ci, rows in enumerate(chunk_rows)], axis=0)
    k_out = jnp.concatenate([(k_e[ci] * jnp.exp(g_before[nc] - g_before[ci + 1])).astype(BF16)
                             for ci in range(nc)], axis=0)
    decay_t = jnp.exp(jnp.broadcast_to(g_before[nc], (SUBLANES, heads * dk))).T
    q_i = q_i.astype(BF16)

    out_base = (blk % per_tile) * tb
    o_inter = []
    for h in range(heads):
        kcols = slice(h * dk, (h + 1) * dk)
        vcols = slice(h * dv, (h + 1) * dv)
        s = s_ref[0, h]
        o_inter.append(jnp.dot(q_in[:, kcols], s.astype(BF16), preferred_element_type=F32))
        s_ref[0, h] = decay_t[kcols, 0:1] * s + lax.dot_general(
            k_out[:, kcols], v_ref[0, :, vcols], (((0,), (0,)), ((), ())), preferred_element_type=F32)

    for ci, rows in enumerate(chunk_rows):
        out_rows = pl.ds(pl.multiple_of(out_base + ci * c, c), c)
        n_keys = (ci + 1) * c
        keys = jnp.concatenate(
            [(k_e[cj] * jnp.exp(g_before[ci] - g_before[cj + 1])).astype(BF16) for cj in range(ci)]
            + [k_i[rows, :]], axis=0)
        visible = (lax.broadcasted_iota(jnp.int32, (c, n_keys), 1)
                   <= lax.broadcasted_iota(jnp.int32, (c, n_keys), 0) + ci * c)
        for h in range(heads):
            kcols = slice(h * dk, (h + 1) * dk)
            vcols = slice(h * dv, (h + 1) * dv)
            scores = lax.dot_general(q_i[rows, kcols], keys[:, kcols], (((1,), (1,)), ((), ())),
                                     preferred_element_type=F32)
            scores = jnp.where(visible, scores, 0.0).astype(BF16)
            o = o_inter[h][rows, :] + jnp.dot(scores, v_ref[0, 0:n_keys, vcols],
                                              preferred_element_type=F32)
            o = _rms_scale(o) * on_ref[...]
            o_ref[0, out_rows, vcols] = (o * r_ref[0, rows, vcols].astype(F32)).astype(o_ref.dtype)

    for h in range(heads):
        @pl.when(head_s == h)
        def _(h=h):
            o_ref[0, tp:, h * dv:(h + 1) * dv] = o_s


def _gla(qkvr, u, layer, gate_params, s0, onorm, *, batch, seq, heads, dk, dv, tp, ts, tb=256):
    tm = tp + ts
    nt = seq // tb
    per_tile = tp // tb
    s_blk = tp // ts
    hk, hv = heads * dk, heads * dv
    assert hv == 2 * hk
    assert batch * nt == N_TOKEN_TILES * heads and per_tile == heads
    qkvr3 = qkvr.reshape(N_TOKEN_TILES, tm, qkvr.shape[1])
    d = u.shape[1]
    u3 = u.reshape(N_TOKEN_TILES, tm, d)
    resident = lambda a: pl.BlockSpec(a.shape, lambda b, t: (0,) * a.ndim)
    kq, vq, rq = heads, (2 * hk) // dv, (2 * hk) // dv + heads

    def tok(col):
        def index(b, t):
            blk = b * nt + t
            return (blk // per_tile, blk % per_tile, col)
        return index

    def smp(col0):
        def index(b, t):
            blk = b * nt + t
            return (blk // heads, s_blk, col0 + blk % heads)
        return index

    state = lambda b, t: ((b * nt + t) // heads, (b * nt + t) % heads, 0, 0)
    on_arg, on_spec = _layer_vector(onorm, layer)
    return pl.pallas_call(
        functools.partial(_gla_body, heads=heads, nt=nt, per_tile=per_tile),
        grid=(batch, nt),
        in_specs=[
            pl.BlockSpec((1, tb, hk), tok(0)),
            pl.BlockSpec((1, tb, hk), tok(1)),
            pl.BlockSpec((1, tb, hv), tok(1)),
            pl.BlockSpec((1, tb, hv), tok(2)),
            pl.BlockSpec((1, tb, d), tok(0)),
            on_spec,
            *[resident(a) for a in gate_params],
            pl.BlockSpec((1, ts, dk), smp(0)),
            pl.BlockSpec((1, ts, dk), smp(kq)),
            pl.BlockSpec((1, ts, dv), smp(vq)),
            pl.BlockSpec((1, ts, dv), smp(rq)),
            pl.BlockSpec((1, ts, d), lambda b, t: ((b * nt + t) // heads, s_blk, 0)),
            _layer_operand(s0, layer, (ts, 1, dk, dv), state)[1],
        ],
        out_specs=[
            pl.BlockSpec((1, tm, hv), lambda b, t: ((b * nt + t) // per_tile, 0, 0)),
            pl.BlockSpec((1, heads, dk, dv), lambda b, t: (b, 0, 0, 0)),
            pl.BlockSpec((ts, 1, dk, dv), state),
        ],
        out_shape=[
            jax.ShapeDtypeStruct((N_TOKEN_TILES, tm, hv), BF16),
            jax.ShapeDtypeStruct((batch, heads, dk, dv), F32),
            jax.ShapeDtypeStruct(s0.shape[1:], F32),
        ],
        compiler_params=_cparams("arbitrary", "arbitrary"),
        name="gla",
    )(qkvr3, qkvr3, qkvr3, qkvr3, u3, on_arg, *gate_params, qkvr3, qkvr3, qkvr3, qkvr3, u3, s0)


def kernel(x_prompt, x_sample, state_rglru_h, state_rglru_conv, state_gla_S, ln_ffn1, ffn1_w1, ffn1_w3, ffn1_w2, ln_mix, ln_ffn2, ffn2_w1, ffn2_w3, ffn2_w2, rg_w_y, rg_w_x, rg_conv_w, rg_conv_b, rg_w_a, rg_b_a, rg_w_i, rg_b_i, rg_lambda, rg_w_o, gla_w_q, gla_w_k, gla_w_v, gla_w_g1, gla_w_g2, gla_b_g, gla_w_r, gla_onorm, gla_w_o, ln_final):
    batch, seq, d = x_prompt.shape
    n_dec = x_sample.shape[0]
    depth = ln_ffn1.shape[0]
    heads, dk, dv = state_gla_S.shape[2:]
    tp, ts = _token_tiles(batch * seq, n_dec)
    tm = tp + ts
    tiles = dict(tp=tp, ts=ts)

    x = (x_prompt.reshape(batch * seq, d), x_sample.reshape(n_dec, d))

    ffn_f32 = []
    for i in range(depth):
        ffn_f32 += [((ffn1_w1, ffn1_w3, ffn1_w2), i), ((ffn2_w1, ffn2_w3, ffn2_w2), i)]
    ffn_f32.append(None)
    ffn_w = tuple(_to_bf16(w, 0) for w in ffn_f32[0][0])

    p_h, p_conv, p_s, s_h, s_conv, s_s = [], [], [], [], [], []
    for i in range(depth):
        j = i // 2
        x, u, ffn_w = _ffn(x, i, ln_ffn1, ffn_w, g_mix=ln_mix, next_weights=ffn_f32[2 * i + 1], **tiles)
        if i % 2 == 0:
            proj = _matmul(u, j, [rg_w_y, rg_w_x], [_gelu_tanh, _identity], tm=tm)
            params = _rg_params(j, rg_conv_w, rg_conv_b, rg_w_a, rg_b_a, rg_w_i, rg_b_i, rg_lambda)
            mix, hl, cl = _rg_prompt(proj, params, batch=batch, seq=seq, **tiles)
            mix, hn, cn = _rg_sample(mix, proj, j, state_rglru_conv, state_rglru_h, params, **tiles)
            p_h.append(hl.reshape(batch, d))
            p_conv.append(cl)
            s_h.append(hn)
            s_conv.append(cn.reshape(n_dec, CONV_W - 1, d))
            w_o = rg_w_o
        else:
            qkvr = _matmul(u, j, [gla_w_q, gla_w_k, gla_w_v, gla_w_r],
                           [_identity, _identity, _identity, _silu], tm=tm, out_dtype=BF16)
            rank = gla_w_g1.shape[2]
            gate_params = (jnp.pad(gla_w_g1[j], ((0, 0), (0, LANES - rank))).astype(BF16),
                           jnp.pad(gla_w_g2[j], ((0, LANES - rank), (0, 0))).astype(BF16),
                           gla_b_g[j].reshape(1, heads * dk))
            mix, s_fin, s_new = _gla(qkvr, u, j, gate_params, state_gla_S, gla_onorm, batch=batch,
                                     seq=seq, heads=heads, dk=dk, dv=dv, **tiles)
            p_s.append(s_fin)
            s_s.append(s_new)
            w_o = gla_w_o
        x = _matmul(mix.reshape(N_TOKEN_TILES * tm, d), j, [w_o], [_identity], res=x, tm=tm)
        g_final = ln_final.reshape(1, d) if i == depth - 1 else None
        tf = 512 if g_final is None else 256
        x, _, ffn_w = _ffn(x, i, ln_ffn2, ffn_w, g_final=g_final, next_weights=ffn_f32[2 * i + 2], tf=tf,
                           **tiles)

    y_prompt = x[0].reshape(batch, seq, d)
    y_sample = x[1].reshape(n_dec, 1, d)
    return (y_prompt, y_sample, jnp.stack(p_h), jnp.stack(p_conv), jnp.stack(p_s),
            jnp.stack(s_h), jnp.stack(s_conv), jnp.stack(s_s))
```

```python
import functools

import jax
import jax.numpy as jnp
from jax import lax
from jax.experimental import pallas as pl
from jax.experimental.pallas import tpu as pltpu

F32 = jnp.float32
BF16 = jnp.bfloat16

RMS_EPS = 1e-6
RG_C = 8.0
CONV_W = 4
GLA_GATE_NORM = 16.0
GLA_CHUNK = 64

V7X_VMEM_LIMIT_BYTES = 63 * 1024 * 1024
SUBLANES = 8
LANES = 128
BF16_SUBLANES = 16
N_TOKEN_TILES = 8


def _cparams(*sem):
    return pltpu.CompilerParams(dimension_semantics=sem, vmem_limit_bytes=V7X_VMEM_LIMIT_BYTES)


def _rms_scale(x):
    return x * lax.rsqrt(jnp.mean(x * x, axis=-1, keepdims=True) + RMS_EPS)


def _silu(x):
    return x * jax.nn.sigmoid(x)


def _gelu_tanh(y):
    return jax.nn.gelu(y, approximate=True)


def _identity(y):
    return y


def _softplus(x):
    return jnp.maximum(x, 0.0) + jnp.log1p(jnp.exp(-jnp.abs(x)))


ROW_CHUNK = BF16_SUBLANES


def _for_row_chunks(n_rows, body):
    n = n_rows // ROW_CHUNK
    assert n * ROW_CHUNK == n_rows
    unroll = next(u for u in (8, 5, 4, 3, 2, 1) if n % u == 0)

    def step(c, carry):
        body(pl.multiple_of(c * ROW_CHUNK, ROW_CHUNK))
        return carry

    lax.fori_loop(0, n, step, 0, unroll=unroll)


def _chunk_rows(base, off):
    assert base % ROW_CHUNK == 0
    return pl.ds(pl.multiple_of(base + off, ROW_CHUNK), ROW_CHUNK)


def _layer_operand(stacked, layer, block, index_map):
    spec = pl.BlockSpec((None,) + tuple(block), lambda *g: (layer,) + tuple(index_map(*g)))
    return stacked, spec


def _layer_vector(stacked, layer):
    n_layers, n = stacked.shape
    return _layer_operand(stacked.reshape(n_layers, 1, n), layer, (1, n), lambda *g: (0, 0))


def _layer_full(stacked, layer):
    block = stacked.shape[1:]
    return _layer_operand(stacked, layer, block, lambda *g: (0,) * len(block))


def _token_tiles(n_prompt, n_sample):
    tp, ts = n_prompt // N_TOKEN_TILES, n_sample // N_TOKEN_TILES
    assert tp * N_TOKEN_TILES == n_prompt and ts * N_TOKEN_TILES == n_sample
    assert tp % BF16_SUBLANES == 0 and ts % BF16_SUBLANES == 0 and tp % ts == 0
    return tp, ts


def _cast_body(w_ref, o_ref):
    o_ref[...] = w_ref[...].astype(o_ref.dtype)


def _to_bf16(w, layer, *, rows=512):
    _, r, c = w.shape
    return pl.pallas_call(
        _cast_body,
        grid=(r // rows,),
        in_specs=[pl.BlockSpec((None, rows, c), lambda i: (layer, i, 0))],
        out_specs=pl.BlockSpec((rows, c), lambda i: (i, 0)),
        out_shape=jax.ShapeDtypeStruct((r, c), BF16),
        compiler_params=_cparams("arbitrary"),
        name="to_bf16",
    )(w)


def _ffn_body(*refs, tp, ts, split_in, split_out, mix_norm, cast_next):
    refs = list(refs)
    x_refs = [refs.pop(0) for _ in range(2 if split_in else 1)]
    g_ref, w1_ref, w3_ref, w2_ref = (refs.pop(0) for _ in range(4))
    g2_ref = refs.pop(0) if (split_out or mix_norm) else None
    next_f32 = [refs.pop(0) for _ in range(3 if cast_next else 0)]
    o_refs = [refs.pop(0) for _ in range(2 if split_out else 1)]
    u_hbm = refs.pop(0) if mix_norm else None
    next_bf16 = [refs.pop(0) for _ in range(3 if cast_next else 0)]
    xn_ref = refs.pop(0)
    acc_ref = refs.pop(0) if split_out else o_refs[0]
    (sem,) = refs
    tm = tp + ts
    i = pl.program_id(0)
    f = pl.program_id(1)

    def segments(blocks):
        if len(blocks) == 2:
            return [(0, tp, blocks[0]), (tp, ts, blocks[1])]
        return [(0, tm, blocks[0])]

    @pl.when(f == 0)
    def _():
        for t0, n_rows, x_ref in segments(x_refs):
            def chunk(off, x_ref=x_ref, t0=t0):
                x = x_ref[_chunk_rows(0, off), :]
                xn_ref[_chunk_rows(t0, off), :] = (_rms_scale(x) * g_ref[...]).astype(BF16)
                acc_ref[_chunk_rows(t0, off), :] = 2.0 * x

            _for_row_chunks(n_rows, chunk)

    xn = xn_ref[...]
    h1 = jnp.dot(xn, w1_ref[...].astype(BF16), preferred_element_type=F32)
    h3 = jnp.dot(xn, w3_ref[...].astype(BF16), preferred_element_type=F32)
    h = (_silu(h1) * h3).astype(BF16)
    acc_ref[...] += jnp.dot(h, w2_ref[...].astype(BF16), preferred_element_type=F32)

    for src_ref, dst_ref in zip(next_f32, next_bf16):
        dst_ref[...] = src_ref[...].astype(BF16)

    @pl.when(f == pl.num_programs(1) - 1)
    def _():
        for t0, n_rows, o_ref in segments(o_refs):
            def chunk(off, o_ref=o_ref, t0=t0):
                y = 0.5 * acc_ref[_chunk_rows(t0, off), :]
                if split_out:
                    y = _rms_scale(y) * g2_ref[...]
                o_ref[_chunk_rows(0, off), :] = y
                if mix_norm:
                    xn_ref[_chunk_rows(t0, off), :] = (_rms_scale(y) * g2_ref[...]).astype(BF16)

            _for_row_chunks(n_rows, chunk)
        if mix_norm:
            cu = pltpu.make_async_copy(xn_ref, u_hbm.at[pl.ds(i * tm, tm)], sem.at[0])
            cu.start()
            cu.wait()


def _ffn(x, layer, g, weights, *, tp, ts, g_mix=None, g_final=None, next_weights=None, tf=512):
    split_in = isinstance(x, tuple)
    split_out = g_final is not None
    mix_norm = g_mix is not None
    cast_next = next_weights is not None
    assert not (split_out and mix_norm)
    xs = list(x) if split_in else [x]
    w1, w3, w2 = weights
    d, f = w1.shape
    tm = tp + ts
    nf = f // tf
    operands = [
        _layer_vector(g, layer),
        (w1, pl.BlockSpec((d, tf), lambda i, j: (0, j))),
        (w3, pl.BlockSpec((d, tf), lambda i, j: (0, j))),
        (w2, pl.BlockSpec((tf, d), lambda i, j: (j, 0))),
    ]
    if mix_norm:
        operands.append(_layer_vector(g_mix, layer))
    if split_out:
        operands.append(_layer_vector(g_final, 0))
    row_tile = lambda rows: pl.BlockSpec((rows, d), lambda i, j: (i, 0))
    x_specs = [row_tile(tp), row_tile(ts)] if split_in else [row_tile(tm)]
    if split_out:
        out_specs = [row_tile(tp), row_tile(ts)]
        out_shape = [jax.ShapeDtypeStruct((tp * N_TOKEN_TILES, d), F32),
                     jax.ShapeDtypeStruct((ts * N_TOKEN_TILES, d), F32)]
    else:
        out_specs = [row_tile(tm)]
        out_shape = [jax.ShapeDtypeStruct((tm * N_TOKEN_TILES, d), F32)]
    if mix_norm:
        out_specs.append(pl.BlockSpec(memory_space=pl.ANY))
        out_shape.append(jax.ShapeDtypeStruct((tm * N_TOKEN_TILES, d), BF16))
    if cast_next:
        (n1, n3, n2), next_layer = next_weights
        db = d // N_TOKEN_TILES
        assert n1.shape[1:] == (d, f) and n2.shape[1:] == (f, d) and db * N_TOKEN_TILES == d
        operands += [
            _layer_operand(n1, next_layer, (db, tf), lambda i, j: (i, j)),
            _layer_operand(n3, next_layer, (db, tf), lambda i, j: (i, j)),
            _layer_operand(n2, next_layer, (tf, db), lambda i, j: (j, i)),
        ]
        out_specs += [pl.BlockSpec((db, tf), lambda i, j: (i, j)), pl.BlockSpec((db, tf), lambda i, j: (i, j)),
                      pl.BlockSpec((tf, db), lambda i, j: (j, i))]
        out_shape += [jax.ShapeDtypeStruct((d, f), BF16), jax.ShapeDtypeStruct((d, f), BF16),
                      jax.ShapeDtypeStruct((f, d), BF16)]
    out = pl.pallas_call(
        functools.partial(_ffn_body, tp=tp, ts=ts, split_in=split_in, split_out=split_out,
                          mix_norm=mix_norm, cast_next=cast_next),
        grid=(N_TOKEN_TILES, nf),
        in_specs=x_specs + [spec for _, spec in operands],
        out_specs=out_specs,
        out_shape=out_shape,
        scratch_shapes=[pltpu.VMEM((tm, d), BF16)] + ([pltpu.VMEM((tm, d), F32)] if split_out else [])
        + [pltpu.SemaphoreType.DMA((1,))],
        compiler_params=_cparams("arbitrary", "arbitrary"),
        name="ffn",
    )(*xs, *[arr for arr, _ in operands])
    out = list(out)
    acts = tuple(out[:2]) if split_out else out[0]
    rest = out[2:] if split_out else out[1:]
    u = rest.pop(0) if mix_norm else None
    return acts, u, (tuple(rest) if cast_next else None)


def _matmul_body(u_ref, *refs, tiles, layer, has_res):
    n_src = 1 + max(src for src, _, _ in tiles)
    w_hbm = refs[:n_src]
    rest = refs[n_src:]
    if has_res:
        r_ref, o_ref, stage_ref, wb_ref, sem = rest
    else:
        o_ref, stage_ref, wb_ref, sem = rest
    tn = stage_ref.shape[1]
    j = pl.program_id(0)
    i = pl.program_id(1)

    def weight_copy(jj):
        src, col0, _ = tiles[jj]
        return pltpu.make_async_copy(w_hbm[src].at[layer, :, pl.ds(col0, tn)], stage_ref, sem.at[0])

    def at_tile(jj, fn):
        pl.when(j == jj)(fn)

    @pl.when(i == 0)
    def _():
        at_tile(0, lambda: weight_copy(0).start())
        for jj in range(len(tiles)):
            at_tile(jj, lambda jj=jj: weight_copy(jj).wait())
        wb_ref[...] = stage_ref[...].astype(BF16)
        for jj in range(len(tiles) - 1):
            at_tile(jj, lambda jj=jj: weight_copy(jj + 1).start())

    for act in dict.fromkeys(act for _, _, act in tiles):
        uses = functools.reduce(jnp.logical_or, [j == jj for jj, t in enumerate(tiles) if t[2] is act])

        @pl.when(uses)
        def _(act=act):
            tm = u_ref.shape[0]
            cuts = [0] + [tm - tm // div // BF16_SUBLANES * BF16_SUBLANES for div in (2, 8)] + [tm]
            for rows in (slice(lo, hi) for lo, hi in zip(cuts[:-1], cuts[1:])):
                y = act(jnp.dot(u_ref[rows, :], wb_ref[...], preferred_element_type=F32))
                if has_res:
                    y = r_ref[rows, :] + y
                o_ref[rows, :] = y.astype(o_ref.dtype)


def _matmul(u, layer, ws, acts, res=None, *, tm, tn=1024, out_dtype=F32):
    m, d = u.shape
    tiles = []
    for src, (w, act) in enumerate(zip(ws, acts)):
        assert w.shape[2] % tn == 0
        tiles += [(src, col0, act) for col0 in range(0, w.shape[2], tn)]
    n = len(tiles) * tn
    io_spec = lambda: pl.BlockSpec((tm, tn), lambda j, i: (i, j))
    has_res = res is not None
    return pl.pallas_call(
        functools.partial(_matmul_body, tiles=tuple(tiles), layer=layer, has_res=has_res),
        grid=(len(tiles), m // tm),
        in_specs=[pl.BlockSpec((tm, d), lambda j, i: (i, 0))]
        + [pl.BlockSpec(memory_space=pl.ANY)] * len(ws) + ([io_spec()] if has_res else []),
        out_specs=io_spec(),
        out_shape=jax.ShapeDtypeStruct((m, n), out_dtype),
        scratch_shapes=[pltpu.VMEM((d, tn), F32), pltpu.VMEM((d, tn), BF16),
                        pltpu.SemaphoreType.DMA((1,))],
        compiler_params=_cparams("arbitrary", "arbitrary"),
        name="matmul",
    )(u, *ws, *([res] if has_res else []))


def _prompt_rows_of_tile(o_ref, nt, per_tile, tb):
    blk = pl.program_id(0) * nt + pl.program_id(1)
    k = blk % per_tile
    tp = per_tile * tb

    @pl.when(k == 0)
    def _():
        o_ref[0, tp:, :] = jnp.zeros((o_ref.shape[1] - tp, o_ref.shape[2]), o_ref.dtype)

    return pl.ds(pl.multiple_of(k * tb, tb), tb)


def _sigmoid(x):
    return 0.5 * jnp.tanh(0.5 * x) + 0.5


def _rglru_coeffs(conv_out, wa_ref, ba_ref, wi_ref, bi_ref, lam_ref, store):
    n_blocks, bw, _ = wa_ref.shape
    for n in range(n_blocks):
        cols = slice(n * bw, (n + 1) * bw)
        xb = conv_out(cols)
        xb16 = xb.astype(BF16)
        wa = wa_ref[n].astype(BF16)
        wi = wi_ref[n].astype(BF16)
        r = _sigmoid(jnp.dot(xb16, wa, preferred_element_type=F32) + ba_ref[:, cols])
        i = _sigmoid(jnp.dot(xb16, wi, preferred_element_type=F32) + bi_ref[:, cols])
        log_a = -RG_C * r * _softplus(-lam_ref[:, cols])
        a = jnp.exp(log_a)
        b = jnp.sqrt(1.0 - a * a) * (i * xb)
        store(n, a, b)


def _rg_prompt_body(xw_ref, gate_ref, cw_ref, cb_ref, wa_ref, ba_ref, wi_ref, bi_ref, lam_ref,
                    o_ref, hl_ref, cl_ref, ext_ref, a_ref, b_ref, h_ref, *, nt, per_tile):
    t = pl.program_id(1)
    tb = xw_ref.shape[1]
    bw = wa_ref.shape[1]
    pad = SUBLANES

    @pl.when(t == 0)
    def _():
        ext_ref[0:pad, :] = jnp.zeros((pad, ext_ref.shape[1]), F32)
        h_ref[...] = jnp.zeros_like(h_ref)

    ext_ref[pad:pad + tb, :] = xw_ref[0]

    def conv_out(cols):
        ext = ext_ref[:, cols]
        y = ext * cw_ref[0:1, cols]
        for j in range(1, CONV_W):
            y = ext * cw_ref[j:j + 1, cols] + pltpu.roll(y, 1, axis=0)
        return cb_ref[:, cols] + y[pad:, :]

    seg = tb // SUBLANES
    pitch = a_ref.shape[1] // SUBLANES
    n_slabs = a_ref.shape[0]

    def store(n, a, b):
        for half in range(bw // LANES):
            lanes = slice(half * LANES, (half + 1) * LANES)
            for s in range(SUBLANES):
                dst = slice(s * pitch, s * pitch + seg)
                a_ref[n * (bw // LANES) + half, dst, :] = a[s * seg:(s + 1) * seg, lanes]
                b_ref[n * (bw // LANES) + half, dst, :] = b[s * seg:(s + 1) * seg, lanes]

    _rglru_coeffs(conv_out, wa_ref, ba_ref, wi_ref, bi_ref, lam_ref, store)

    def step(v, carry):
        hs, ps = carry
        rows = pl.ds(v, SUBLANES, stride=pitch)
        new_h, new_p = [], []
        for slab in range(n_slabs):
            a = a_ref[slab, rows, :]
            h = a * hs[slab] + b_ref[slab, rows, :]
            p = a * ps[slab]
            b_ref[slab, rows, :] = h
            a_ref[slab, rows, :] = p
            new_h.append(h)
            new_p.append(p)
        return tuple(new_h), tuple(new_p)

    zeros = tuple(jnp.zeros((SUBLANES, LANES), F32) for _ in range(n_slabs))
    ones = tuple(jnp.ones((SUBLANES, LANES), F32) for _ in range(n_slabs))
    lax.fori_loop(0, seg, step, (zeros, ones))

    out_rows = _prompt_rows_of_tile(o_ref, nt, per_tile, tb)
    for slab in range(n_slabs):
        lanes = slice(slab * LANES, (slab + 1) * LANES)
        h_in = h_ref[0:1, lanes]
        for s in range(SUBLANES):
            src = slice(s * pitch, s * pitch + seg)
            h = b_ref[slab, src, :] + a_ref[slab, src, :] * h_in
            dst = pl.ds(pl.multiple_of(out_rows.start + s * seg, seg), seg)
            o_ref[0, dst, lanes] = (h * gate_ref[0, s * seg:(s + 1) * seg, lanes]).astype(o_ref.dtype)
            h_in = h[seg - 1:seg, :]
        h_ref[0:1, lanes] = h_in
        hl_ref[0, :, lanes] = h_in
    cl_ref[0] = ext_ref[pad + tb - (CONV_W - 1):pad + tb, :]
    ext_ref[0:pad, :] = ext_ref[tb:tb + pad, :]


def _segment_pitch(seg):
    pitch = -(-seg // SUBLANES) * SUBLANES
    while (pitch // SUBLANES) % 2 == 0:
        pitch += SUBLANES
    return pitch


def _rg_params(layer, conv_w, conv_b, w_a, b_a, w_i, b_i, lam):
    operands = [_layer_full(conv_w, layer), _layer_vector(conv_b, layer), _layer_full(w_a, layer),
                _layer_vector(b_a, layer), _layer_full(w_i, layer), _layer_vector(b_i, layer),
                _layer_vector(lam, layer)]
    return [arr for arr, _ in operands], [spec for _, spec in operands]


def _rg_prompt(proj, params, *, batch, seq, tp, ts, tb=256):
    p_args, p_specs = params
    m, d2 = proj.shape
    d = d2 // 2
    tm = tp + ts
    nt = seq // tb
    per_tile = tp // tb
    proj3 = proj.reshape(N_TOKEN_TILES, tm, d2)

    def tok(col):
        def index(b, t):
            blk = b * nt + t
            return (blk // per_tile, blk % per_tile, col)
        return index

    mix, hl, cl = pl.pallas_call(
        functools.partial(_rg_prompt_body, nt=nt, per_tile=per_tile),
        grid=(batch, nt),
        in_specs=[pl.BlockSpec((1, tb, d), tok(1)), pl.BlockSpec((1, tb, d), tok(0))] + p_specs,
        out_specs=[
            pl.BlockSpec((1, tm, d), lambda b, t: ((b * nt + t) // per_tile, 0, 0)),
            pl.BlockSpec((1, 1, d), lambda b, t: (b, 0, 0)),
            pl.BlockSpec((1, CONV_W - 1, d), lambda b, t: (b, 0, 0)),
        ],
        out_shape=[
            jax.ShapeDtypeStruct((N_TOKEN_TILES, tm, d), BF16),
            jax.ShapeDtypeStruct((batch, 1, d), F32),
            jax.ShapeDtypeStruct((batch, CONV_W - 1, d), F32),
        ],
        scratch_shapes=[
            pltpu.VMEM((tb + SUBLANES, d), F32),
            pltpu.VMEM((d // LANES, SUBLANES * _segment_pitch(tb // SUBLANES), LANES), F32),
            pltpu.VMEM((d // LANES, SUBLANES * _segment_pitch(tb // SUBLANES), LANES), F32),
            pltpu.VMEM((SUBLANES, d), F32),
        ],
        compiler_params=_cparams("arbitrary", "arbitrary"),
        name="rg_prompt",
    )(proj3, proj3, *p_args)
    return mix, hl, cl


def _rg_sample_body(mix_ref, xw_ref, gate_ref, conv_ref, h0_ref, cw_ref, cb_ref, wa_ref, ba_ref,
                    wi_ref, bi_ref, lam_ref, o_ref, hn_ref, cn_ref, a_ref, b_ref):
    del mix_ref
    n, d = h0_ref.shape
    bw = wa_ref.shape[1]
    x = xw_ref[...].reshape(n, d)
    xc = cb_ref[...]
    for j in range(CONV_W - 1):
        xc = xc + conv_ref[:, j * d:(j + 1) * d] * cw_ref[j:j + 1, :]
    xc = xc + x * cw_ref[CONV_W - 1:CONV_W, :]

    def store(k, a, b):
        a_ref[:, k * bw:(k + 1) * bw] = a
        b_ref[:, k * bw:(k + 1) * bw] = b

    _rglru_coeffs(lambda cols: xc[:, cols], wa_ref, ba_ref, wi_ref, bi_ref, lam_ref, store)
    h = b_ref[...] + a_ref[...] * h0_ref[...]
    hn_ref[...] = h
    o_ref[...] = (h * gate_ref[...].reshape(n, d)).reshape(o_ref.shape).astype(o_ref.dtype)
    cn_ref[:, 0:(CONV_W - 2) * d] = conv_ref[:, d:(CONV_W - 1) * d]
    cn_ref[:, (CONV_W - 2) * d:] = x


def _rg_sample(mix, proj, layer, conv_state, h0, params, *, tp, ts):
    p_args, p_specs = params
    n_layers, n, d = h0.shape
    tm = tp + ts
    proj3 = proj.reshape(N_TOKEN_TILES, tm, 2 * d)
    s_blk = tp // ts
    full2 = lambda shape: pl.BlockSpec(shape, lambda i: (0, 0))
    kc = (CONV_W - 1) * d
    conv_arg, conv_spec = _layer_full(conv_state.reshape(n_layers, n, kc), layer)
    h0_arg, h0_spec = _layer_full(h0, layer)
    return pl.pallas_call(
        _rg_sample_body,
        grid=(1,),
        in_specs=[
            pl.BlockSpec(memory_space=pl.ANY),
            pl.BlockSpec((N_TOKEN_TILES, ts, d), lambda i: (0, s_blk, 1)),
            pl.BlockSpec((N_TOKEN_TILES, ts, d), lambda i: (0, s_blk, 0)),
            conv_spec, h0_spec,
        ] + p_specs,
        out_specs=[
            pl.BlockSpec((N_TOKEN_TILES, ts, d), lambda i: (0, s_blk, 0)),
            full2((n, d)),
            full2((n, kc)),
        ],
        out_shape=[
            jax.ShapeDtypeStruct(mix.shape, mix.dtype),
            jax.ShapeDtypeStruct((n, d), F32),
            jax.ShapeDtypeStruct((n, kc), F32),
        ],
        scratch_shapes=[pltpu.VMEM((n, d), F32), pltpu.VMEM((n, d), F32)],
        input_output_aliases={0: 0},
        compiler_params=_cparams("arbitrary"),
        name="rg_sample",
    )(mix, proj3, proj3, conv_arg, h0_arg, *p_args)


def _gla_log_decay(u, w1_ref, w2_ref, b_ref):
    t = jnp.dot(u, w1_ref[...].astype(BF16), preferred_element_type=F32)
    logit = jnp.dot(t.astype(BF16), w2_ref[...].astype(BF16), preferred_element_type=F32) + b_ref[...]
    return -_softplus(-logit) * (1.0 / GLA_GATE_NORM)


def _gla_sample_update(q_ref, k_ref, v_ref, r_ref, g, s0_ref, on_ref, s_ref):
    bb, dk = q_ref.shape[1:]
    alpha_t = jnp.exp(g).T
    k_t = k_ref[0].astype(F32).T
    q_t = (q_ref[0].astype(F32) * dk ** -0.5).T
    v = v_ref[0].astype(F32)
    rows = []
    for b in range(bb):
        s_new = alpha_t[:, b:b + 1] * s0_ref[b, 0] + k_t[:, b:b + 1] * v[b:b + 1, :]
        s_ref[b, 0] = s_new
        rows.append(jnp.sum(q_t[:, b:b + 1] * s_new, axis=0, keepdims=True))
    o = jnp.concatenate(rows, axis=0)
    o = _rms_scale(o) * on_ref[...]
    return o * r_ref[0].astype(F32)


def _gla_body(q_ref, k_ref, v_ref, r_ref, u_ref, on_ref, wg1_ref, wg2_ref, bg_ref,
              sq_ref, sk_ref, sv_ref, sr_ref, su_ref, s0_ref, o_ref, s_ref, ss_ref, *,
              heads, nt, per_tile):
    t = pl.program_id(1)
    tb = q_ref.shape[1]
    tp = per_tile * tb
    c = GLA_CHUNK
    dk = q_ref.shape[2] // heads
    dv = v_ref.shape[2] // heads

    @pl.when(t == 0)
    def _():
        s_ref[...] = jnp.zeros_like(s_ref)

    blk = pl.program_id(0) * nt + t
    head_s = blk % heads
    g_s_all = _gla_log_decay(su_ref[0], wg1_ref, wg2_ref, bg_ref)
    g_s = jnp.zeros((g_s_all.shape[0], dk), F32)
    for h in range(heads):
        g_s = jnp.where(head_s == h, g_s_all[:, h * dk:(h + 1) * dk], g_s)
    o_s = _gla_sample_update(sq_ref, sk_ref, sv_ref, sr_ref, g_s, s0_ref, on_ref, ss_ref)
    o_s = o_s.astype(o_ref.dtype)

    row = lax.broadcasted_iota(jnp.int32, (tb, tb), 0)
    col = lax.broadcasted_iota(jnp.int32, (tb, tb), 1)
    same_chunk_causal = ((row >= col) & ((row ^ col) < c)).astype(BF16)
    g_rest = _gla_log_decay(u_ref[0], wg1_ref, wg2_ref, bg_ref)
    bcum = jnp.zeros(g_rest.shape, F32)
    for _ in range(3):
        g_part = g_rest.astype(BF16)
        bcum = bcum + jnp.dot(same_chunk_causal, g_part, preferred_element_type=F32)
        g_rest = g_rest - g_part.astype(F32)
    nc = tb // c
    chunk_rows = [slice(ci * c, (ci + 1) * c) for ci in range(nc)]
    q = q_ref[0].astype(F32) * dk ** -0.5
    k = k_ref[0].astype(F32)
    q_i = q * jnp.exp(bcum)
    k_i = (k * jnp.exp(-bcum)).astype(BF16)
    g_tot = [bcum[(ci + 1) * c - 1:(ci + 1) * c, :] for ci in range(nc)]
    g_before = [jnp.zeros_like(g_tot[0])]
    for ci in range(nc):
        g_before.append(g_before[-1] + g_tot[ci])
    k_e = [k[rows, :] * jnp.exp(g_tot[ci] - bcum[rows, :]) for ci, rows in enumerate(chunk_rows)]
    q_in = jnp.concatenate([(q_i[rows, :] * jnp.exp(g_before[ci])).astype(BF16)
                            for ci, rows in enumerate(chunk_rows)], axis=0)
    k_out = jnp.concatenate([(k_e[ci] * jnp.exp(g_before[nc] - g_before[ci + 1])).astype(BF16)
                             for ci in range(nc)], axis=0)
    decay_t = jnp.exp(jnp.broadcast_to(g_before[nc], (SUBLANES, heads * dk))).T
    q_i = q_i.astype(BF16)

    out_base = (blk % per_tile) * tb
    o_inter = []
    for h in range(heads):
        kcols = slice(h * dk, (h + 1) * dk)
        vcols = slice(h * dv, (h + 1) * dv)
        s = s_ref[0, h]
        o_inter.append(jnp.dot(q_in[:, kcols], s.astype(BF16), preferred_element_type=F32))
        s_ref[0, h] = decay_t[kcols, 0:1] * s + lax.dot_general(
            k_out[:, kcols], v_ref[0, :, vcols], (((0,), (0,)), ((), ())), preferred_element_type=F32)

    for ci, rows in enumerate(chunk_rows):
        out_rows = pl.ds(pl.multiple_of(out_base + ci * c, c), c)
        n_keys = (ci + 1) * c
        keys = jnp.concatenate(
            [(k_e[cj] * jnp.exp(g_before[ci] - g_before[cj + 1])).astype(BF16) for cj in range(ci)]
            + [k_i[rows, :]], axis=0)
        visible = (lax.broadcasted_iota(jnp.int32, (c, n_keys), 1)
                   <= lax.broadcasted_iota(jnp.int32, (c, n_keys), 0) + ci * c)
        for h in range(heads):
            kcols = slice(h * dk, (h + 1) * dk)
            vcols = slice(h * dv, (h + 1) * dv)
            scores = lax.dot_general(q_i[rows, kcols], keys[:, kcols], (((1,), (1,)), ((), ())),
                                     preferred_element_type=F32)
            scores = jnp.where(visible, scores, 0.0).astype(BF16)
            o = o_inter[h][rows, :] + jnp.dot(scores, v_ref[0, 0:n_keys, vcols],
                                              preferred_element_type=F32)
            o = _rms_scale(o) * on_ref[...]
            o_ref[0, out_rows, vcols] = (o * r_ref[0, rows, vcols].astype(F32)).astype(o_ref.dtype)

    for h in range(heads):
        @pl.when(head_s == h)
        def _(h=h):
            o_ref[0, tp:, h * dv:(h + 1) * dv] = o_s


def _gla(qkvr, u, layer, gate_params, s0, onorm, *, batch, seq, heads, dk, dv, tp, ts, tb=256):
    tm = tp + ts
    nt = seq // tb
    per_tile = tp // tb
    s_blk = tp // ts
    hk, hv = heads * dk, heads * dv
    assert hv == 2 * hk
    assert batch * nt == N_TOKEN_TILES * heads and per_tile == heads
    qkvr3 = qkvr.reshape(N_TOKEN_TILES, tm, qkvr.shape[1])
    d = u.shape[1]
    u3 = u.reshape(N_TOKEN_TILES, tm, d)
    resident = lambda a: pl.BlockSpec(a.shape, lambda b, t: (0,) * a.ndim)
    kq, vq, rq = heads, (2 * hk) // dv, (2 * hk) // dv + heads

    def tok(col):
        def index(b, t):
            blk = b * nt + t
            return (blk // per_tile, blk % per_tile, col)
        return index

    def smp(col0):
        def index(b, t):
            blk = b * nt + t
            return (blk // heads, s_blk, col0 + blk % heads)
        return index

    state = lambda b, t: ((b * nt + t) // heads, (b * nt + t) % heads, 0, 0)
    on_arg, on_spec = _layer_vector(onorm, layer)
    return pl.pallas_call(
        functools.partial(_gla_body, heads=heads, nt=nt, per_tile=per_tile),
        grid=(batch, nt),
        in_specs=[
            pl.BlockSpec((1, tb, hk), tok(0)),
            pl.BlockSpec((1, tb, hk), tok(1)),
            pl.BlockSpec((1, tb, hv), tok(1)),
            pl.BlockSpec((1, tb, hv), tok(2)),
            pl.BlockSpec((1, tb, d), tok(0)),
            on_spec,
            *[resident(a) for a in gate_params],
            pl.BlockSpec((1, ts, dk), smp(0)),
            pl.BlockSpec((1, ts, dk), smp(kq)),
            pl.BlockSpec((1, ts, dv), smp(vq)),
            pl.BlockSpec((1, ts, dv), smp(rq)),
            pl.BlockSpec((1, ts, d), lambda b, t: ((b * nt + t) // heads, s_blk, 0)),
            _layer_operand(s0, layer, (ts, 1, dk, dv), state)[1],
        ],
        out_specs=[
            pl.BlockSpec((1, tm, hv), lambda b, t: ((b * nt + t) // per_tile, 0, 0)),
            pl.BlockSpec((1, heads, dk, dv), lambda b, t: (b, 0, 0, 0)),
            pl.BlockSpec((ts, 1, dk, dv), state),
        ],
        out_shape=[
            jax.ShapeDtypeStruct((N_TOKEN_TILES, tm, hv), BF16),
            jax.ShapeDtypeStruct((batch, heads, dk, dv), F32),
            jax.ShapeDtypeStruct(s0.shape[1:], F32),
        ],
        compiler_params=_cparams("arbitrary", "arbitrary"),
        name="gla",
    )(qkvr3, qkvr3, qkvr3, qkvr3, u3, on_arg, *gate_params, qkvr3, qkvr3, qkvr3, qkvr3, u3, s0)


def kernel(x_prompt, x_sample, state_rglru_h, state_rglru_conv, state_gla_S, ln_ffn1, ffn1_w1, ffn1_w3, ffn1_w2, ln_mix, ln_ffn2, ffn2_w1, ffn2_w3, ffn2_w2, rg_w_y, rg_w_x, rg_conv_w, rg_conv_b, rg_w_a, rg_b_a, rg_w_i, rg_b_i, rg_lambda, rg_w_o, gla_w_q, gla_w_k, gla_w_v, gla_w_g1, gla_w_g2, gla_b_g, gla_w_r, gla_onorm, gla_w_o, ln_final):
    batch, seq, d = x_prompt.shape
    n_dec = x_sample.shape[0]
    depth = ln_ffn1.shape[0]
    heads, dk, dv = state_gla_S.shape[2:]
    tp, ts = _token_tiles(batch * seq, n_dec)
    tm = tp + ts
    tiles = dict(tp=tp, ts=ts)

    x = (x_prompt.reshape(batch * seq, d), x_sample.reshape(n_dec, d))

    ffn_f32 = []
    for i in range(depth):
        ffn_f32 += [((ffn1_w1, ffn1_w3, ffn1_w2), i), ((ffn2_w1, ffn2_w3, ffn2_w2), i)]
    ffn_f32.append(None)
    ffn_w = tuple(_to_bf16(w, 0) for w in ffn_f32[0][0])

    p_h, p_conv, p_s, s_h, s_conv, s_s = [], [], [], [], [], []
    for i in range(depth):
        j = i // 2
        x, u, ffn_w = _ffn(x, i, ln_ffn1, ffn_w, g_mix=ln_mix, next_weights=ffn_f32[2 * i + 1], **tiles)
        if i % 2 == 0:
            proj = _matmul(u, j, [rg_w_y, rg_w_x], [_gelu_tanh, _identity], tm=tm)
            params = _rg_params(j, rg_conv_w, rg_conv_b, rg_w_a, rg_b_a, rg_w_i, rg_b_i, rg_lambda)
            mix, hl, cl = _rg_prompt(proj, params, batch=batch, seq=seq, **tiles)
            mix, hn, cn = _rg_sample(mix, proj, j, state_rglru_conv, state_rglru_h, params, **tiles)
            p_h.append(hl.reshape(batch, d))
            p_conv.append(cl)
            s_h.append(hn)
            s_conv.append(cn.reshape(n_dec, CONV_W - 1, d))
            w_o = rg_w_o
        else:
            qkvr = _matmul(u, j, [gla_w_q, gla_w_k, gla_w_v, gla_w_r],
                           [_identity, _identity, _identity, _silu], tm=tm, out_dtype=BF16)
            rank = gla_w_g1.shape[2]
            gate_params = (jnp.pad(gla_w_g1[j], ((0, 0), (0, LANES - rank))).astype(BF16),
                           jnp.pad(gla_w_g2[j], ((0, LANES - rank), (0, 0))).astype(BF16),
                           gla_b_g[j].reshape(1, heads * dk))
            mix, s_fin, s_new = _gla(qkvr, u, j, gate_params, state_gla_S, gla_onorm, batch=batch,
                                     seq=seq, heads=heads, dk=dk, dv=dv, **tiles)
            p_s.append(s_fin)
            s_s.append(s_new)
            w_o = gla_w_o
        x = _matmul(mix.reshape(N_TOKEN_TILES * tm, d), j, [w_o], [_identity], res=x, tm=tm)
        g_final = ln_final.reshape(1, d) if i == depth - 1 else None
        tf = 512 if g_final is None else 256
        x, _, ffn_w = _ffn(x, i, ln_ffn2, ffn_w, g_final=g_final, next_weights=ffn_f32[2 * i + 2], tf=tf,
                           **tiles)

    y_prompt = x[0].reshape(batch, seq, d)
    y_sample = x[1].reshape(n_dec, 1, d)
    return (y_prompt, y_sample, jnp.stack(p_h), jnp.stack(p_conv), jnp.stack(p_s),
            jnp.stack(s_h), jnp.stack(s_conv), jnp.stack(s_s))
```

```python
import functools

import jax
import jax.numpy as jnp
from jax import lax
from jax.experimental import pallas as pl
from jax.experimental.pallas import tpu as pltpu

F32 = jnp.float32
BF16 = jnp.bfloat16

RMS_EPS = 1e-6
RG_C = 8.0
CONV_W = 4
GLA_GATE_NORM = 16.0
GLA_CHUNK = 64

V7X_VMEM_LIMIT_BYTES = 63 * 1024 * 1024
SUBLANES = 8
LANES = 128
BF16_SUBLANES = 16
N_TOKEN_TILES = 8


def _cparams(*sem):
    return pltpu.CompilerParams(dimension_semantics=sem, vmem_limit_bytes=V7X_VMEM_LIMIT_BYTES)


def _rms_scale(x):
    return x * lax.rsqrt(jnp.mean(x * x, axis=-1, keepdims=True) + RMS_EPS)


def _silu(x):
    return x * jax.nn.sigmoid(x)


def _gelu_tanh(y):
    return jax.nn.gelu(y, approximate=True)


def _identity(y):
    return y


def _softplus(x):
    return jnp.maximum(x, 0.0) + jnp.log1p(jnp.exp(-jnp.abs(x)))


OUT_PROJ_ROW_TILE = 640
FFN_PIECE_ROWS = (256, 256, 256)


def _layer_operand(stacked, layer, block, index_map):
    spec = pl.BlockSpec((None,) + tuple(block), lambda *g: (layer,) + tuple(index_map(*g)))
    return stacked, spec


def _layer_vector(stacked, layer):
    n_layers, n = stacked.shape
    return _layer_operand(stacked.reshape(n_layers, 1, n), layer, (1, n), lambda *g: (0, 0))


def _layer_full(stacked, layer):
    block = stacked.shape[1:]
    return _layer_operand(stacked, layer, block, lambda *g: (0,) * len(block))


def _token_tiles(n_prompt, n_sample):
    tp, ts = n_prompt // N_TOKEN_TILES, n_sample // N_TOKEN_TILES
    assert tp * N_TOKEN_TILES == n_prompt and ts * N_TOKEN_TILES == n_sample
    assert tp % BF16_SUBLANES == 0 and ts % BF16_SUBLANES == 0 and tp % ts == 0
    return tp, ts


def _cast_body(w_ref, o_ref):
    o_ref[...] = w_ref[...].astype(o_ref.dtype)


def _to_bf16(w, layer, *, rows=512):
    _, r, c = w.shape
    return pl.pallas_call(
        _cast_body,
        grid=(r // rows,),
        in_specs=[pl.BlockSpec((None, rows, c), lambda i: (layer, i, 0))],
        out_specs=pl.BlockSpec((rows, c), lambda i: (i, 0)),
        out_shape=jax.ShapeDtypeStruct((r, c), BF16),
        compiler_params=_cparams("arbitrary"),
        name="to_bf16",
    )(w)


def _ffn_body(*refs, tp, ts, split_in, split_out, mix_norm, cast_next):
    refs = list(refs)
    x_refs = [refs.pop(0) for _ in range(2 if split_in else 1)]
    g_ref, w1_ref, w3_ref, w2_ref = (refs.pop(0) for _ in range(4))
    g2_ref = refs.pop(0) if (split_out or mix_norm) else None
    next_f32 = [refs.pop(0) for _ in range(3 if cast_next else 0)]
    o_refs = [refs.pop(0) for _ in range(2 if split_out else 1)]
    u_hbm = refs.pop(0) if mix_norm else None
    next_bf16 = [refs.pop(0) for _ in range(3 if cast_next else 0)]
    xn_ref = refs.pop(0)
    acc_ref = refs.pop(0) if split_out else o_refs[0]
    (sem,) = refs
    tm = tp + ts
    i = pl.program_id(0)
    f = pl.program_id(1)

    n_f = pl.num_programs(1)

    def segments(blocks):
        if len(blocks) == 2:
            return [(0, tp, blocks[0]), (tp, ts, blocks[1])]
        return [(0, tm, blocks[0])]

    def parts(blocks, r0, r1):
        out = []
        for t0, n_rows, ref in segments(blocks):
            lo, hi = max(r0, t0), min(r1, t0 + n_rows)
            if lo < hi:
                out.append((ref, slice(lo - t0, hi - t0)))
        return out

    def read_rows(blocks, r0, r1):
        vals = [ref[rows, :] for ref, rows in parts(blocks, r0, r1)]
        return vals[0] if len(vals) == 1 else jnp.concatenate(vals, axis=0)

    def write_rows(blocks, r0, r1, value):
        off = 0
        for ref, rows in parts(blocks, r0, r1):
            n = rows.stop - rows.start
            ref[rows, :] = value[off:off + n]
            off += n

    def ffn_product(xn):
        h1 = jnp.dot(xn, w1_ref[...].astype(BF16), preferred_element_type=F32)
        h3 = jnp.dot(xn, w3_ref[...].astype(BF16), preferred_element_type=F32)
        h = (_silu(h1) * h3).astype(BF16)
        return jnp.dot(h, w2_ref[...].astype(BF16), preferred_element_type=F32)

    cuts = [0] + [sum(FFN_PIECE_ROWS[:k + 1]) for k in range(len(FFN_PIECE_ROWS))] + [tm]
    assert cuts[-2] < tm and all(c % BF16_SUBLANES == 0 for c in cuts)
    pieces = list(zip(cuts[:-1], cuts[1:]))
    pieces_rev = [(tm - r1, tm - r0) for r0, r1 in pieces][::-1]

    def cast_next_block():
        for src_ref, dst_ref in zip(next_f32, next_bf16):
            dst_ref[...] = src_ref[...].astype(BF16)

    @pl.when(f == 0)
    def _():
        for r0, r1 in pieces:
            x = read_rows(x_refs, r0, r1)
            xn = (_rms_scale(x) * g_ref[...]).astype(BF16)
            xn_ref[r0:r1, :] = xn
            acc_ref[r0:r1, :] = 2.0 * x + ffn_product(xn)
        cast_next_block()

    @pl.when((f > 0) & (f < n_f - 1))
    def _():
        acc_ref[...] += ffn_product(xn_ref[...])
        cast_next_block()

    @pl.when(f == n_f - 1)
    def _():
        cast_next_block()
        for r0, r1 in pieces_rev:
            y = 0.5 * (acc_ref[r0:r1, :] + ffn_product(xn_ref[r0:r1, :]))
            if split_out:
                y = _rms_scale(y) * g2_ref[...]
            write_rows(o_refs, r0, r1, y)
            if mix_norm:
                xn_ref[r0:r1, :] = (_rms_scale(y) * g2_ref[...]).astype(BF16)
        if mix_norm:
            cu = pltpu.make_async_copy(xn_ref, u_hbm.at[pl.ds(i * tm, tm)], sem.at[0])
            cu.start()
            cu.wait()


def _ffn(x, layer, g, weights, *, tp, ts, g_mix=None, g_final=None, next_weights=None, tf=512):
    split_in = isinstance(x, tuple)
    split_out = g_final is not None
    mix_norm = g_mix is not None
    cast_next = next_weights is not None
    assert not (split_out and mix_norm)
    xs = list(x) if split_in else [x]
    w1, w3, w2 = weights
    d, f = w1.shape
    tm = tp + ts
    nf = f // tf
    operands = [
        _layer_vector(g, layer),
        (w1, pl.BlockSpec((d, tf), lambda i, j: (0, j))),
        (w3, pl.BlockSpec((d, tf), lambda i, j: (0, j))),
        (w2, pl.BlockSpec((tf, d), lambda i, j: (j, 0))),
    ]
    if mix_norm:
        operands.append(_layer_vector(g_mix, layer))
    if split_out:
        operands.append(_layer_vector(g_final, 0))
    row_tile = lambda rows: pl.BlockSpec((rows, d), lambda i, j: (i, 0))
    x_specs = [row_tile(tp), row_tile(ts)] if split_in else [row_tile(tm)]
    if split_out:
        out_specs = [row_tile(tp), row_tile(ts)]
        out_shape = [jax.ShapeDtypeStruct((tp * N_TOKEN_TILES, d), F32),
                     jax.ShapeDtypeStruct((ts * N_TOKEN_TILES, d), F32)]
    else:
        out_specs = [row_tile(tm)]
        out_shape = [jax.ShapeDtypeStruct((tm * N_TOKEN_TILES, d), F32)]
    if mix_norm:
        out_specs.append(pl.BlockSpec(memory_space=pl.ANY))
        out_shape.append(jax.ShapeDtypeStruct((tm * N_TOKEN_TILES, d), BF16))
    if cast_next:
        (n1, n3, n2), next_layer = next_weights
        db = d // N_TOKEN_TILES
        assert n1.shape[1:] == (d, f) and n2.shape[1:] == (f, d) and db * N_TOKEN_TILES == d
        operands += [
            _layer_operand(n1, next_layer, (db, tf), lambda i, j: (i, j)),
            _layer_operand(n3, next_layer, (db, tf), lambda i, j: (i, j)),
            _layer_operand(n2, next_layer, (tf, db), lambda i, j: (j, i)),
        ]
        out_specs += [pl.BlockSpec((db, tf), lambda i, j: (i, j)), pl.BlockSpec((db, tf), lambda i, j: (i, j)),
                      pl.BlockSpec((tf, db), lambda i, j: (j, i))]
        out_shape += [jax.ShapeDtypeStruct((d, f), BF16), jax.ShapeDtypeStruct((d, f), BF16),
                      jax.ShapeDtypeStruct((f, d), BF16)]
    out = pl.pallas_call(
        functools.partial(_ffn_body, tp=tp, ts=ts, split_in=split_in, split_out=split_out,
                          mix_norm=mix_norm, cast_next=cast_next),
        grid=(N_TOKEN_TILES, nf),
        in_specs=x_specs + [spec for _, spec in operands],
        out_specs=out_specs,
        out_shape=out_shape,
        scratch_shapes=[pltpu.VMEM((tm, d), BF16)] + ([pltpu.VMEM((tm, d), F32)] if split_out else [])
        + [pltpu.SemaphoreType.DMA((1,))],
        compiler_params=_cparams("arbitrary", "arbitrary"),
        name="ffn",
    )(*xs, *[arr for arr, _ in operands])
    out = list(out)
    acts = tuple(out[:2]) if split_out else out[0]
    rest = out[2:] if split_out else out[1:]
    u = rest.pop(0) if mix_norm else None
    return acts, u, (tuple(rest) if cast_next else None)


def _matmul_body(u_ref, *refs, tiles, layer, has_res):
    n_src = 1 + max(src for src, _, _ in tiles)
    w_hbm = refs[:n_src]
    rest = refs[n_src:]
    if has_res:
        r_ref, o_ref, stage_ref, wb_ref, sem = rest
    else:
        o_ref, stage_ref, wb_ref, sem = rest
    tn = stage_ref.shape[1]
    j = pl.program_id(0)
    i = pl.program_id(1)

    def weight_copy(jj):
        src, col0, _ = tiles[jj]
        return pltpu.make_async_copy(w_hbm[src].at[layer, :, pl.ds(col0, tn)], stage_ref, sem.at[0])

    def at_tile(jj, fn):
        pl.when(j == jj)(fn)

    @pl.when(i == 0)
    def _():
        at_tile(0, lambda: weight_copy(0).start())
        for jj in range(len(tiles)):
            at_tile(jj, lambda jj=jj: weight_copy(jj).wait())
        wb_ref[...] = stage_ref[...].astype(BF16)
        for jj in range(len(tiles) - 1):
            at_tile(jj, lambda jj=jj: weight_copy(jj + 1).start())

    for act in dict.fromkeys(act for _, _, act in tiles):
        uses = functools.reduce(jnp.logical_or, [j == jj for jj, t in enumerate(tiles) if t[2] is act])

        @pl.when(uses)
        def _(act=act):
            tm = u_ref.shape[0]
            cuts = [0] + [tm - tm // div // BF16_SUBLANES * BF16_SUBLANES for div in (2, 8)] + [tm]
            for rows in (slice(lo, hi) for lo, hi in zip(cuts[:-1], cuts[1:])):
                y = act(jnp.dot(u_ref[rows, :], wb_ref[...], preferred_element_type=F32))
                if has_res:
                    y = r_ref[rows, :] + y
                o_ref[rows, :] = y.astype(o_ref.dtype)


def _matmul(u, layer, ws, acts, res=None, *, tm, tn=1024, out_dtype=F32):
    m, d = u.shape
    tiles = []
    for src, (w, act) in enumerate(zip(ws, acts)):
        assert w.shape[2] % tn == 0
        tiles += [(src, col0, act) for col0 in range(0, w.shape[2], tn)]
    n = len(tiles) * tn
    io_spec = lambda: pl.BlockSpec((tm, tn), lambda j, i: (i, j))
    has_res = res is not None
    return pl.pallas_call(
        functools.partial(_matmul_body, tiles=tuple(tiles), layer=layer, has_res=has_res),
        grid=(len(tiles), m // tm),
        in_specs=[pl.BlockSpec((tm, d), lambda j, i: (i, 0))]
        + [pl.BlockSpec(memory_space=pl.ANY)] * len(ws) + ([io_spec()] if has_res else []),
        out_specs=io_spec(),
        out_shape=jax.ShapeDtypeStruct((m, n), out_dtype),
        scratch_shapes=[pltpu.VMEM((d, tn), F32), pltpu.VMEM((d, tn), BF16),
                        pltpu.SemaphoreType.DMA((1,))],
        compiler_params=_cparams("arbitrary", "arbitrary"),
        name="matmul",
    )(u, *ws, *([res] if has_res else []))


def _prompt_rows_of_tile(o_ref, nt, per_tile, tb):
    blk = pl.program_id(0) * nt + pl.program_id(1)
    k = blk % per_tile
    tp = per_tile * tb

    @pl.when(k == 0)
    def _():
        o_ref[0, tp:, :] = jnp.zeros((o_ref.shape[1] - tp, o_ref.shape[2]), o_ref.dtype)

    return pl.ds(pl.multiple_of(k * tb, tb), tb)


def _sigmoid(x):
    return 0.5 * jnp.tanh(0.5 * x) + 0.5


def _rglru_coeffs(conv_out, wa_ref, ba_ref, wi_ref, bi_ref, lam_ref, store):
    n_blocks, bw, _ = wa_ref.shape
    for n in range(n_blocks):
        cols = slice(n * bw, (n + 1) * bw)
        xb = conv_out(cols)
        xb16 = xb.astype(BF16)
        wa = wa_ref[n].astype(BF16)
        wi = wi_ref[n].astype(BF16)
        r = _sigmoid(jnp.dot(xb16, wa, preferred_element_type=F32) + ba_ref[:, cols])
        i = _sigmoid(jnp.dot(xb16, wi, preferred_element_type=F32) + bi_ref[:, cols])
        log_a = -RG_C * r * _softplus(-lam_ref[:, cols])
        a = jnp.exp(log_a)
        b = jnp.sqrt(1.0 - a * a) * (i * xb)
        store(n, a, b)


def _rg_prompt_body(xw_ref, gate_ref, cw_ref, cb_ref, wa_ref, ba_ref, wi_ref, bi_ref, lam_ref,
                    o_ref, hl_ref, cl_ref, ext_ref, a_ref, b_ref, h_ref, *, nt, per_tile):
    t = pl.program_id(1)
    tb = xw_ref.shape[1]
    bw = wa_ref.shape[1]
    pad = SUBLANES

    @pl.when(t == 0)
    def _():
        ext_ref[0:pad, :] = jnp.zeros((pad, ext_ref.shape[1]), F32)
        h_ref[...] = jnp.zeros_like(h_ref)

    ext_ref[pad:pad + tb, :] = xw_ref[0]

    def conv_out(cols):
        ext = ext_ref[:, cols]
        y = ext * cw_ref[0:1, cols]
        for j in range(1, CONV_W):
            y = ext * cw_ref[j:j + 1, cols] + pltpu.roll(y, 1, axis=0)
        return cb_ref[:, cols] + y[pad:, :]

    seg = tb // SUBLANES
    pitch = a_ref.shape[1] // SUBLANES
    n_slabs = a_ref.shape[0]

    def store(n, a, b):
        for half in range(bw // LANES):
            lanes = slice(half * LANES, (half + 1) * LANES)
            for s in range(SUBLANES):
                dst = slice(s * pitch, s * pitch + seg)
                a_ref[n * (bw // LANES) + half, dst, :] = a[s * seg:(s + 1) * seg, lanes]
                b_ref[n * (bw // LANES) + half, dst, :] = b[s * seg:(s + 1) * seg, lanes]

    _rglru_coeffs(conv_out, wa_ref, ba_ref, wi_ref, bi_ref, lam_ref, store)

    def step(v, carry):
        hs, ps = carry
        rows = pl.ds(v, SUBLANES, stride=pitch)
        new_h, new_p = [], []
        for slab in range(n_slabs):
            a = a_ref[slab, rows, :]
            h = a * hs[slab] + b_ref[slab, rows, :]
            p = a * ps[slab]
            b_ref[slab, rows, :] = h
            a_ref[slab, rows, :] = p
            new_h.append(h)
            new_p.append(p)
        return tuple(new_h), tuple(new_p)

    zeros = tuple(jnp.zeros((SUBLANES, LANES), F32) for _ in range(n_slabs))
    ones = tuple(jnp.ones((SUBLANES, LANES), F32) for _ in range(n_slabs))
    lax.fori_loop(0, seg, step, (zeros, ones))

    out_rows = _prompt_rows_of_tile(o_ref, nt, per_tile, tb)
    for slab in range(n_slabs):
        lanes = slice(slab * LANES, (slab + 1) * LANES)
        h_in = h_ref[0:1, lanes]
        for s in range(SUBLANES):
            src = slice(s * pitch, s * pitch + seg)
            h = b_ref[slab, src, :] + a_ref[slab, src, :] * h_in
            dst = pl.ds(pl.multiple_of(out_rows.start + s * seg, seg), seg)
            o_ref[0, dst, lanes] = (h * gate_ref[0, s * seg:(s + 1) * seg, lanes]).astype(o_ref.dtype)
            h_in = h[seg - 1:seg, :]
        h_ref[0:1, lanes] = h_in
        hl_ref[0, :, lanes] = h_in
    cl_ref[0] = ext_ref[pad + tb - (CONV_W - 1):pad + tb, :]
    ext_ref[0:pad, :] = ext_ref[tb:tb + pad, :]


def _segment_pitch(seg):
    pitch = -(-seg // SUBLANES) * SUBLANES
    while (pitch // SUBLANES) % 2 == 0:
        pitch += SUBLANES
    return pitch


def _rg_params(layer, conv_w, conv_b, w_a, b_a, w_i, b_i, lam):
    operands = [_layer_full(conv_w, layer), _layer_vector(conv_b, layer), _layer_full(w_a, layer),
                _layer_vector(b_a, layer), _layer_full(w_i, layer), _layer_vector(b_i, layer),
                _layer_vector(lam, layer)]
    return [arr for arr, _ in operands], [spec for _, spec in operands]


def _rg_prompt(proj, params, *, batch, seq, tp, ts, tb=256):
    p_args, p_specs = params
    m, d2 = proj.shape
    d = d2 // 2
    tm = tp + ts
    nt = seq // tb
    per_tile = tp // tb
    proj3 = proj.reshape(N_TOKEN_TILES, tm, d2)

    def tok(col):
        def index(b, t):
            blk = b * nt + t
            return (blk // per_tile, blk % per_tile, col)
        return index

    mix, hl, cl = pl.pallas_call(
        functools.partial(_rg_prompt_body, nt=nt, per_tile=per_tile),
        grid=(batch, nt),
        in_specs=[pl.BlockSpec((1, tb, d), tok(1)), pl.BlockSpec((1, tb, d), tok(0))] + p_specs,
        out_specs=[
            pl.BlockSpec((1, tm, d), lambda b, t: ((b * nt + t) // per_tile, 0, 0)),
            pl.BlockSpec((1, 1, d), lambda b, t: (b, 0, 0)),
            pl.BlockSpec((1, CONV_W - 1, d), lambda b, t: (b, 0, 0)),
        ],
        out_shape=[
            jax.ShapeDtypeStruct((N_TOKEN_TILES, tm, d), BF16),
            jax.ShapeDtypeStruct((batch, 1, d), F32),
            jax.ShapeDtypeStruct((batch, CONV_W - 1, d), F32),
        ],
        scratch_shapes=[
            pltpu.VMEM((tb + SUBLANES, d), F32),
            pltpu.VMEM((d // LANES, SUBLANES * _segment_pitch(tb // SUBLANES), LANES), F32),
            pltpu.VMEM((d // LANES, SUBLANES * _segment_pitch(tb // SUBLANES), LANES), F32),
            pltpu.VMEM((SUBLANES, d), F32),
        ],
        compiler_params=_cparams("arbitrary", "arbitrary"),
        name="rg_prompt",
    )(proj3, proj3, *p_args)
    return mix, hl, cl


def _rg_sample_body(mix_ref, xw_ref, gate_ref, conv_ref, h0_ref, cw_ref, cb_ref, wa_ref, ba_ref,
                    wi_ref, bi_ref, lam_ref, o_ref, hn_ref, cn_ref, a_ref, b_ref):
    del mix_ref
    n, d = h0_ref.shape
    bw = wa_ref.shape[1]
    x = xw_ref[...].reshape(n, d)
    xc = cb_ref[...]
    for j in range(CONV_W - 1):
        xc = xc + conv_ref[:, j * d:(j + 1) * d] * cw_ref[j:j + 1, :]
    xc = xc + x * cw_ref[CONV_W - 1:CONV_W, :]

    def store(k, a, b):
        a_ref[:, k * bw:(k + 1) * bw] = a
        b_ref[:, k * bw:(k + 1) * bw] = b

    _rglru_coeffs(lambda cols: xc[:, cols], wa_ref, ba_ref, wi_ref, bi_ref, lam_ref, store)
    h = b_ref[...] + a_ref[...] * h0_ref[...]
    hn_ref[...] = h
    o_ref[...] = (h * gate_ref[...].reshape(n, d)).reshape(o_ref.shape).astype(o_ref.dtype)
    cn_ref[:, 0:(CONV_W - 2) * d] = conv_ref[:, d:(CONV_W - 1) * d]
    cn_ref[:, (CONV_W - 2) * d:] = x


def _rg_sample(mix, proj, layer, conv_state, h0, params, *, tp, ts):
    p_args, p_specs = params
    n_layers, n, d = h0.shape
    tm = tp + ts
    proj3 = proj.reshape(N_TOKEN_TILES, tm, 2 * d)
    s_blk = tp // ts
    full2 = lambda shape: pl.BlockSpec(shape, lambda i: (0, 0))
    kc = (CONV_W - 1) * d
    conv_arg, conv_spec = _layer_full(conv_state.reshape(n_layers, n, kc), layer)
    h0_arg, h0_spec = _layer_full(h0, layer)
    return pl.pallas_call(
        _rg_sample_body,
        grid=(1,),
        in_specs=[
            pl.BlockSpec(memory_space=pl.ANY),
            pl.BlockSpec((N_TOKEN_TILES, ts, d), lambda i: (0, s_blk, 1)),
            pl.BlockSpec((N_TOKEN_TILES, ts, d), lambda i: (0, s_blk, 0)),
            conv_spec, h0_spec,
        ] + p_specs,
        out_specs=[
            pl.BlockSpec((N_TOKEN_TILES, ts, d), lambda i: (0, s_blk, 0)),
            full2((n, d)),
            full2((n, kc)),
        ],
        out_shape=[
            jax.ShapeDtypeStruct(mix.shape, mix.dtype),
            jax.ShapeDtypeStruct((n, d), F32),
            jax.ShapeDtypeStruct((n, kc), F32),
        ],
        scratch_shapes=[pltpu.VMEM((n, d), F32), pltpu.VMEM((n, d), F32)],
        input_output_aliases={0: 0},
        compiler_params=_cparams("arbitrary"),
        name="rg_sample",
    )(mix, proj3, proj3, conv_arg, h0_arg, *p_args)


def _gla_log_decay(u, w1_ref, w2_ref, b_ref):
    t = jnp.dot(u, w1_ref[...].astype(BF16), preferred_element_type=F32)
    logit = jnp.dot(t.astype(BF16), w2_ref[...].astype(BF16), preferred_element_type=F32) + b_ref[...]
    log_sigmoid = jnp.minimum(logit, 0.0) - jnp.log(1.0 + jnp.exp(-jnp.abs(logit)))
    return log_sigmoid * (1.0 / GLA_GATE_NORM)


def _gla_sample_update(q_ref, k_ref, v_ref, r_ref, g, s0_ref, on_ref, s_ref):
    bb, dk = q_ref.shape[1:]
    alpha_t = jnp.exp(g).T
    k_t = k_ref[0].astype(F32).T
    q_t = (q_ref[0].astype(F32) * dk ** -0.5).T
    v = v_ref[0].astype(F32)
    rows = []
    for b in range(bb):
        s_new = alpha_t[:, b:b + 1] * s0_ref[b, 0] + k_t[:, b:b + 1] * v[b:b + 1, :]
        s_ref[b, 0] = s_new
        rows.append(jnp.sum(q_t[:, b:b + 1] * s_new, axis=0, keepdims=True))
    o = jnp.concatenate(rows, axis=0)
    o = _rms_scale(o) * on_ref[...]
    return o * r_ref[0].astype(F32)


def _gla_body(q_ref, k_ref, v_ref, r_ref, u_ref, on_ref, wg1_ref, wg2_ref, bg_ref,
              sq_ref, sk_ref, sv_ref, sr_ref, su_ref, s0_ref, o_ref, s_ref, ss_ref, *,
              heads, nt, per_tile):
    t = pl.program_id(1)
    tb = q_ref.shape[1]
    tp = per_tile * tb
    c = GLA_CHUNK
    dk = q_ref.shape[2] // heads
    dv = v_ref.shape[2] // heads

    @pl.when(t == 0)
    def _():
        s_ref[...] = jnp.zeros_like(s_ref)

    blk = pl.program_id(0) * nt + t
    head_s = blk % heads
    g_s_all = _gla_log_decay(su_ref[0], wg1_ref, wg2_ref, bg_ref)
    g_s = jnp.zeros((g_s_all.shape[0], dk), F32)
    for h in range(heads):
        g_s = jnp.where(head_s == h, g_s_all[:, h * dk:(h + 1) * dk], g_s)
    o_s = _gla_sample_update(sq_ref, sk_ref, sv_ref, sr_ref, g_s, s0_ref, on_ref, ss_ref)
    o_s = o_s.astype(o_ref.dtype)

    row = lax.broadcasted_iota(jnp.int32, (tb, tb), 0)
    col = lax.broadcasted_iota(jnp.int32, (tb, tb), 1)
    same_chunk_causal = ((row >= col) & ((row ^ col) < c)).astype(BF16)
    g_rest = _gla_log_decay(u_ref[0], wg1_ref, wg2_ref, bg_ref)
    bcum = jnp.zeros(g_rest.shape, F32)
    for _ in range(3):
        g_part = g_rest.astype(BF16)
        bcum = bcum + jnp.dot(same_chunk_causal, g_part, preferred_element_type=F32)
        g_rest = g_rest - g_part.astype(F32)
    nc = tb // c
    chunk_rows = [slice(ci * c, (ci + 1) * c) for ci in range(nc)]
    q = q_ref[0].astype(F32) * dk ** -0.5
    k = k_ref[0].astype(F32)
    q_i = q * jnp.exp(bcum)
    k_i = (k * jnp.exp(-bcum)).astype(BF16)
    g_tot = [bcum[(ci + 1) * c - 1:(ci + 1) * c, :] for ci in range(nc)]
    g_before = [jnp.zeros_like(g_tot[0])]
    for ci in range(nc):
        g_before.append(g_before[-1] + g_tot[ci])
    k_e = [k[rows, :] * jnp.exp(g_tot[ci] - bcum[rows, :]) for ci, rows in enumerate(chunk_rows)]
    q_in = jnp.concatenate([(q_i[rows, :] * jnp.exp(g_before[ci])).astype(BF16)
                            for ci, rows in enumerate(chunk_rows)], axis=0)
    k_out = jnp.concatenate([(k_e[ci] * jnp.exp(g_before[nc] - g_before[ci + 1])).astype(BF16)
                             for ci in range(nc)], axis=0)
    decay_t = jnp.exp(jnp.broadcast_to(g_before[nc], (SUBLANES, heads * dk))).T
    q_i = q_i.astype(BF16)

    out_base = (blk % per_tile) * tb
    o_inter = []
    for h in range(heads):
        kcols = slice(h * dk, (h + 1) * dk)
        vcols = slice(h * dv, (h + 1) * dv)
        s = s_ref[0, h]
        o_inter.append(jnp.dot(q_in[:, kcols], s.astype(BF16), preferred_element_type=F32))
        s_ref[0, h] = decay_t[kcols, 0:1] * s + lax.dot_general(
            k_out[:, kcols], v_ref[0, :, vcols], (((0,), (0,)), ((), ())), preferred_element_type=F32)

    for ci, rows in enumerate(chunk_rows):
        out_rows = pl.ds(pl.multiple_of(out_base + ci * c, c), c)
        n_keys = (ci + 1) * c
        keys = jnp.concatenate(
            [(k_e[cj] * jnp.exp(g_before[ci] - g_before[cj + 1])).astype(BF16) for cj in range(ci)]
            + [k_i[rows, :]], axis=0)
        visible = (lax.broadcasted_iota(jnp.int32, (c, n_keys), 1)
                   <= lax.broadcasted_iota(jnp.int32, (c, n_keys), 0) + ci * c)
        for h in range(heads):
            kcols = slice(h * dk, (h + 1) * dk)
            vcols = slice(h * dv, (h + 1) * dv)
            scores = lax.dot_general(q_i[rows, kcols], keys[:, kcols], (((1,), (1,)), ((), ())),
                                     preferred_element_type=F32)
            scores = jnp.where(visible, scores, 0.0).astype(BF16)
            o = o_inter[h][rows, :] + jnp.dot(scores, v_ref[0, 0:n_keys, vcols],
                                              preferred_element_type=F32)
            o = _rms_scale(o) * on_ref[...]
            o_ref[0, out_rows, vcols] = (o * r_ref[0, rows, vcols].astype(F32)).astype(o_ref.dtype)

    for h in range(heads):
        @pl.when(head_s == h)
        def _(h=h):
            o_ref[0, tp:, h * dv:(h + 1) * dv] = o_s


def _gla(qkvr, u, layer, gate_params, s0, onorm, *, batch, seq, heads, dk, dv, tp, ts, tb=256):
    tm = tp + ts
    nt = seq // tb
    per_tile = tp // tb
    s_blk = tp // ts
    hk, hv = heads * dk, heads * dv
    assert hv == 2 * hk
    assert batch * nt == N_TOKEN_TILES * heads and per_tile == heads
    qkvr3 = qkvr.reshape(N_TOKEN_TILES, tm, qkvr.shape[1])
    d = u.shape[1]
    u3 = u.reshape(N_TOKEN_TILES, tm, d)
    resident = lambda a: pl.BlockSpec(a.shape, lambda b, t: (0,) * a.ndim)
    kq, vq, rq = heads, (2 * hk) // dv, (2 * hk) // dv + heads

    def tok(col):
        def index(b, t):
            blk = b * nt + t
            return (blk // per_tile, blk % per_tile, col)
        return index

    def smp(col0):
        def index(b, t):
            blk = b * nt + t
            return (blk // heads, s_blk, col0 + blk % heads)
        return index

    state = lambda b, t: ((b * nt + t) // heads, (b * nt + t) % heads, 0, 0)
    on_arg, on_spec = _layer_vector(onorm, layer)
    return pl.pallas_call(
        functools.partial(_gla_body, heads=heads, nt=nt, per_tile=per_tile),
        grid=(batch, nt),
        in_specs=[
            pl.BlockSpec((1, tb, hk), tok(0)),
            pl.BlockSpec((1, tb, hk), tok(1)),
            pl.BlockSpec((1, tb, hv), tok(1)),
            pl.BlockSpec((1, tb, hv), tok(2)),
            pl.BlockSpec((1, tb, d), tok(0)),
            on_spec,
            *[resident(a) for a in gate_params],
            pl.BlockSpec((1, ts, dk), smp(0)),
            pl.BlockSpec((1, ts, dk), smp(kq)),
            pl.BlockSpec((1, ts, dv), smp(vq)),
            pl.BlockSpec((1, ts, dv), smp(rq)),
            pl.BlockSpec((1, ts, d), lambda b, t: ((b * nt + t) // heads, s_blk, 0)),
            _layer_operand(s0, layer, (ts, 1, dk, dv), state)[1],
        ],
        out_specs=[
            pl.BlockSpec((1, tm, hv), lambda b, t: ((b * nt + t) // per_tile, 0, 0)),
            pl.BlockSpec((1, heads, dk, dv), lambda b, t: (b, 0, 0, 0)),
            pl.BlockSpec((ts, 1, dk, dv), state),
        ],
        out_shape=[
            jax.ShapeDtypeStruct((N_TOKEN_TILES, tm, hv), BF16),
            jax.ShapeDtypeStruct((batch, heads, dk, dv), F32),
            jax.ShapeDtypeStruct(s0.shape[1:], F32),
        ],
        compiler_params=_cparams("arbitrary", "arbitrary"),
        name="gla",
    )(qkvr3, qkvr3, qkvr3, qkvr3, u3, on_arg, *gate_params, qkvr3, qkvr3, qkvr3, qkvr3, u3, s0)


def kernel(x_prompt, x_sample, state_rglru_h, state_rglru_conv, state_gla_S, ln_ffn1, ffn1_w1, ffn1_w3, ffn1_w2, ln_mix, ln_ffn2, ffn2_w1, ffn2_w3, ffn2_w2, rg_w_y, rg_w_x, rg_conv_w, rg_conv_b, rg_w_a, rg_b_a, rg_w_i, rg_b_i, rg_lambda, rg_w_o, gla_w_q, gla_w_k, gla_w_v, gla_w_g1, gla_w_g2, gla_b_g, gla_w_r, gla_onorm, gla_w_o, ln_final):
    batch, seq, d = x_prompt.shape
    n_dec = x_sample.shape[0]
    depth = ln_ffn1.shape[0]
    heads, dk, dv = state_gla_S.shape[2:]
    tp, ts = _token_tiles(batch * seq, n_dec)
    tm = tp + ts
    tiles = dict(tp=tp, ts=ts)

    x = (x_prompt.reshape(batch * seq, d), x_sample.reshape(n_dec, d))

    ffn_f32 = []
    for i in range(depth):
        ffn_f32 += [((ffn1_w1, ffn1_w3, ffn1_w2), i), ((ffn2_w1, ffn2_w3, ffn2_w2), i)]
    ffn_f32.append(None)
    ffn_w = tuple(_to_bf16(w, 0) for w in ffn_f32[0][0])

    p_h, p_conv, p_s, s_h, s_conv, s_s = [], [], [], [], [], []
    for i in range(depth):
        j = i // 2
        x, u, ffn_w = _ffn(x, i, ln_ffn1, ffn_w, g_mix=ln_mix, next_weights=ffn_f32[2 * i + 1], **tiles)
        if i % 2 == 0:
            proj = _matmul(u, j, [rg_w_y, rg_w_x], [_gelu_tanh, _identity], tm=tm)
            params = _rg_params(j, rg_conv_w, rg_conv_b, rg_w_a, rg_b_a, rg_w_i, rg_b_i, rg_lambda)
            mix, hl, cl = _rg_prompt(proj, params, batch=batch, seq=seq, **tiles)
            mix, hn, cn = _rg_sample(mix, proj, j, state_rglru_conv, state_rglru_h, params, **tiles)
            p_h.append(hl.reshape(batch, d))
            p_conv.append(cl)
            s_h.append(hn)
            s_conv.append(cn.reshape(n_dec, CONV_W - 1, d))
            w_o = rg_w_o
        else:
            qkvr = _matmul(u, j, [gla_w_q, gla_w_k, gla_w_v, gla_w_r],
                           [_identity, _identity, _identity, _silu], tm=tm, out_dtype=BF16)
            rank = gla_w_g1.shape[2]
            gate_params = (jnp.pad(gla_w_g1[j], ((0, 0), (0, LANES - rank))).astype(BF16),
                           jnp.pad(gla_w_g2[j], ((0, LANES - rank), (0, 0))).astype(BF16),
                           gla_b_g[j].reshape(1, heads * dk))
            mix, s_fin, s_new = _gla(qkvr, u, j, gate_params, state_gla_S, gla_onorm, batch=batch,
                                     seq=seq, heads=heads, dk=dk, dv=dv, **tiles)
            p_s.append(s_fin)
            s_s.append(s_new)
            w_o = gla_w_o
        x = _matmul(mix.reshape(N_TOKEN_TILES * tm, d), j, [w_o], [_identity], res=x,
                    tm=OUT_PROJ_ROW_TILE, tn=d)
        g_final = ln_final.reshape(1, d) if i == depth - 1 else None
        tf = 512 if g_final is None else 256
        x, _, ffn_w = _ffn(x, i, ln_ffn2, ffn_w, g_final=g_final, next_weights=ffn_f32[2 * i + 2], tf=tf,
                           **tiles)

    y_prompt = x[0].reshape(batch, seq, d)
    y_sample = x[1].reshape(n_dec, 1, d)
    return (y_prompt, y_sample, jnp.stack(p_h), jnp.stack(p_conv), jnp.stack(p_s),
            jnp.stack(s_h), jnp.stack(s_conv), jnp.stack(s_s))
```

```python
import functools

import jax
import jax.numpy as jnp
from jax import lax
from jax.experimental import pallas as pl
from jax.experimental.pallas import tpu as pltpu

F32 = jnp.float32
BF16 = jnp.bfloat16

RMS_EPS = 1e-6
RG_C = 8.0
CONV_W = 4
GLA_GATE_NORM = 16.0
GLA_CHUNK = 64

V7X_VMEM_LIMIT_BYTES = 63 * 1024 * 1024
SUBLANES = 8
LANES = 128
BF16_SUBLANES = 16
N_TOKEN_TILES = 8


def _cparams(*sem):
    return pltpu.CompilerParams(dimension_semantics=sem, vmem_limit_bytes=V7X_VMEM_LIMIT_BYTES)


def _rms_scale(x):
    return x * lax.rsqrt(jnp.mean(x * x, axis=-1, keepdims=True) + RMS_EPS)


def _silu(x):
    return x * jax.nn.sigmoid(x)


def _gelu_tanh(y):
    return jax.nn.gelu(y, approximate=True)


def _identity(y):
    return y


def _softplus(x):
    return jnp.maximum(x, 0.0) + jnp.log1p(jnp.exp(-jnp.abs(x)))


FFN_PIECE_ROWS = (256, 256, 256)


def _layer_operand(stacked, layer, block, index_map):
    spec = pl.BlockSpec((None,) + tuple(block), lambda *g: (layer,) + tuple(index_map(*g)))
    return stacked, spec


def _layer_vector(stacked, layer):
    n_layers, n = stacked.shape
    return _layer_operand(stacked.reshape(n_layers, 1, n), layer, (1, n), lambda *g: (0, 0))


def _layer_full(stacked, layer):
    block = stacked.shape[1:]
    return _layer_operand(stacked, layer, block, lambda *g: (0,) * len(block))


def _token_tiles(n_prompt, n_sample):
    tp, ts = n_prompt // N_TOKEN_TILES, n_sample // N_TOKEN_TILES
    assert tp * N_TOKEN_TILES == n_prompt and ts * N_TOKEN_TILES == n_sample
    assert tp % BF16_SUBLANES == 0 and ts % BF16_SUBLANES == 0 and tp % ts == 0
    return tp, ts


def _cast_body(w_ref, o_ref):
    o_ref[...] = w_ref[...].astype(o_ref.dtype)


def _to_bf16(w, layer, *, rows=512):
    _, r, c = w.shape
    return pl.pallas_call(
        _cast_body,
        grid=(r // rows,),
        in_specs=[pl.BlockSpec((None, rows, c), lambda i: (layer, i, 0))],
        out_specs=pl.BlockSpec((rows, c), lambda i: (i, 0)),
        out_shape=jax.ShapeDtypeStruct((r, c), BF16),
        compiler_params=_cparams("arbitrary"),
        name="to_bf16",
    )(w)


def _ffn_body(*refs, tp, ts, split_in, split_out, mix_norm, cast_next):
    refs = list(refs)
    x_refs = [refs.pop(0) for _ in range(2 if split_in else 1)]
    g_ref, w1_ref, w3_ref, w2_ref = (refs.pop(0) for _ in range(4))
    g2_ref = refs.pop(0) if (split_out or mix_norm) else None
    next_f32 = [refs.pop(0) for _ in range(3 if cast_next else 0)]
    o_refs = [refs.pop(0) for _ in range(2 if split_out else 1)]
    u_hbm = refs.pop(0) if mix_norm else None
    next_bf16 = [refs.pop(0) for _ in range(3 if cast_next else 0)]
    xn_ref = refs.pop(0)
    acc_ref = refs.pop(0) if split_out else o_refs[0]
    (sem,) = refs
    tm = tp + ts
    i = pl.program_id(0)
    f = pl.program_id(1)

    n_f = pl.num_programs(1)

    def segments(blocks):
        if len(blocks) == 2:
            return [(0, tp, blocks[0]), (tp, ts, blocks[1])]
        return [(0, tm, blocks[0])]

    def parts(blocks, r0, r1):
        out = []
        for t0, n_rows, ref in segments(blocks):
            lo, hi = max(r0, t0), min(r1, t0 + n_rows)
            if lo < hi:
                out.append((ref, slice(lo - t0, hi - t0)))
        return out

    def read_rows(blocks, r0, r1):
        vals = [ref[rows, :] for ref, rows in parts(blocks, r0, r1)]
        return vals[0] if len(vals) == 1 else jnp.concatenate(vals, axis=0)

    def write_rows(blocks, r0, r1, value):
        off = 0
        for ref, rows in parts(blocks, r0, r1):
            n = rows.stop - rows.start
            ref[rows, :] = value[off:off + n]
            off += n

    def ffn_product(xn):
        h1 = jnp.dot(xn, w1_ref[...].astype(BF16), preferred_element_type=F32)
        h3 = jnp.dot(xn, w3_ref[...].astype(BF16), preferred_element_type=F32)
        h = (_silu(h1) * h3).astype(BF16)
        return jnp.dot(h, w2_ref[...].astype(BF16), preferred_element_type=F32)

    cuts = [0] + [sum(FFN_PIECE_ROWS[:k + 1]) for k in range(len(FFN_PIECE_ROWS))] + [tm]
    assert cuts[-2] < tm and all(c % BF16_SUBLANES == 0 for c in cuts)
    pieces = list(zip(cuts[:-1], cuts[1:]))
    pieces_rev = [(tm - r1, tm - r0) for r0, r1 in pieces][::-1]

    def cast_next_block():
        for src_ref, dst_ref in zip(next_f32, next_bf16):
            dst_ref[...] = src_ref[...].astype(BF16)

    @pl.when(f == 0)
    def _():
        for r0, r1 in pieces:
            x = read_rows(x_refs, r0, r1)
            xn = (_rms_scale(x) * g_ref[...]).astype(BF16)
            xn_ref[r0:r1, :] = xn
            acc_ref[r0:r1, :] = 2.0 * x + ffn_product(xn)
        cast_next_block()

    @pl.when((f > 0) & (f < n_f - 1))
    def _():
        acc_ref[...] += ffn_product(xn_ref[...])
        cast_next_block()

    @pl.when(f == n_f - 1)
    def _():
        cast_next_block()
        for r0, r1 in pieces_rev:
            y = 0.5 * (acc_ref[r0:r1, :] + ffn_product(xn_ref[r0:r1, :]))
            if split_out:
                y = _rms_scale(y) * g2_ref[...]
            write_rows(o_refs, r0, r1, y)
            if mix_norm:
                xn_ref[r0:r1, :] = (_rms_scale(y) * g2_ref[...]).astype(BF16)
        if mix_norm:
            cu = pltpu.make_async_copy(xn_ref, u_hbm.at[pl.ds(i * tm, tm)], sem.at[0])
            cu.start()
            cu.wait()


def _ffn(x, layer, g, weights, *, tp, ts, g_mix=None, g_final=None, next_weights=None, tf=512):
    split_in = isinstance(x, tuple)
    split_out = g_final is not None
    mix_norm = g_mix is not None
    cast_next = next_weights is not None
    assert not (split_out and mix_norm)
    xs = list(x) if split_in else [x]
    w1, w3, w2 = weights
    d, f = w1.shape
    tm = tp + ts
    nf = f // tf
    operands = [
        _layer_vector(g, layer),
        (w1, pl.BlockSpec((d, tf), lambda i, j: (0, j))),
        (w3, pl.BlockSpec((d, tf), lambda i, j: (0, j))),
        (w2, pl.BlockSpec((tf, d), lambda i, j: (j, 0))),
    ]
    if mix_norm:
        operands.append(_layer_vector(g_mix, layer))
    if split_out:
        operands.append(_layer_vector(g_final, 0))
    row_tile = lambda rows: pl.BlockSpec((rows, d), lambda i, j: (i, 0))
    x_specs = [row_tile(tp), row_tile(ts)] if split_in else [row_tile(tm)]
    if split_out:
        out_specs = [row_tile(tp), row_tile(ts)]
        out_shape = [jax.ShapeDtypeStruct((tp * N_TOKEN_TILES, d), F32),
                     jax.ShapeDtypeStruct((ts * N_TOKEN_TILES, d), F32)]
    else:
        out_specs = [row_tile(tm)]
        out_shape = [jax.ShapeDtypeStruct((tm * N_TOKEN_TILES, d), F32)]
    if mix_norm:
        out_specs.append(pl.BlockSpec(memory_space=pl.ANY))
        out_shape.append(jax.ShapeDtypeStruct((tm * N_TOKEN_TILES, d), BF16))
    if cast_next:
        (n1, n3, n2), next_layer = next_weights
        db = d // N_TOKEN_TILES
        assert n1.shape[1:] == (d, f) and n2.shape[1:] == (f, d) and db * N_TOKEN_TILES == d
        operands += [
            _layer_operand(n1, next_layer, (db, tf), lambda i, j: (i, j)),
            _layer_operand(n3, next_layer, (db, tf), lambda i, j: (i, j)),
            _layer_operand(n2, next_layer, (tf, db), lambda i, j: (j, i)),
        ]
        out_specs += [pl.BlockSpec((db, tf), lambda i, j: (i, j)), pl.BlockSpec((db, tf), lambda i, j: (i, j)),
                      pl.BlockSpec((tf, db), lambda i, j: (j, i))]
        out_shape += [jax.ShapeDtypeStruct((d, f), BF16), jax.ShapeDtypeStruct((d, f), BF16),
                      jax.ShapeDtypeStruct((f, d), BF16)]
    out = pl.pallas_call(
        functools.partial(_ffn_body, tp=tp, ts=ts, split_in=split_in, split_out=split_out,
                          mix_norm=mix_norm, cast_next=cast_next),
        grid=(N_TOKEN_TILES, nf),
        in_specs=x_specs + [spec for _, spec in operands],
        out_specs=out_specs,
        out_shape=out_shape,
        scratch_shapes=[pltpu.VMEM((tm, d), BF16)] + ([pltpu.VMEM((tm, d), F32)] if split_out else [])
        + [pltpu.SemaphoreType.DMA((1,))],
        compiler_params=_cparams("arbitrary", "arbitrary"),
        name="ffn",
    )(*xs, *[arr for arr, _ in operands])
    out = list(out)
    acts = tuple(out[:2]) if split_out else out[0]
    rest = out[2:] if split_out else out[1:]
    u = rest.pop(0) if mix_norm else None
    return acts, u, (tuple(rest) if cast_next else None)


def _matmul_body(u_ref, *refs, tiles, layer, has_res):
    n_src = 1 + max(src for src, _, _ in tiles)
    w_hbm = refs[:n_src]
    rest = refs[n_src:]
    if has_res:
        r_ref, o_ref, stage_ref, wb_ref, sem = rest
    else:
        o_ref, stage_ref, wb_ref, sem = rest
    tn = stage_ref.shape[1]
    j = pl.program_id(0)
    i = pl.program_id(1)

    def weight_copy(jj):
        src, col0, _ = tiles[jj]
        return pltpu.make_async_copy(w_hbm[src].at[layer, :, pl.ds(col0, tn)], stage_ref, sem.at[0])

    def at_tile(jj, fn):
        pl.when(j == jj)(fn)

    @pl.when(i == 0)
    def _():
        at_tile(0, lambda: weight_copy(0).start())
        for jj in range(len(tiles)):
            at_tile(jj, lambda jj=jj: weight_copy(jj).wait())
        wb_ref[...] = stage_ref[...].astype(BF16)
        for jj in range(len(tiles) - 1):
            at_tile(jj, lambda jj=jj: weight_copy(jj + 1).start())

    for act in dict.fromkeys(act for _, _, act in tiles):
        uses = functools.reduce(jnp.logical_or, [j == jj for jj, t in enumerate(tiles) if t[2] is act])

        @pl.when(uses)
        def _(act=act):
            y = act(jnp.dot(u_ref[...], wb_ref[...], preferred_element_type=F32))
            if has_res:
                y = r_ref[...] + y
            o_ref[...] = y.astype(o_ref.dtype)


def _matmul(u, layer, ws, acts, res=None, *, tm, tn=1024, out_dtype=F32):
    m, d = u.shape
    tiles = []
    for src, (w, act) in enumerate(zip(ws, acts)):
        assert w.shape[2] % tn == 0
        tiles += [(src, col0, act) for col0 in range(0, w.shape[2], tn)]
    n = len(tiles) * tn
    io_spec = lambda: pl.BlockSpec((tm, tn), lambda j, i: (i, j))
    has_res = res is not None
    return pl.pallas_call(
        functools.partial(_matmul_body, tiles=tuple(tiles), layer=layer, has_res=has_res),
        grid=(len(tiles), m // tm),
        in_specs=[pl.BlockSpec((tm, d), lambda j, i: (i, 0))]
        + [pl.BlockSpec(memory_space=pl.ANY)] * len(ws) + ([io_spec()] if has_res else []),
        out_specs=io_spec(),
        out_shape=jax.ShapeDtypeStruct((m, n), out_dtype),
        scratch_shapes=[pltpu.VMEM((d, tn), F32), pltpu.VMEM((d, tn), BF16),
                        pltpu.SemaphoreType.DMA((1,))],
        compiler_params=_cparams("arbitrary", "arbitrary"),
        name="matmul",
    )(u, *ws, *([res] if has_res else []))


def _prompt_rows_of_tile(o_ref, nt, per_tile, tb):
    blk = pl.program_id(0) * nt + pl.program_id(1)
    k = blk % per_tile
    tp = per_tile * tb

    @pl.when(k == 0)
    def _():
        o_ref[0, tp:, :] = jnp.zeros((o_ref.shape[1] - tp, o_ref.shape[2]), o_ref.dtype)

    return pl.ds(pl.multiple_of(k * tb, tb), tb)


def _sigmoid(x):
    return 0.5 * jnp.tanh(0.5 * x) + 0.5


def _rglru_coeffs(conv_out, wa_ref, ba_ref, wi_ref, bi_ref, lam_ref, store):
    n_blocks, bw, _ = wa_ref.shape
    for n in range(n_blocks):
        cols = slice(n * bw, (n + 1) * bw)
        xb = conv_out(cols)
        xb16 = xb.astype(BF16)
        wa = wa_ref[n].astype(BF16)
        wi = wi_ref[n].astype(BF16)
        r = _sigmoid(jnp.dot(xb16, wa, preferred_element_type=F32) + ba_ref[:, cols])
        i = _sigmoid(jnp.dot(xb16, wi, preferred_element_type=F32) + bi_ref[:, cols])
        log_a = -RG_C * r * _softplus(-lam_ref[:, cols])
        a = jnp.exp(log_a)
        b = jnp.sqrt(1.0 - a * a) * (i * xb)
        store(n, a, b)


def _rg_prompt_body(xw_ref, gate_ref, cw_ref, cb_ref, wa_ref, ba_ref, wi_ref, bi_ref, lam_ref,
                    o_ref, hl_ref, cl_ref, ext_ref, a_ref, b_ref, h_ref, *, nt, per_tile):
    t = pl.program_id(1)
    tb = xw_ref.shape[1]
    bw = wa_ref.shape[1]
    pad = SUBLANES

    @pl.when(t == 0)
    def _():
        ext_ref[0:pad, :] = jnp.zeros((pad, ext_ref.shape[1]), F32)
        h_ref[...] = jnp.zeros_like(h_ref)

    ext_ref[pad:pad + tb, :] = xw_ref[0]

    def conv_out(cols):
        ext = ext_ref[:, cols]
        y = ext * cw_ref[0:1, cols]
        for j in range(1, CONV_W):
            y = ext * cw_ref[j:j + 1, cols] + pltpu.roll(y, 1, axis=0)
        return cb_ref[:, cols] + y[pad:, :]

    seg = tb // SUBLANES
    pitch = a_ref.shape[1] // SUBLANES
    n_slabs = a_ref.shape[0]

    def store(n, a, b):
        for half in range(bw // LANES):
            lanes = slice(half * LANES, (half + 1) * LANES)
            for s in range(SUBLANES):
                dst = slice(s * pitch, s * pitch + seg)
                a_ref[n * (bw // LANES) + half, dst, :] = a[s * seg:(s + 1) * seg, lanes]
                b_ref[n * (bw // LANES) + half, dst, :] = b[s * seg:(s + 1) * seg, lanes]

    _rglru_coeffs(conv_out, wa_ref, ba_ref, wi_ref, bi_ref, lam_ref, store)

    def step(v, carry):
        hs, ps = carry
        rows = pl.ds(v, SUBLANES, stride=pitch)
        new_h, new_p = [], []
        for slab in range(n_slabs):
            a = a_ref[slab, rows, :]
            h = a * hs[slab] + b_ref[slab, rows, :]
            p = a * ps[slab]
            b_ref[slab, rows, :] = h
            a_ref[slab, rows, :] = p
            new_h.append(h)
            new_p.append(p)
        return tuple(new_h), tuple(new_p)

    zeros = tuple(jnp.zeros((SUBLANES, LANES), F32) for _ in range(n_slabs))
    ones = tuple(jnp.ones((SUBLANES, LANES), F32) for _ in range(n_slabs))
    lax.fori_loop(0, seg, step, (zeros, ones))

    out_rows = _prompt_rows_of_tile(o_ref, nt, per_tile, tb)
    for slab in range(n_slabs):
        lanes = slice(slab * LANES, (slab + 1) * LANES)
        h_in = h_ref[0:1, lanes]
        for s in range(SUBLANES):
            src = slice(s * pitch, s * pitch + seg)
            h = b_ref[slab, src, :] + a_ref[slab, src, :] * h_in
            dst = pl.ds(pl.multiple_of(out_rows.start + s * seg, seg), seg)
            o_ref[0, dst, lanes] = (h * gate_ref[0, s * seg:(s + 1) * seg, lanes]).astype(o_ref.dtype)
            h_in = h[seg - 1:seg, :]
        h_ref[0:1, lanes] = h_in
        hl_ref[0, :, lanes] = h_in
    cl_ref[0] = ext_ref[pad + tb - (CONV_W - 1):pad + tb, :]
    ext_ref[0:pad, :] = ext_ref[tb:tb + pad, :]


def _segment_pitch(seg):
    pitch = -(-seg // SUBLANES) * SUBLANES
    while (pitch // SUBLANES) % 2 == 0:
        pitch += SUBLANES
    return pitch


def _rg_params(layer, conv_w, conv_b, w_a, b_a, w_i, b_i, lam):
    operands = [_layer_full(conv_w, layer), _layer_vector(conv_b, layer), _layer_full(w_a, layer),
                _layer_vector(b_a, layer), _layer_full(w_i, layer), _layer_vector(b_i, layer),
                _layer_vector(lam, layer)]
    return [arr for arr, _ in operands], [spec for _, spec in operands]


def _rg_prompt(proj, params, *, batch, seq, tp, ts, tb=256):
    p_args, p_specs = params
    m, d2 = proj.shape
    d = d2 // 2
    tm = tp + ts
    nt = seq // tb
    per_tile = tp // tb
    proj3 = proj.reshape(N_TOKEN_TILES, tm, d2)

    def tok(col):
        def index(b, t):
            blk = b * nt + t
            return (blk // per_tile, blk % per_tile, col)
        return index

    mix, hl, cl = pl.pallas_call(
        functools.partial(_rg_prompt_body, nt=nt, per_tile=per_tile),
        grid=(batch, nt),
        in_specs=[pl.BlockSpec((1, tb, d), tok(1)), pl.BlockSpec((1, tb, d), tok(0))] + p_specs,
        out_specs=[
            pl.BlockSpec((1, tm, d), lambda b, t: ((b * nt + t) // per_tile, 0, 0)),
            pl.BlockSpec((1, 1, d), lambda b, t: (b, 0, 0)),
            pl.BlockSpec((1, CONV_W - 1, d), lambda b, t: (b, 0, 0)),
        ],
        out_shape=[
            jax.ShapeDtypeStruct((N_TOKEN_TILES, tm, d), BF16),
            jax.ShapeDtypeStruct((batch, 1, d), F32),
            jax.ShapeDtypeStruct((batch, CONV_W - 1, d), F32),
        ],
        scratch_shapes=[
            pltpu.VMEM((tb + SUBLANES, d), F32),
            pltpu.VMEM((d // LANES, SUBLANES * _segment_pitch(tb // SUBLANES), LANES), F32),
            pltpu.VMEM((d // LANES, SUBLANES * _segment_pitch(tb // SUBLANES), LANES), F32),
            pltpu.VMEM((SUBLANES, d), F32),
        ],
        compiler_params=_cparams("arbitrary", "arbitrary"),
        name="rg_prompt",
    )(proj3, proj3, *p_args)
    return mix, hl, cl


def _rg_sample_body(mix_ref, xw_ref, gate_ref, conv_ref, h0_ref, cw_ref, cb_ref, wa_ref, ba_ref,
                    wi_ref, bi_ref, lam_ref, o_ref, hn_ref, cn_ref, a_ref, b_ref):
    del mix_ref
    n, d = h0_ref.shape
    bw = wa_ref.shape[1]
    x = xw_ref[...].reshape(n, d)
    xc = cb_ref[...]
    for j in range(CONV_W - 1):
        xc = xc + conv_ref[:, j * d:(j + 1) * d] * cw_ref[j:j + 1, :]
    xc = xc + x * cw_ref[CONV_W - 1:CONV_W, :]

    def store(k, a, b):
        a_ref[:, k * bw:(k + 1) * bw] = a
        b_ref[:, k * bw:(k + 1) * bw] = b

    _rglru_coeffs(lambda cols: xc[:, cols], wa_ref, ba_ref, wi_ref, bi_ref, lam_ref, store)
    h = b_ref[...] + a_ref[...] * h0_ref[...]
    hn_ref[...] = h
    o_ref[...] = (h * gate_ref[...].reshape(n, d)).reshape(o_ref.shape).astype(o_ref.dtype)
    cn_ref[:, 0:(CONV_W - 2) * d] = conv_ref[:, d:(CONV_W - 1) * d]
    cn_ref[:, (CONV_W - 2) * d:] = x


def _rg_sample(mix, proj, layer, conv_state, h0, params, *, tp, ts):
    p_args, p_specs = params
    n_layers, n, d = h0.shape
    tm = tp + ts
    proj3 = proj.reshape(N_TOKEN_TILES, tm, 2 * d)
    s_blk = tp // ts
    full2 = lambda shape: pl.BlockSpec(shape, lambda i: (0, 0))
    kc = (CONV_W - 1) * d
    conv_arg, conv_spec = _layer_full(conv_state.reshape(n_layers, n, kc), layer)
    h0_arg, h0_spec = _layer_full(h0, layer)
    return pl.pallas_call(
        _rg_sample_body,
        grid=(1,),
        in_specs=[
            pl.BlockSpec(memory_space=pl.ANY),
            pl.BlockSpec((N_TOKEN_TILES, ts, d), lambda i: (0, s_blk, 1)),
            pl.BlockSpec((N_TOKEN_TILES, ts, d), lambda i: (0, s_blk, 0)),
            conv_spec, h0_spec,
        ] + p_specs,
        out_specs=[
            pl.BlockSpec((N_TOKEN_TILES, ts, d), lambda i: (0, s_blk, 0)),
            full2((n, d)),
            full2((n, kc)),
        ],
        out_shape=[
            jax.ShapeDtypeStruct(mix.shape, mix.dtype),
            jax.ShapeDtypeStruct((n, d), F32),
            jax.ShapeDtypeStruct((n, kc), F32),
        ],
        scratch_shapes=[pltpu.VMEM((n, d), F32), pltpu.VMEM((n, d), F32)],
        input_output_aliases={0: 0},
        compiler_params=_cparams("arbitrary"),
        name="rg_sample",
    )(mix, proj3, proj3, conv_arg, h0_arg, *p_args)


def _gla_log_decay(u, w1_ref, w2_ref, b_ref):
    t = jnp.dot(u, w1_ref[...].astype(BF16), preferred_element_type=F32)
    logit = jnp.dot(t.astype(BF16), w2_ref[...].astype(BF16), preferred_element_type=F32) + b_ref[...]
    log_sigmoid = jnp.minimum(logit, 0.0) - jnp.log(1.0 + jnp.exp(-jnp.abs(logit)))
    return log_sigmoid * (1.0 / GLA_GATE_NORM)


def _gla_sample_update(q_ref, k_ref, v_ref, r_ref, g, s0_ref, on_ref, s_ref):
    bb, dk = q_ref.shape[1:]
    alpha_t = jnp.exp(g).T
    k_t = k_ref[0].astype(F32).T
    q_t = (q_ref[0].astype(F32) * dk ** -0.5).T
    v = v_ref[0].astype(F32)
    rows = []
    for b in range(bb):
        s_new = alpha_t[:, b:b + 1] * s0_ref[b, 0] + k_t[:, b:b + 1] * v[b:b + 1, :]
        s_ref[b, 0] = s_new
        rows.append(jnp.sum(q_t[:, b:b + 1] * s_new, axis=0, keepdims=True))
    o = jnp.concatenate(rows, axis=0)
    o = _rms_scale(o) * on_ref[...]
    return o * r_ref[0].astype(F32)


def _gla_body(q_ref, k_ref, v_ref, r_ref, u_ref, on_ref, wg1_ref, wg2_ref, bg_ref,
              sq_ref, sk_ref, sv_ref, sr_ref, su_ref, s0_ref, o_ref, s_ref, ss_ref, *,
              heads, nt, per_tile):
    t = pl.program_id(1)
    tb = q_ref.shape[1]
    tp = per_tile * tb
    c = GLA_CHUNK
    dk = q_ref.shape[2] // heads
    dv = v_ref.shape[2] // heads

    @pl.when(t == 0)
    def _():
        s_ref[...] = jnp.zeros_like(s_ref)

    blk = pl.program_id(0) * nt + t
    head_s = blk % heads
    g_s_all = _gla_log_decay(su_ref[0], wg1_ref, wg2_ref, bg_ref)
    g_s = jnp.zeros((g_s_all.shape[0], dk), F32)
    for h in range(heads):
        g_s = jnp.where(head_s == h, g_s_all[:, h * dk:(h + 1) * dk], g_s)
    o_s = _gla_sample_update(sq_ref, sk_ref, sv_ref, sr_ref, g_s, s0_ref, on_ref, ss_ref)
    o_s = o_s.astype(o_ref.dtype)

    row = lax.broadcasted_iota(jnp.int32, (tb, tb), 0)
    col = lax.broadcasted_iota(jnp.int32, (tb, tb), 1)
    same_chunk_causal = ((row >= col) & ((row ^ col) < c)).astype(BF16)
    g_rest = _gla_log_decay(u_ref[0], wg1_ref, wg2_ref, bg_ref)
    bcum = jnp.zeros(g_rest.shape, F32)
    for _ in range(3):
        g_part = g_rest.astype(BF16)
        bcum = bcum + jnp.dot(same_chunk_causal, g_part, preferred_element_type=F32)
        g_rest = g_rest - g_part.astype(F32)
    nc = tb // c
    chunk_rows = [slice(ci * c, (ci + 1) * c) for ci in range(nc)]
    q = q_ref[0].astype(F32) * dk ** -0.5
    k = k_ref[0].astype(F32)
    q_i = q * jnp.exp(bcum)
    k_i = (k * jnp.exp(-bcum)).astype(BF16)
    g_tot = [bcum[(ci + 1) * c - 1:(ci + 1) * c, :] for ci in range(nc)]
    g_before = [jnp.zeros_like(g_tot[0])]
    for ci in range(nc):
        g_before.append(g_before[-1] + g_tot[ci])
    k_e = [k[rows, :] * jnp.exp(g_tot[ci] - bcum[rows, :]) for ci, rows in enumerate(chunk_rows)]
    q_in = jnp.concatenate([(q_i[rows, :] * jnp.exp(g_before[ci])).astype(BF16)
                            for ci, rows in enumerate(chunk_rows)], axis=0)
    k_out = jnp.concatenate([(k_e[ci] * jnp.exp(g_before[nc] - g_before[ci + 1])).astype(BF16)
                             for ci in range(nc)], axis=0)
    decay_t = jnp.exp(jnp.broadcast_to(g_before[nc], (SUBLANES, heads * dk))).T
    q_i = q_i.astype(BF16)

    out_base = (blk % per_tile) * tb
    o_inter = []
    for h in range(heads):
        kcols = slice(h * dk, (h + 1) * dk)
        vcols = slice(h * dv, (h + 1) * dv)
        s = s_ref[0, h]
        o_inter.append(jnp.dot(q_in[:, kcols], s.astype(BF16), preferred_element_type=F32))
        s_ref[0, h] = decay_t[kcols, 0:1] * s + lax.dot_general(
            k_out[:, kcols], v_ref[0, :, vcols], (((0,), (0,)), ((), ())), preferred_element_type=F32)

    for ci, rows in enumerate(chunk_rows):
        out_rows = pl.ds(pl.multiple_of(out_base + ci * c, c), c)
        n_keys = (ci + 1) * c
        keys = jnp.concatenate(
            [(k_e[cj] * jnp.exp(g_before[ci] - g_before[cj + 1])).astype(BF16) for cj in range(ci)]
            + [k_i[rows, :]], axis=0)
        visible = (lax.broadcasted_iota(jnp.int32, (c, n_keys), 1)
                   <= lax.broadcasted_iota(jnp.int32, (c, n_keys), 0) + ci * c)
        for h in range(heads):
            kcols = slice(h * dk, (h + 1) * dk)
            vcols = slice(h * dv, (h + 1) * dv)
            scores = lax.dot_general(q_i[rows, kcols], keys[:, kcols], (((1,), (1,)), ((), ())),
                                     preferred_element_type=F32)
            scores = jnp.where(visible, scores, 0.0).astype(BF16)
            o = o_inter[h][rows, :] + jnp.dot(scores, v_ref[0, 0:n_keys, vcols],
                                              preferred_element_type=F32)
            o = _rms_scale(o) * on_ref[...]
            o_ref[0, out_rows, vcols] = (o * r_ref[0, rows, vcols].astype(F32)).astype(o_ref.dtype)

    for h in range(heads):
        @pl.when(head_s == h)
        def _(h=h):
            o_ref[0, tp:, h * dv:(h + 1) * dv] = o_s


def _gla(qkvr, u, layer, gate_params, s0, onorm, *, batch, seq, heads, dk, dv, tp, ts, tb=256):
    tm = tp + ts
    nt = seq // tb
    per_tile = tp // tb
    s_blk = tp // ts
    hk, hv = heads * dk, heads * dv
    assert hv == 2 * hk
    assert batch * nt == N_TOKEN_TILES * heads and per_tile == heads
    qkvr3 = qkvr.reshape(N_TOKEN_TILES, tm, qkvr.shape[1])
    d = u.shape[1]
    u3 = u.reshape(N_TOKEN_TILES, tm, d)
    resident = lambda a: pl.BlockSpec(a.shape, lambda b, t: (0,) * a.ndim)
    kq, vq, rq = heads, (2 * hk) // dv, (2 * hk) // dv + heads

    def tok(col):
        def index(b, t):
            blk = b * nt + t
            return (blk // per_tile, blk % per_tile, col)
        return index

    def smp(col0):
        def index(b, t):
            blk = b * nt + t
            return (blk // heads, s_blk, col0 + blk % heads)
        return index

    state = lambda b, t: ((b * nt + t) // heads, (b * nt + t) % heads, 0, 0)
    on_arg, on_spec = _layer_vector(onorm, layer)
    return pl.pallas_call(
        functools.partial(_gla_body, heads=heads, nt=nt, per_tile=per_tile),
        grid=(batch, nt),
        in_specs=[
            pl.BlockSpec((1, tb, hk), tok(0)),
            pl.BlockSpec((1, tb, hk), tok(1)),
            pl.BlockSpec((1, tb, hv), tok(1)),
            pl.BlockSpec((1, tb, hv), tok(2)),
            pl.BlockSpec((1, tb, d), tok(0)),
            on_spec,
            *[resident(a) for a in gate_params],
            pl.BlockSpec((1, ts, dk), smp(0)),
            pl.BlockSpec((1, ts, dk), smp(kq)),
            pl.BlockSpec((1, ts, dv), smp(vq)),
            pl.BlockSpec((1, ts, dv), smp(rq)),
            pl.BlockSpec((1, ts, d), lambda b, t: ((b * nt + t) // heads, s_blk, 0)),
            _layer_operand(s0, layer, (ts, 1, dk, dv), state)[1],
        ],
        out_specs=[
            pl.BlockSpec((1, tm, hv), lambda b, t: ((b * nt + t) // per_tile, 0, 0)),
            pl.BlockSpec((1, heads, dk, dv), lambda b, t: (b, 0, 0, 0)),
            pl.BlockSpec((ts, 1, dk, dv), state),
        ],
        out_shape=[
            jax.ShapeDtypeStruct((N_TOKEN_TILES, tm, hv), BF16),
            jax.ShapeDtypeStruct((batch, heads, dk, dv), F32),
            jax.ShapeDtypeStruct(s0.shape[1:], F32),
        ],
        compiler_params=_cparams("arbitrary", "arbitrary"),
        name="gla",
    )(qkvr3, qkvr3, qkvr3, qkvr3, u3, on_arg, *gate_params, qkvr3, qkvr3, qkvr3, qkvr3, u3, s0)


def kernel(x_prompt, x_sample, state_rglru_h, state_rglru_conv, state_gla_S, ln_ffn1, ffn1_w1, ffn1_w3, ffn1_w2, ln_mix, ln_ffn2, ffn2_w1, ffn2_w3, ffn2_w2, rg_w_y, rg_w_x, rg_conv_w, rg_conv_b, rg_w_a, rg_b_a, rg_w_i, rg_b_i, rg_lambda, rg_w_o, gla_w_q, gla_w_k, gla_w_v, gla_w_g1, gla_w_g2, gla_b_g, gla_w_r, gla_onorm, gla_w_o, ln_final):
    batch, seq, d = x_prompt.shape
    n_dec = x_sample.shape[0]
    depth = ln_ffn1.shape[0]
    heads, dk, dv = state_gla_S.shape[2:]
    tp, ts = _token_tiles(batch * seq, n_dec)
    tm = tp + ts
    tiles = dict(tp=tp, ts=ts)

    x = (x_prompt.reshape(batch * seq, d), x_sample.reshape(n_dec, d))

    ffn_f32 = []
    for i in range(depth):
        ffn_f32 += [((ffn1_w1, ffn1_w3, ffn1_w2), i), ((ffn2_w1, ffn2_w3, ffn2_w2), i)]
    ffn_f32.append(None)
    ffn_w = tuple(_to_bf16(w, 0) for w in ffn_f32[0][0])

    p_h, p_conv, p_s, s_h, s_conv, s_s = [], [], [], [], [], []
    for i in range(depth):
        j = i // 2
        x, u, ffn_w = _ffn(x, i, ln_ffn1, ffn_w, g_mix=ln_mix, next_weights=ffn_f32[2 * i + 1], **tiles)
        if i % 2 == 0:
            proj = _matmul(u, j, [rg_w_y, rg_w_x], [_gelu_tanh, _identity], tm=tm)
            params = _rg_params(j, rg_conv_w, rg_conv_b, rg_w_a, rg_b_a, rg_w_i, rg_b_i, rg_lambda)
            mix, hl, cl = _rg_prompt(proj, params, batch=batch, seq=seq, **tiles)
            mix, hn, cn = _rg_sample(mix, proj, j, state_rglru_conv, state_rglru_h, params, **tiles)
            p_h.append(hl.reshape(batch, d))
            p_conv.append(cl)
            s_h.append(hn)
            s_conv.append(cn.reshape(n_dec, CONV_W - 1, d))
            w_o = rg_w_o
        else:
            qkvr = _matmul(u, j, [gla_w_q, gla_w_k, gla_w_v, gla_w_r],
                           [_identity, _identity, _identity, _silu], tm=tm, out_dtype=BF16)
            rank = gla_w_g1.shape[2]
            gate_params = (jnp.pad(gla_w_g1[j], ((0, 0), (0, LANES - rank))).astype(BF16),
                           jnp.pad(gla_w_g2[j], ((0, LANES - rank), (0, 0))).astype(BF16),
                           gla_b_g[j].reshape(1, heads * dk))
            mix, s_fin, s_new = _gla(qkvr, u, j, gate_params, state_gla_S, gla_onorm, batch=batch,
                                     seq=seq, heads=heads, dk=dk, dv=dv, **tiles)
            p_s.append(s_fin)
            s_s.append(s_new)
            w_o = gla_w_o
        x = _matmul(mix.reshape(N_TOKEN_TILES * tm, d), j, [w_o], [_identity], res=x, tm=tm)
        g_final = ln_final.reshape(1, d) if i == depth - 1 else None
        tf = 512 if g_final is None else 256
        x, _, ffn_w = _ffn(x, i, ln_ffn2, ffn_w, g_final=g_final, next_weights=ffn_f32[2 * i + 2], tf=tf,
                           **tiles)

    y_prompt = x[0].reshape(batch, seq, d)
    y_sample = x[1].reshape(n_dec, 1, d)
    return (y_prompt, y_sample, jnp.stack(p_h), jnp.stack(p_conv), jnp.stack(p_s),
            jnp.stack(s_h), jnp.stack(s_conv), jnp.stack(s_s))
```

```python
import functools

import jax
import jax.numpy as jnp
from jax import lax
from jax.experimental import pallas as pl
from jax.experimental.pallas import tpu as pltpu

F32 = jnp.float32
BF16 = jnp.bfloat16

RMS_EPS = 1e-6
RG_C = 8.0
CONV_W = 4
GLA_GATE_NORM = 16.0
GLA_CHUNK = 64

V7X_VMEM_LIMIT_BYTES = 63 * 1024 * 1024
SUBLANES = 8
LANES = 128
BF16_SUBLANES = 16
N_TOKEN_TILES = 8


def _cparams(*sem):
    return pltpu.CompilerParams(dimension_semantics=sem, vmem_limit_bytes=V7X_VMEM_LIMIT_BYTES)


def _rms_scale(x):
    return x * lax.rsqrt(jnp.mean(x * x, axis=-1, keepdims=True) + RMS_EPS)


def _silu(x):
    return x * jax.nn.sigmoid(x)


def _gelu_tanh(y):
    return jax.nn.gelu(y, approximate=True)


def _identity(y):
    return y


def _softplus(x):
    return jnp.maximum(x, 0.0) + jnp.log1p(jnp.exp(-jnp.abs(x)))


FFN_FIRST_PIECE = 512
FFN_LAST_PIECE = 512


def _layer_operand(stacked, layer, block, index_map):
    spec = pl.BlockSpec((None,) + tuple(block), lambda *g: (layer,) + tuple(index_map(*g)))
    return stacked, spec


def _layer_vector(stacked, layer):
    n_layers, n = stacked.shape
    return _layer_operand(stacked.reshape(n_layers, 1, n), layer, (1, n), lambda *g: (0, 0))


def _layer_full(stacked, layer):
    block = stacked.shape[1:]
    return _layer_operand(stacked, layer, block, lambda *g: (0,) * len(block))


def _token_tiles(n_prompt, n_sample):
    tp, ts = n_prompt // N_TOKEN_TILES, n_sample // N_TOKEN_TILES
    assert tp * N_TOKEN_TILES == n_prompt and ts * N_TOKEN_TILES == n_sample
    assert tp % BF16_SUBLANES == 0 and ts % BF16_SUBLANES == 0 and tp % ts == 0
    return tp, ts


def _cast_body(w_ref, o_ref):
    o_ref[...] = w_ref[...].astype(o_ref.dtype)


def _to_bf16(w, layer, *, rows=512):
    _, r, c = w.shape
    return pl.pallas_call(
        _cast_body,
        grid=(r // rows,),
        in_specs=[pl.BlockSpec((None, rows, c), lambda i: (layer, i, 0))],
        out_specs=pl.BlockSpec((rows, c), lambda i: (i, 0)),
        out_shape=jax.ShapeDtypeStruct((r, c), BF16),
        compiler_params=_cparams("arbitrary"),
        name="to_bf16",
    )(w)


def _ffn_body(*refs, tp, ts, split_in, split_out, mix_norm, cast_next):
    refs = list(refs)
    x_refs = [refs.pop(0) for _ in range(2 if split_in else 1)]
    g_ref, w1_ref, w3_ref, w2_ref = (refs.pop(0) for _ in range(4))
    g2_ref = refs.pop(0) if (split_out or mix_norm) else None
    next_f32 = [refs.pop(0) for _ in range(3 if cast_next else 0)]
    o_refs = [refs.pop(0) for _ in range(2 if split_out else 1)]
    u_hbm = refs.pop(0) if mix_norm else None
    next_bf16 = [refs.pop(0) for _ in range(3 if cast_next else 0)]
    xn_ref = refs.pop(0)
    acc_ref = refs.pop(0) if split_out else o_refs[0]
    (sem,) = refs
    tm = tp + ts
    i = pl.program_id(0)
    f = pl.program_id(1)

    n_f = pl.num_programs(1)

    def segments(blocks):
        if len(blocks) == 2:
            return [(0, tp, blocks[0]), (tp, ts, blocks[1])]
        return [(0, tm, blocks[0])]

    def parts(blocks, r0, r1):
        out = []
        for t0, n_rows, ref in segments(blocks):
            lo, hi = max(r0, t0), min(r1, t0 + n_rows)
            if lo < hi:
                out.append((ref, slice(lo - t0, hi - t0)))
        return out

    def read_rows(blocks, r0, r1):
        vals = [ref[rows, :] for ref, rows in parts(blocks, r0, r1)]
        return vals[0] if len(vals) == 1 else jnp.concatenate(vals, axis=0)

    def write_rows(blocks, r0, r1, value):
        off = 0
        for ref, rows in parts(blocks, r0, r1):
            n = rows.stop - rows.start
            ref[rows, :] = value[off:off + n]
            off += n

    def ffn_product(xn):
        h1 = jnp.dot(xn, w1_ref[...].astype(BF16), preferred_element_type=F32)
        h3 = jnp.dot(xn, w3_ref[...].astype(BF16), preferred_element_type=F32)
        h = (_silu(h1) * h3).astype(BF16)
        return jnp.dot(h, w2_ref[...].astype(BF16), preferred_element_type=F32)

    assert 0 < FFN_FIRST_PIECE < tm and 0 < FFN_LAST_PIECE < tm
    assert FFN_FIRST_PIECE % BF16_SUBLANES == 0 and FFN_LAST_PIECE % BF16_SUBLANES == 0
    pieces = [(0, FFN_FIRST_PIECE), (FFN_FIRST_PIECE, tm)]
    pieces_rev = [(0, tm - FFN_LAST_PIECE), (tm - FFN_LAST_PIECE, tm)]

    def cast_next_block():
        for src_ref, dst_ref in zip(next_f32, next_bf16):
            dst_ref[...] = src_ref[...].astype(BF16)

    @pl.when(f == 0)
    def _():
        for r0, r1 in pieces:
            x = read_rows(x_refs, r0, r1)
            xn = (_rms_scale(x) * g_ref[...]).astype(BF16)
            xn_ref[r0:r1, :] = xn
            acc_ref[r0:r1, :] = 2.0 * x + ffn_product(xn)
        cast_next_block()

    @pl.when((f > 0) & (f < n_f - 1))
    def _():
        acc_ref[...] += ffn_product(xn_ref[...])
        cast_next_block()

    @pl.when(f == n_f - 1)
    def _():
        cast_next_block()
        for r0, r1 in pieces_rev:
            y = 0.5 * (acc_ref[r0:r1, :] + ffn_product(xn_ref[r0:r1, :]))
            if split_out:
                y = _rms_scale(y) * g2_ref[...]
            write_rows(o_refs, r0, r1, y)
            if mix_norm:
                xn_ref[r0:r1, :] = (_rms_scale(y) * g2_ref[...]).astype(BF16)
        if mix_norm:
            cu = pltpu.make_async_copy(xn_ref, u_hbm.at[pl.ds(i * tm, tm)], sem.at[0])
            cu.start()
            cu.wait()


def _ffn(x, layer, g, weights, *, tp, ts, g_mix=None, g_final=None, next_weights=None, tf=512):
    split_in = isinstance(x, tuple)
    split_out = g_final is not None
    mix_norm = g_mix is not None
    cast_next = next_weights is not None
    assert not (split_out and mix_norm)
    xs = list(x) if split_in else [x]
    w1, w3, w2 = weights
    d, f = w1.shape
    tm = tp + ts
    nf = f // tf
    operands = [
        _layer_vector(g, layer),
        (w1, pl.BlockSpec((d, tf), lambda i, j: (0, j))),
        (w3, pl.BlockSpec((d, tf), lambda i, j: (0, j))),
        (w2, pl.BlockSpec((tf, d), lambda i, j: (j, 0))),
    ]
    if mix_norm:
        operands.append(_layer_vector(g_mix, layer))
    if split_out:
        operands.append(_layer_vector(g_final, 0))
    row_tile = lambda rows: pl.BlockSpec((rows, d), lambda i, j: (i, 0))
    x_specs = [row_tile(tp), row_tile(ts)] if split_in else [row_tile(tm)]
    if split_out:
        out_specs = [row_tile(tp), row_tile(ts)]
        out_shape = [jax.ShapeDtypeStruct((tp * N_TOKEN_TILES, d), F32),
                     jax.ShapeDtypeStruct((ts * N_TOKEN_TILES, d), F32)]
    else:
        out_specs = [row_tile(tm)]
        out_shape = [jax.ShapeDtypeStruct((tm * N_TOKEN_TILES, d), F32)]
    if mix_norm:
        out_specs.append(pl.BlockSpec(memory_space=pl.ANY))
        out_shape.append(jax.ShapeDtypeStruct((tm * N_TOKEN_TILES, d), BF16))
    if cast_next:
        (n1, n3, n2), next_layer = next_weights
        db = d // N_TOKEN_TILES
        assert n1.shape[1:] == (d, f) and n2.shape[1:] == (f, d) and db * N_TOKEN_TILES == d
        operands += [
            _layer_operand(n1, next_layer, (db, tf), lambda i, j: (i, j)),
            _layer_operand(n3, next_layer, (db, tf), lambda i, j: (i, j)),
            _layer_operand(n2, next_layer, (tf, db), lambda i, j: (j, i)),
        ]
        out_specs += [pl.BlockSpec((db, tf), lambda i, j: (i, j)), pl.BlockSpec((db, tf), lambda i, j: (i, j)),
                      pl.BlockSpec((tf, db), lambda i, j: (j, i))]
        out_shape += [jax.ShapeDtypeStruct((d, f), BF16), jax.ShapeDtypeStruct((d, f), BF16),
                      jax.ShapeDtypeStruct((f, d), BF16)]
    out = pl.pallas_call(
        functools.partial(_ffn_body, tp=tp, ts=ts, split_in=split_in, split_out=split_out,
                          mix_norm=mix_norm, cast_next=cast_next),
        grid=(N_TOKEN_TILES, nf),
        in_specs=x_specs + [spec for _, spec in operands],
        out_specs=out_specs,
        out_shape=out_shape,
        scratch_shapes=[pltpu.VMEM((tm, d), BF16)] + ([pltpu.VMEM((tm, d), F32)] if split_out else [])
        + [pltpu.SemaphoreType.DMA((1,))],
        compiler_params=_cparams("arbitrary", "arbitrary"),
        name="ffn",
    )(*xs, *[arr for arr, _ in operands])
    out = list(out)
    acts = tuple(out[:2]) if split_out else out[0]
    rest = out[2:] if split_out else out[1:]
    u = rest.pop(0) if mix_norm else None
    return acts, u, (tuple(rest) if cast_next else None)


def _matmul_body(u_ref, *refs, tiles, layer, has_res):
    n_src = 1 + max(src for src, _, _ in tiles)
    w_hbm = refs[:n_src]
    rest = refs[n_src:]
    if has_res:
        r_ref, o_ref, stage_ref, wb_ref, sem = rest
    else:
        o_ref, stage_ref, wb_ref, sem = rest
    tn = stage_ref.shape[1]
    j = pl.program_id(0)
    i = pl.program_id(1)

    def weight_copy(jj):
        src, col0, _ = tiles[jj]
        return pltpu.make_async_copy(w_hbm[src].at[layer, :, pl.ds(col0, tn)], stage_ref, sem.at[0])

    def at_tile(jj, fn):
        pl.when(j == jj)(fn)

    @pl.when(i == 0)
    def _():
        at_tile(0, lambda: weight_copy(0).start())
        for jj in range(len(tiles)):
            at_tile(jj, lambda jj=jj: weight_copy(jj).wait())
        wb_ref[...] = stage_ref[...].astype(BF16)
        for jj in range(len(tiles) - 1):
            at_tile(jj, lambda jj=jj: weight_copy(jj + 1).start())

    for act in dict.fromkeys(act for _, _, act in tiles):
        uses = functools.reduce(jnp.logical_or, [j == jj for jj, t in enumerate(tiles) if t[2] is act])

        @pl.when(uses)
        def _(act=act):
            y = act(jnp.dot(u_ref[...], wb_ref[...], preferred_element_type=F32))
            if has_res:
                y = r_ref[...] + y
            o_ref[...] = y.astype(o_ref.dtype)


def _matmul(u, layer, ws, acts, res=None, *, tm, tn=1024, out_dtype=F32):
    m, d = u.shape
    tiles = []
    for src, (w, act) in enumerate(zip(ws, acts)):
        assert w.shape[2] % tn == 0
        tiles += [(src, col0, act) for col0 in range(0, w.shape[2], tn)]
    n = len(tiles) * tn
    io_spec = lambda: pl.BlockSpec((tm, tn), lambda j, i: (i, j))
    has_res = res is not None
    return pl.pallas_call(
        functools.partial(_matmul_body, tiles=tuple(tiles), layer=layer, has_res=has_res),
        grid=(len(tiles), m // tm),
        in_specs=[pl.BlockSpec((tm, d), lambda j, i: (i, 0))]
        + [pl.BlockSpec(memory_space=pl.ANY)] * len(ws) + ([io_spec()] if has_res else []),
        out_specs=io_spec(),
        out_shape=jax.ShapeDtypeStruct((m, n), out_dtype),
        scratch_shapes=[pltpu.VMEM((d, tn), F32), pltpu.VMEM((d, tn), BF16),
                        pltpu.SemaphoreType.DMA((1,))],
        compiler_params=_cparams("arbitrary", "arbitrary"),
        name="matmul",
    )(u, *ws, *([res] if has_res else []))


def _prompt_rows_of_tile(o_ref, nt, per_tile, tb):
    blk = pl.program_id(0) * nt + pl.program_id(1)
    k = blk % per_tile
    tp = per_tile * tb

    @pl.when(k == 0)
    def _():
        o_ref[0, tp:, :] = jnp.zeros((o_ref.shape[1] - tp, o_ref.shape[2]), o_ref.dtype)

    return pl.ds(pl.multiple_of(k * tb, tb), tb)


def _sigmoid(x):
    return 0.5 * jnp.tanh(0.5 * x) + 0.5


def _rglru_coeffs(conv_out, wa_ref, ba_ref, wi_ref, bi_ref, lam_ref, store):
    n_blocks, bw, _ = wa_ref.shape
    for n in range(n_blocks):
        cols = slice(n * bw, (n + 1) * bw)
        xb = conv_out(cols)
        xb16 = xb.astype(BF16)
        wa = wa_ref[n].astype(BF16)
        wi = wi_ref[n].astype(BF16)
        r = _sigmoid(jnp.dot(xb16, wa, preferred_element_type=F32) + ba_ref[:, cols])
        i = _sigmoid(jnp.dot(xb16, wi, preferred_element_type=F32) + bi_ref[:, cols])
        log_a = -RG_C * r * _softplus(-lam_ref[:, cols])
        a = jnp.exp(log_a)
        b = jnp.sqrt(1.0 - a * a) * (i * xb)
        store(n, a, b)


def _rg_prompt_body(xw_ref, gate_ref, cw_ref, cb_ref, wa_ref, ba_ref, wi_ref, bi_ref, lam_ref,
                    o_ref, hl_ref, cl_ref, ext_ref, a_ref, b_ref, h_ref, *, nt, per_tile):
    t = pl.program_id(1)
    tb = xw_ref.shape[1]
    bw = wa_ref.shape[1]
    pad = SUBLANES

    @pl.when(t == 0)
    def _():
        ext_ref[0:pad, :] = jnp.zeros((pad, ext_ref.shape[1]), F32)
        h_ref[...] = jnp.zeros_like(h_ref)

    ext_ref[pad:pad + tb, :] = xw_ref[0]

    def conv_out(cols):
        ext = ext_ref[:, cols]
        y = ext * cw_ref[0:1, cols]
        for j in range(1, CONV_W):
            y = ext * cw_ref[j:j + 1, cols] + pltpu.roll(y, 1, axis=0)
        return cb_ref[:, cols] + y[pad:, :]

    seg = tb // SUBLANES
    pitch = a_ref.shape[1] // SUBLANES
    n_slabs = a_ref.shape[0]

    def store(n, a, b):
        for half in range(bw // LANES):
            lanes = slice(half * LANES, (half + 1) * LANES)
            for s in range(SUBLANES):
                dst = slice(s * pitch, s * pitch + seg)
                a_ref[n * (bw // LANES) + half, dst, :] = a[s * seg:(s + 1) * seg, lanes]
                b_ref[n * (bw // LANES) + half, dst, :] = b[s * seg:(s + 1) * seg, lanes]

    _rglru_coeffs(conv_out, wa_ref, ba_ref, wi_ref, bi_ref, lam_ref, store)

    def step(v, carry):
        hs, ps = carry
        rows = pl.ds(v, SUBLANES, stride=pitch)
        new_h, new_p = [], []
        for slab in range(n_slabs):
            a = a_ref[slab, rows, :]
            h = a * hs[slab] + b_ref[slab, rows, :]
            p = a * ps[slab]
            b_ref[slab, rows, :] = h
            a_ref[slab, rows, :] = p
            new_h.append(h)
            new_p.append(p)
        return tuple(new_h), tuple(new_p)

    zeros = tuple(jnp.zeros((SUBLANES, LANES), F32) for _ in range(n_slabs))
    ones = tuple(jnp.ones((SUBLANES, LANES), F32) for _ in range(n_slabs))
    lax.fori_loop(0, seg, step, (zeros, ones))

    out_rows = _prompt_rows_of_tile(o_ref, nt, per_tile, tb)
    for slab in range(n_slabs):
        lanes = slice(slab * LANES, (slab + 1) * LANES)
        h_in = h_ref[0:1, lanes]
        for s in range(SUBLANES):
            src = slice(s * pitch, s * pitch + seg)
            h = b_ref[slab, src, :] + a_ref[slab, src, :] * h_in
            dst = pl.ds(pl.multiple_of(out_rows.start + s * seg, seg), seg)
            o_ref[0, dst, lanes] = (h * gate_ref[0, s * seg:(s + 1) * seg, lanes]).astype(o_ref.dtype)
            h_in = h[seg - 1:seg, :]
        h_ref[0:1, lanes] = h_in
        hl_ref[0, :, lanes] = h_in
    cl_ref[0] = ext_ref[pad + tb - (CONV_W - 1):pad + tb, :]
    ext_ref[0:pad, :] = ext_ref[tb:tb + pad, :]


def _segment_pitch(seg):
    pitch = -(-seg // SUBLANES) * SUBLANES
    while (pitch // SUBLANES) % 2 == 0:
        pitch += SUBLANES
    return pitch


def _rg_params(layer, conv_w, conv_b, w_a, b_a, w_i, b_i, lam):
    operands = [_layer_full(conv_w, layer), _layer_vector(conv_b, layer), _layer_full(w_a, layer),
                _layer_vector(b_a, layer), _layer_full(w_i, layer), _layer_vector(b_i, layer),
                _layer_vector(lam, layer)]
    return [arr for arr, _ in operands], [spec for _, spec in operands]


def _rg_prompt(proj, params, *, batch, seq, tp, ts, tb=256):
    p_args, p_specs = params
    m, d2 = proj.shape
    d = d2 // 2
    tm = tp + ts
    nt = seq // tb
    per_tile = tp // tb
    proj3 = proj.reshape(N_TOKEN_TILES, tm, d2)

    def tok(col):
        def index(b, t):
            blk = b * nt + t
            return (blk // per_tile, blk % per_tile, col)
        return index

    mix, hl, cl = pl.pallas_call(
        functools.partial(_rg_prompt_body, nt=nt, per_tile=per_tile),
        grid=(batch, nt),
        in_specs=[pl.BlockSpec((1, tb, d), tok(1)), pl.BlockSpec((1, tb, d), tok(0))] + p_specs,
        out_specs=[
            pl.BlockSpec((1, tm, d), lambda b, t: ((b * nt + t) // per_tile, 0, 0)),
            pl.BlockSpec((1, 1, d), lambda b, t: (b, 0, 0)),
            pl.BlockSpec((1, CONV_W - 1, d), lambda b, t: (b, 0, 0)),
        ],
        out_shape=[
            jax.ShapeDtypeStruct((N_TOKEN_TILES, tm, d), BF16),
            jax.ShapeDtypeStruct((batch, 1, d), F32),
            jax.ShapeDtypeStruct((batch, CONV_W - 1, d), F32),
        ],
        scratch_shapes=[
            pltpu.VMEM((tb + SUBLANES, d), F32),
            pltpu.VMEM((d // LANES, SUBLANES * _segment_pitch(tb // SUBLANES), LANES), F32),
            pltpu.VMEM((d // LANES, SUBLANES * _segment_pitch(tb // SUBLANES), LANES), F32),
            pltpu.VMEM((SUBLANES, d), F32),
        ],
        compiler_params=_cparams("arbitrary", "arbitrary"),
        name="rg_prompt",
    )(proj3, proj3, *p_args)
    return mix, hl, cl


def _rg_sample_body(mix_ref, xw_ref, gate_ref, conv_ref, h0_ref, cw_ref, cb_ref, wa_ref, ba_ref,
                    wi_ref, bi_ref, lam_ref, o_ref, hn_ref, cn_ref, a_ref, b_ref):
    del mix_ref
    n, d = h0_ref.shape
    bw = wa_ref.shape[1]
    x = xw_ref[...].reshape(n, d)
    xc = cb_ref[...]
    for j in range(CONV_W - 1):
        xc = xc + conv_ref[:, j * d:(j + 1) * d] * cw_ref[j:j + 1, :]
    xc = xc + x * cw_ref[CONV_W - 1:CONV_W, :]

    def store(k, a, b):
        a_ref[:, k * bw:(k + 1) * bw] = a
        b_ref[:, k * bw:(k + 1) * bw] = b

    _rglru_coeffs(lambda cols: xc[:, cols], wa_ref, ba_ref, wi_ref, bi_ref, lam_ref, store)
    h = b_ref[...] + a_ref[...] * h0_ref[...]
    hn_ref[...] = h
    o_ref[...] = (h * gate_ref[...].reshape(n, d)).reshape(o_ref.shape).astype(o_ref.dtype)
    cn_ref[:, 0:(CONV_W - 2) * d] = conv_ref[:, d:(CONV_W - 1) * d]
    cn_ref[:, (CONV_W - 2) * d:] = x


def _rg_sample(mix, proj, layer, conv_state, h0, params, *, tp, ts):
    p_args, p_specs = params
    n_layers, n, d = h0.shape
    tm = tp + ts
    proj3 = proj.reshape(N_TOKEN_TILES, tm, 2 * d)
    s_blk = tp // ts
    full2 = lambda shape: pl.BlockSpec(shape, lambda i: (0, 0))
    kc = (CONV_W - 1) * d
    conv_arg, conv_spec = _layer_full(conv_state.reshape(n_layers, n, kc), layer)
    h0_arg, h0_spec = _layer_full(h0, layer)
    return pl.pallas_call(
        _rg_sample_body,
        grid=(1,),
        in_specs=[
            pl.BlockSpec(memory_space=pl.ANY),
            pl.BlockSpec((N_TOKEN_TILES, ts, d), lambda i: (0, s_blk, 1)),
            pl.BlockSpec((N_TOKEN_TILES, ts, d), lambda i: (0, s_blk, 0)),
            conv_spec, h0_spec,
        ] + p_specs,
        out_specs=[
            pl.BlockSpec((N_TOKEN_TILES, ts, d), lambda i: (0, s_blk, 0)),
            full2((n, d)),
            full2((n, kc)),
        ],
        out_shape=[
            jax.ShapeDtypeStruct(mix.shape, mix.dtype),
            jax.ShapeDtypeStruct((n, d), F32),
            jax.ShapeDtypeStruct((n, kc), F32),
        ],
        scratch_shapes=[pltpu.VMEM((n, d), F32), pltpu.VMEM((n, d), F32)],
        input_output_aliases={0: 0},
        compiler_params=_cparams("arbitrary"),
        name="rg_sample",
    )(mix, proj3, proj3, conv_arg, h0_arg, *p_args)


def _gla_log_decay(u, w1_ref, w2_ref, b_ref):
    t = jnp.dot(u, w1_ref[...].astype(BF16), preferred_element_type=F32)
    logit = jnp.dot(t.astype(BF16), w2_ref[...].astype(BF16), preferred_element_type=F32) + b_ref[...]
    log_sigmoid = jnp.minimum(logit, 0.0) - jnp.log(1.0 + jnp.exp(-jnp.abs(logit)))
    return log_sigmoid * (1.0 / GLA_GATE_NORM)


def _gla_sample_update(q_ref, k_ref, v_ref, r_ref, g, s0_ref, on_ref, s_ref):
    bb, dk = q_ref.shape[1:]
    alpha_t = jnp.exp(g).T
    k_t = k_ref[0].astype(F32).T
    q_t = (q_ref[0].astype(F32) * dk ** -0.5).T
    v = v_ref[0].astype(F32)
    rows = []
    for b in range(bb):
        s_new = alpha_t[:, b:b + 1] * s0_ref[b, 0] + k_t[:, b:b + 1] * v[b:b + 1, :]
        s_ref[b, 0] = s_new
        rows.append(jnp.sum(q_t[:, b:b + 1] * s_new, axis=0, keepdims=True))
    o = jnp.concatenate(rows, axis=0)
    o = _rms_scale(o) * on_ref[...]
    return o * r_ref[0].astype(F32)


def _gla_body(q_ref, k_ref, v_ref, r_ref, u_ref, on_ref, wg1_ref, wg2_ref, bg_ref,
              sq_ref, sk_ref, sv_ref, sr_ref, su_ref, s0_ref, o_ref, s_ref, ss_ref, *,
              heads, nt, per_tile):
    t = pl.program_id(1)
    tb = q_ref.shape[1]
    tp = per_tile * tb
    c = GLA_CHUNK
    dk = q_ref.shape[2] // heads
    dv = v_ref.shape[2] // heads

    @pl.when(t == 0)
    def _():
        s_ref[...] = jnp.zeros_like(s_ref)

    blk = pl.program_id(0) * nt + t
    head_s = blk % heads
    g_s_all = _gla_log_decay(su_ref[0], wg1_ref, wg2_ref, bg_ref)
    g_s = jnp.zeros((g_s_all.shape[0], dk), F32)
    for h in range(heads):
        g_s = jnp.where(head_s == h, g_s_all[:, h * dk:(h + 1) * dk], g_s)
    o_s = _gla_sample_update(sq_ref, sk_ref, sv_ref, sr_ref, g_s, s0_ref, on_ref, ss_ref)
    o_s = o_s.astype(o_ref.dtype)

    row = lax.broadcasted_iota(jnp.int32, (tb, tb), 0)
    col = lax.broadcasted_iota(jnp.int32, (tb, tb), 1)
    same_chunk_causal = ((row >= col) & ((row ^ col) < c)).astype(BF16)
    g_rest = _gla_log_decay(u_ref[0], wg1_ref, wg2_ref, bg_ref)
    bcum = jnp.zeros(g_rest.shape, F32)
    for _ in range(3):
        g_part = g_rest.astype(BF16)
        bcum = bcum + jnp.dot(same_chunk_causal, g_part, preferred_element_type=F32)
        g_rest = g_rest - g_part.astype(F32)
    nc = tb // c
    chunk_rows = [slice(ci * c, (ci + 1) * c) for ci in range(nc)]
    q = q_ref[0].astype(F32) * dk ** -0.5
    k = k_ref[0].astype(F32)
    q_i = q * jnp.exp(bcum)
    k_i = (k * jnp.exp(-bcum)).astype(BF16)
    g_tot = [bcum[(ci + 1) * c - 1:(ci + 1) * c, :] for ci in range(nc)]
    g_before = [jnp.zeros_like(g_tot[0])]
    for ci in range(nc):
        g_before.append(g_before[-1] + g_tot[ci])
    k_e = [k[rows, :] * jnp.exp(g_tot[ci] - bcum[rows, :]) for ci, rows in enumerate(chunk_rows)]
    q_in = jnp.concatenate([(q_i[rows, :] * jnp.exp(g_before[ci])).astype(BF16)
                            for ci, rows in enumerate(chunk_rows)], axis=0)
    k_out = jnp.concatenate([(k_e[ci] * jnp.exp(g_before[nc] - g_before[ci + 1])).astype(BF16)
                             for ci in range(nc)], axis=0)
    decay_t = jnp.exp(jnp.broadcast_to(g_before[nc], (SUBLANES, heads * dk))).T
    q_i = q_i.astype(BF16)

    out_base = (blk % per_tile) * tb
    o_inter = []
    for h in range(heads):
        kcols = slice(h * dk, (h + 1) * dk)
        vcols = slice(h * dv, (h + 1) * dv)
        s = s_ref[0, h]
        o_inter.append(jnp.dot(q_in[:, kcols], s.astype(BF16), preferred_element_type=F32))
        s_ref[0, h] = decay_t[kcols, 0:1] * s + lax.dot_general(
            k_out[:, kcols], v_ref[0, :, vcols], (((0,), (0,)), ((), ())), preferred_element_type=F32)

    for ci, rows in enumerate(chunk_rows):
        out_rows = pl.ds(pl.multiple_of(out_base + ci * c, c), c)
        n_keys = (ci + 1) * c
        keys = jnp.concatenate(
            [(k_e[cj] * jnp.exp(g_before[ci] - g_before[cj + 1])).astype(BF16) for cj in range(ci)]
            + [k_i[rows, :]], axis=0)
        visible = (lax.broadcasted_iota(jnp.int32, (c, n_keys), 1)
                   <= lax.broadcasted_iota(jnp.int32, (c, n_keys), 0) + ci * c)
        for h in range(heads):
            kcols = slice(h * dk, (h + 1) * dk)
            vcols = slice(h * dv, (h + 1) * dv)
            scores = lax.dot_general(q_i[rows, kcols], keys[:, kcols], (((1,), (1,)), ((), ())),
                                     preferred_element_type=F32)
            scores = jnp.where(visible, scores, 0.0).astype(BF16)
            o = o_inter[h][rows, :] + jnp.dot(scores, v_ref[0, 0:n_keys, vcols],
                                              preferred_element_type=F32)
            o = _rms_scale(o) * on_ref[...]
            o_ref[0, out_rows, vcols] = (o * r_ref[0, rows, vcols].astype(F32)).astype(o_ref.dtype)

    for h in range(heads):
        @pl.when(head_s == h)
        def _(h=h):
            o_ref[0, tp:, h * dv:(h + 1) * dv] = o_s


def _gla(qkvr, u, layer, gate_params, s0, onorm, *, batch, seq, heads, dk, dv, tp, ts, tb=256):
    tm = tp + ts
    nt = seq // tb
    per_tile = tp // tb
    s_blk = tp // ts
    hk, hv = heads * dk, heads * dv
    assert hv == 2 * hk
    assert batch * nt == N_TOKEN_TILES * heads and per_tile == heads
    qkvr3 = qkvr.reshape(N_TOKEN_TILES, tm, qkvr.shape[1])
    d = u.shape[1]
    u3 = u.reshape(N_TOKEN_TILES, tm, d)
    resident = lambda a: pl.BlockSpec(a.shape, lambda b, t: (0,) * a.ndim)
    kq, vq, rq = heads, (2 * hk) // dv, (2 * hk) // dv + heads

    def tok(col):
        def index(b, t):
            blk = b * nt + t
            return (blk // per_tile, blk % per_tile, col)
        return index

    def smp(col0):
        def index(b, t):
            blk = b * nt + t
            return (blk // heads, s_blk, col0 + blk % heads)
        return index

    state = lambda b, t: ((b * nt + t) // heads, (b * nt + t) % heads, 0, 0)
    on_arg, on_spec = _layer_vector(onorm, layer)
    return pl.pallas_call(
        functools.partial(_gla_body, heads=heads, nt=nt, per_tile=per_tile),
        grid=(batch, nt),
        in_specs=[
            pl.BlockSpec((1, tb, hk), tok(0)),
            pl.BlockSpec((1, tb, hk), tok(1)),
            pl.BlockSpec((1, tb, hv), tok(1)),
            pl.BlockSpec((1, tb, hv), tok(2)),
            pl.BlockSpec((1, tb, d), tok(0)),
            on_spec,
            *[resident(a) for a in gate_params],
            pl.BlockSpec((1, ts, dk), smp(0)),
            pl.BlockSpec((1, ts, dk), smp(kq)),
            pl.BlockSpec((1, ts, dv), smp(vq)),
            pl.BlockSpec((1, ts, dv), smp(rq)),
            pl.BlockSpec((1, ts, d), lambda b, t: ((b * nt + t) // heads, s_blk, 0)),
            _layer_operand(s0, layer, (ts, 1, dk, dv), state)[1],
        ],
        out_specs=[
            pl.BlockSpec((1, tm, hv), lambda b, t: ((b * nt + t) // per_tile, 0, 0)),
            pl.BlockSpec((1, heads, dk, dv), lambda b, t: (b, 0, 0, 0)),
            pl.BlockSpec((ts, 1, dk, dv), state),
        ],
        out_shape=[
            jax.ShapeDtypeStruct((N_TOKEN_TILES, tm, hv), BF16),
            jax.ShapeDtypeStruct((batch, heads, dk, dv), F32),
            jax.ShapeDtypeStruct(s0.shape[1:], F32),
        ],
        compiler_params=_cparams("arbitrary", "arbitrary"),
        name="gla",
    )(qkvr3, qkvr3, qkvr3, qkvr3, u3, on_arg, *gate_params, qkvr3, qkvr3, qkvr3, qkvr3, u3, s0)


def kernel(x_prompt, x_sample, state_rglru_h, state_rglru_conv, state_gla_S, ln_ffn1, ffn1_w1, ffn1_w3, ffn1_w2, ln_mix, ln_ffn2, ffn2_w1, ffn2_w3, ffn2_w2, rg_w_y, rg_w_x, rg_conv_w, rg_conv_b, rg_w_a, rg_b_a, rg_w_i, rg_b_i, rg_lambda, rg_w_o, gla_w_q, gla_w_k, gla_w_v, gla_w_g1, gla_w_g2, gla_b_g, gla_w_r, gla_onorm, gla_w_o, ln_final):
    batch, seq, d = x_prompt.shape
    n_dec = x_sample.shape[0]
    depth = ln_ffn1.shape[0]
    heads, dk, dv = state_gla_S.shape[2:]
    tp, ts = _token_tiles(batch * seq, n_dec)
    tm = tp + ts
    tiles = dict(tp=tp, ts=ts)

    x = (x_prompt.reshape(batch * seq, d), x_sample.reshape(n_dec, d))

    ffn_f32 = []
    for i in range(depth):
        ffn_f32 += [((ffn1_w1, ffn1_w3, ffn1_w2), i), ((ffn2_w1, ffn2_w3, ffn2_w2), i)]
    ffn_f32.append(None)
    ffn_w = tuple(_to_bf16(w, 0) for w in ffn_f32[0][0])

    p_h, p_conv, p_s, s_h, s_conv, s_s = [], [], [], [], [], []
    for i in range(depth):
        j = i // 2
        x, u, ffn_w = _ffn(x, i, ln_ffn1, ffn_w, g_mix=ln_mix, next_weights=ffn_f32[2 * i + 1], **tiles)
        if i % 2 == 0:
            proj = _matmul(u, j, [rg_w_y, rg_w_x], [_gelu_tanh, _identity], tm=tm)
            params = _rg_params(j, rg_conv_w, rg_conv_b, rg_w_a, rg_b_a, rg_w_i, rg_b_i, rg_lambda)
            mix, hl, cl = _rg_prompt(proj, params, batch=batch, seq=seq, **tiles)
            mix, hn, cn = _rg_sample(mix, proj, j, state_rglru_conv, state_rglru_h, params, **tiles)
            p_h.append(hl.reshape(batch, d))
            p_conv.append(cl)
            s_h.append(hn)
            s_conv.append(cn.reshape(n_dec, CONV_W - 1, d))
            w_o = rg_w_o
        else:
            qkvr = _matmul(u, j, [gla_w_q, gla_w_k, gla_w_v, gla_w_r],
                           [_identity, _identity, _identity, _silu], tm=tm, out_dtype=BF16)
            rank = gla_w_g1.shape[2]
            gate_params = (jnp.pad(gla_w_g1[j], ((0, 0), (0, LANES - rank))).astype(BF16),
                           jnp.pad(gla_w_g2[j], ((0, LANES - rank), (0, 0))).astype(BF16),
                           gla_b_g[j].reshape(1, heads * dk))
            mix, s_fin, s_new = _gla(qkvr, u, j, gate_params, state_gla_S, gla_onorm, batch=batch,
                                     seq=seq, heads=heads, dk=dk, dv=dv, **tiles)
            p_s.append(s_fin)
            s_s.append(s_new)
            w_o = gla_w_o
        x = _matmul(mix.reshape(N_TOKEN_TILES * tm, d), j, [w_o], [_identity], res=x, tm=tm)
        g_final = ln_final.reshape(1, d) if i == depth - 1 else None
        tf = 512 if g_final is None else 256
        x, _, ffn_w = _ffn(x, i, ln_ffn2, ffn_w, g_final=g_final, next_weights=ffn_f32[2 * i + 2], tf=tf,
                           **tiles)

    y_prompt = x[0].reshape(batch, seq, d)
    y_sample = x[1].reshape(n_dec, 1, d)
    return (y_prompt, y_sample, jnp.stack(p_h), jnp.stack(p_conv), jnp.stack(p_s),
            jnp.stack(s_h), jnp.stack(s_conv), jnp.stack(s_s))
```

```python
import functools

import jax
import jax.numpy as jnp
from jax import lax
from jax.experimental import pallas as pl
from jax.experimental.pallas import tpu as pltpu

F32 = jnp.float32
BF16 = jnp.bfloat16

RMS_EPS = 1e-6
RG_C = 8.0
CONV_W = 4
GLA_GATE_NORM = 16.0
GLA_CHUNK = 64

V7X_VMEM_LIMIT_BYTES = 63 * 1024 * 1024
SUBLANES = 8
LANES = 128
BF16_SUBLANES = 16
N_TOKEN_TILES = 8


def _cparams(*sem):
    return pltpu.CompilerParams(dimension_semantics=sem, vmem_limit_bytes=V7X_VMEM_LIMIT_BYTES)


def _rms_scale(x):
    return x * lax.rsqrt(jnp.mean(x * x, axis=-1, keepdims=True) + RMS_EPS)


def _silu(x):
    return x * jax.nn.sigmoid(x)


def _gelu_tanh(y):
    return jax.nn.gelu(y, approximate=True)


def _identity(y):
    return y


def _softplus(x):
    return jnp.maximum(x, 0.0) + jnp.log1p(jnp.exp(-jnp.abs(x)))


FFN_FIRST_PIECE = 512
FFN_LAST_PIECE = 512


def _layer_operand(stacked, layer, block, index_map):
    spec = pl.BlockSpec((None,) + tuple(block), lambda *g: (layer,) + tuple(index_map(*g)))
    return stacked, spec


def _layer_vector(stacked, layer):
    n_layers, n = stacked.shape
    return _layer_operand(stacked.reshape(n_layers, 1, n), layer, (1, n), lambda *g: (0, 0))


def _layer_full(stacked, layer):
    block = stacked.shape[1:]
    return _layer_operand(stacked, layer, block, lambda *g: (0,) * len(block))


def _token_tiles(n_prompt, n_sample):
    tp, ts = n_prompt // N_TOKEN_TILES, n_sample // N_TOKEN_TILES
    assert tp * N_TOKEN_TILES == n_prompt and ts * N_TOKEN_TILES == n_sample
    assert tp % BF16_SUBLANES == 0 and ts % BF16_SUBLANES == 0 and tp % ts == 0
    return tp, ts


def _cast_body(w_ref, o_ref):
    o_ref[...] = w_ref[...].astype(o_ref.dtype)


def _to_bf16(w, layer, *, rows=512):
    _, r, c = w.shape
    return pl.pallas_call(
        _cast_body,
        grid=(r // rows,),
        in_specs=[pl.BlockSpec((None, rows, c), lambda i: (layer, i, 0))],
        out_specs=pl.BlockSpec((rows, c), lambda i: (i, 0)),
        out_shape=jax.ShapeDtypeStruct((r, c), BF16),
        compiler_params=_cparams("arbitrary"),
        name="to_bf16",
    )(w)


def _ffn_body(*refs, tp, ts, split_in, split_out, mix_norm, cast_next):
    refs = list(refs)
    x_refs = [refs.pop(0) for _ in range(2 if split_in else 1)]
    g_ref, w1_ref, w3_ref, w2_ref = (refs.pop(0) for _ in range(4))
    g2_ref = refs.pop(0) if (split_out or mix_norm) else None
    next_f32 = [refs.pop(0) for _ in range(3 if cast_next else 0)]
    o_refs = [refs.pop(0) for _ in range(2 if split_out else 1)]
    u_hbm = refs.pop(0) if mix_norm else None
    next_bf16 = [refs.pop(0) for _ in range(3 if cast_next else 0)]
    xn_ref = refs.pop(0)
    acc_ref = refs.pop(0) if split_out else o_refs[0]
    (sem,) = refs
    tm = tp + ts
    i = pl.program_id(0)
    f = pl.program_id(1)

    n_f = pl.num_programs(1)

    def segments(blocks):
        if len(blocks) == 2:
            return [(0, tp, blocks[0]), (tp, ts, blocks[1])]
        return [(0, tm, blocks[0])]

    def parts(blocks, r0, r1):
        out = []
        for t0, n_rows, ref in segments(blocks):
            lo, hi = max(r0, t0), min(r1, t0 + n_rows)
            if lo < hi:
                out.append((ref, slice(lo - t0, hi - t0)))
        return out

    def read_rows(blocks, r0, r1):
        vals = [ref[rows, :] for ref, rows in parts(blocks, r0, r1)]
        return vals[0] if len(vals) == 1 else jnp.concatenate(vals, axis=0)

    def write_rows(blocks, r0, r1, value):
        off = 0
        for ref, rows in parts(blocks, r0, r1):
            n = rows.stop - rows.start
            ref[rows, :] = value[off:off + n]
            off += n

    def ffn_product(xn):
        h1 = jnp.dot(xn, w1_ref[...].astype(BF16), preferred_element_type=F32)
        h3 = jnp.dot(xn, w3_ref[...].astype(BF16), preferred_element_type=F32)
        h = (_silu(h1) * h3).astype(BF16)
        return jnp.dot(h, w2_ref[...].astype(BF16), preferred_element_type=F32)

    assert 0 < FFN_FIRST_PIECE < tm and 0 < FFN_LAST_PIECE < tm
    assert FFN_FIRST_PIECE % BF16_SUBLANES == 0 and FFN_LAST_PIECE % BF16_SUBLANES == 0
    pieces = [(0, FFN_FIRST_PIECE), (FFN_FIRST_PIECE, tm)]
    pieces_rev = [(0, tm - FFN_LAST_PIECE), (tm - FFN_LAST_PIECE, tm)]

    def cast_next_block():
        for src_ref, dst_ref in zip(next_f32, next_bf16):
            dst_ref[...] = src_ref[...].astype(BF16)

    @pl.when(f == 0)
    def _():
        for r0, r1 in pieces:
            x = read_rows(x_refs, r0, r1)
            xn = (_rms_scale(x) * g_ref[...]).astype(BF16)
            xn_ref[r0:r1, :] = xn
            acc_ref[r0:r1, :] = 2.0 * x + ffn_product(xn)
        cast_next_block()

    @pl.when((f > 0) & (f < n_f - 1))
    def _():
        acc_ref[...] += ffn_product(xn_ref[...])
        cast_next_block()

    @pl.when(f == n_f - 1)
    def _():
        cast_next_block()
        for r0, r1 in pieces_rev:
            y = 0.5 * (acc_ref[r0:r1, :] + ffn_product(xn_ref[r0:r1, :]))
            if split_out:
                y = _rms_scale(y) * g2_ref[...]
            write_rows(o_refs, r0, r1, y)
            if mix_norm:
                xn_ref[r0:r1, :] = (_rms_scale(y) * g2_ref[...]).astype(BF16)
        if mix_norm:
            cu = pltpu.make_async_copy(xn_ref, u_hbm.at[pl.ds(i * tm, tm)], sem.at[0])
            cu.start()
            cu.wait()


def _ffn(x, layer, g, weights, *, tp, ts, g_mix=None, g_final=None, next_weights=None, tf=512):
    split_in = isinstance(x, tuple)
    split_out = g_final is not None
    mix_norm = g_mix is not None
    cast_next = next_weights is not None
    assert not (split_out and mix_norm)
    xs = list(x) if split_in else [x]
    w1, w3, w2 = weights
    d, f = w1.shape
    tm = tp + ts
    nf = f // tf
    operands = [
        _layer_vector(g, layer),
        (w1, pl.BlockSpec((d, tf), lambda i, j: (0, j))),
        (w3, pl.BlockSpec((d, tf), lambda i, j: (0, j))),
        (w2, pl.BlockSpec((tf, d), lambda i, j: (j, 0))),
    ]
    if mix_norm:
        operands.append(_layer_vector(g_mix, layer))
    if split_out:
        operands.append(_layer_vector(g_final, 0))
    row_tile = lambda rows: pl.BlockSpec((rows, d), lambda i, j: (i, 0))
    x_specs = [row_tile(tp), row_tile(ts)] if split_in else [row_tile(tm)]
    if split_out:
        out_specs = [row_tile(tp), row_tile(ts)]
        out_shape = [jax.ShapeDtypeStruct((tp * N_TOKEN_TILES, d), F32),
                     jax.ShapeDtypeStruct((ts * N_TOKEN_TILES, d), F32)]
    else:
        out_specs = [row_tile(tm)]
        out_shape = [jax.ShapeDtypeStruct((tm * N_TOKEN_TILES, d), F32)]
    if mix_norm:
        out_specs.append(pl.BlockSpec(memory_space=pl.ANY))
        out_shape.append(jax.ShapeDtypeStruct((tm * N_TOKEN_TILES, d), BF16))
    if cast_next:
        (n1, n3, n2), next_layer = next_weights
        db = d // N_TOKEN_TILES
        assert n1.shape[1:] == (d, f) and n2.shape[1:] == (f, d) and db * N_TOKEN_TILES == d
        operands += [
            _layer_operand(n1, next_layer, (db, tf), lambda i, j: (i, j)),
            _layer_operand(n3, next_layer, (db, tf), lambda i, j: (i, j)),
            _layer_operand(n2, next_layer, (tf, db), lambda i, j: (j, i)),
        ]
        out_specs += [pl.BlockSpec((db, tf), lambda i, j: (i, j)), pl.BlockSpec((db, tf), lambda i, j: (i, j)),
                      pl.BlockSpec((tf, db), lambda i, j: (j, i))]
        out_shape += [jax.ShapeDtypeStruct((d, f), BF16), jax.ShapeDtypeStruct((d, f), BF16),
                      jax.ShapeDtypeStruct((f, d), BF16)]
    out = pl.pallas_call(
        functools.partial(_ffn_body, tp=tp, ts=ts, split_in=split_in, split_out=split_out,
                          mix_norm=mix_norm, cast_next=cast_next),
        grid=(N_TOKEN_TILES, nf),
        in_specs=x_specs + [spec for _, spec in operands],
        out_specs=out_specs,
        out_shape=out_shape,
        scratch_shapes=[pltpu.VMEM((tm, d), BF16)] + ([pltpu.VMEM((tm, d), F32)] if split_out else [])
        + [pltpu.SemaphoreType.DMA((1,))],
        compiler_params=_cparams("arbitrary", "arbitrary"),
        name="ffn",
    )(*xs, *[arr for arr, _ in operands])
    out = list(out)
    acts = tuple(out[:2]) if split_out else out[0]
    rest = out[2:] if split_out else out[1:]
    u = rest.pop(0) if mix_norm else None
    return acts, u, (tuple(rest) if cast_next else None)


def _matmul_body(u_ref, *refs, tiles, layer, has_res):
    n_src = 1 + max(src for src, _, _ in tiles)
    w_hbm = refs[:n_src]
    rest = refs[n_src:]
    if has_res:
        r_ref, o_ref, stage_ref, wb_ref, sem = rest
    else:
        o_ref, stage_ref, wb_ref, sem = rest
    tn = stage_ref.shape[1]
    j = pl.program_id(0)
    i = pl.program_id(1)

    def weight_copy(jj):
        src, col0, _ = tiles[jj]
        return pltpu.make_async_copy(w_hbm[src].at[layer, :, pl.ds(col0, tn)], stage_ref, sem.at[0])

    def at_tile(jj, fn):
        pl.when(j == jj)(fn)

    @pl.when(i == 0)
    def _():
        at_tile(0, lambda: weight_copy(0).start())
        for jj in range(len(tiles)):
            at_tile(jj, lambda jj=jj: weight_copy(jj).wait())
        wb_ref[...] = stage_ref[...].astype(BF16)
        for jj in range(len(tiles) - 1):
            at_tile(jj, lambda jj=jj: weight_copy(jj + 1).start())

    for act in dict.fromkeys(act for _, _, act in tiles):
        uses = functools.reduce(jnp.logical_or, [j == jj for jj, t in enumerate(tiles) if t[2] is act])

        @pl.when(uses)
        def _(act=act):
            y = act(jnp.dot(u_ref[...], wb_ref[...], preferred_element_type=F32))
            if has_res:
                y = r_ref[...] + y
            o_ref[...] = y.astype(o_ref.dtype)


def _matmul(u, layer, ws, acts, res=None, *, tm, tn=1024, out_dtype=F32):
    m, d = u.shape
    tiles = []
    for src, (w, act) in enumerate(zip(ws, acts)):
        assert w.shape[2] % tn == 0
        tiles += [(src, col0, act) for col0 in range(0, w.shape[2], tn)]
    n = len(tiles) * tn
    io_spec = lambda: pl.BlockSpec((tm, tn), lambda j, i: (i, j))
    has_res = res is not None
    return pl.pallas_call(
        functools.partial(_matmul_body, tiles=tuple(tiles), layer=layer, has_res=has_res),
        grid=(len(tiles), m // tm),
        in_specs=[pl.BlockSpec((tm, d), lambda j, i: (i, 0))]
        + [pl.BlockSpec(memory_space=pl.ANY)] * len(ws) + ([io_spec()] if has_res else []),
        out_specs=io_spec(),
        out_shape=jax.ShapeDtypeStruct((m, n), out_dtype),
        scratch_shapes=[pltpu.VMEM((d, tn), F32), pltpu.VMEM((d, tn), BF16),
                        pltpu.SemaphoreType.DMA((1,))],
        compiler_params=_cparams("arbitrary", "arbitrary"),
        name="matmul",
    )(u, *ws, *([res] if has_res else []))


def _prompt_rows_of_tile(o_ref, nt, per_tile, tb):
    blk = pl.program_id(0) * nt + pl.program_id(1)
    k = blk % per_tile
    tp = per_tile * tb

    @pl.when(k == 0)
    def _():
        o_ref[0, tp:, :] = jnp.zeros((o_ref.shape[1] - tp, o_ref.shape[2]), o_ref.dtype)

    return pl.ds(pl.multiple_of(k * tb, tb), tb)


def _sigmoid(x):
    return 0.5 * jnp.tanh(0.5 * x) + 0.5


def _rglru_coeffs(conv_out, wa_ref, ba_ref, wi_ref, bi_ref, lam_ref, store):
    n_blocks, bw, _ = wa_ref.shape
    for n in range(n_blocks):
        cols = slice(n * bw, (n + 1) * bw)
        xb = conv_out(cols)
        xb16 = xb.astype(BF16)
        wa = wa_ref[n].astype(BF16)
        wi = wi_ref[n].astype(BF16)
        r = _sigmoid(jnp.dot(xb16, wa, preferred_element_type=F32) + ba_ref[:, cols])
        i = _sigmoid(jnp.dot(xb16, wi, preferred_element_type=F32) + bi_ref[:, cols])
        log_a = -RG_C * r * _softplus(-lam_ref[:, cols])
        a = jnp.exp(log_a)
        b = jnp.sqrt(1.0 - a * a) * (i * xb)
        store(n, a, b)


def _rg_prompt_body(xw_ref, gate_ref, cw_ref, cb_ref, wa_ref, ba_ref, wi_ref, bi_ref, lam_ref,
                    o_ref, hl_ref, cl_ref, ext_ref, a_ref, b_ref, h_ref, *, nt, per_tile):
    t = pl.program_id(1)
    tb = xw_ref.shape[1]
    bw = wa_ref.shape[1]
    pad = SUBLANES

    @pl.when(t == 0)
    def _():
        ext_ref[0:pad, :] = jnp.zeros((pad, ext_ref.shape[1]), F32)
        h_ref[...] = jnp.zeros_like(h_ref)

    ext_ref[pad:pad + tb, :] = xw_ref[0]

    def conv_out(cols):
        ext = ext_ref[:, cols]
        y = ext * cw_ref[0:1, cols]
        for j in range(1, CONV_W):
            y = ext * cw_ref[j:j + 1, cols] + pltpu.roll(y, 1, axis=0)
        return cb_ref[:, cols] + y[pad:, :]

    seg = tb // SUBLANES
    pitch = a_ref.shape[1] // SUBLANES
    n_slabs = a_ref.shape[0]

    def store(n, a, b):
        for half in range(bw // LANES):
            lanes = slice(half * LANES, (half + 1) * LANES)
            for s in range(SUBLANES):
                dst = slice(s * pitch, s * pitch + seg)
                a_ref[n * (bw // LANES) + half, dst, :] = a[s * seg:(s + 1) * seg, lanes]
                b_ref[n * (bw // LANES) + half, dst, :] = b[s * seg:(s + 1) * seg, lanes]

    _rglru_coeffs(conv_out, wa_ref, ba_ref, wi_ref, bi_ref, lam_ref, store)

    def step(v, carry):
        hs, ps = carry
        rows = pl.ds(v, SUBLANES, stride=pitch)
        new_h, new_p = [], []
        for slab in range(n_slabs):
            a = a_ref[slab, rows, :]
            h = a * hs[slab] + b_ref[slab, rows, :]
            p = a * ps[slab]
            b_ref[slab, rows, :] = h
            a_ref[slab, rows, :] = p
            new_h.append(h)
            new_p.append(p)
        return tuple(new_h), tuple(new_p)

    zeros = tuple(jnp.zeros((SUBLANES, LANES), F32) for _ in range(n_slabs))
    ones = tuple(jnp.ones((SUBLANES, LANES), F32) for _ in range(n_slabs))
    lax.fori_loop(0, seg, step, (zeros, ones))

    out_rows = _prompt_rows_of_tile(o_ref, nt, per_tile, tb)
    for slab in range(n_slabs):
        lanes = slice(slab * LANES, (slab + 1) * LANES)
        h_in = h_ref[0:1, lanes]
        for s in range(SUBLANES):
            src = slice(s * pitch, s * pitch + seg)
            h = b_ref[slab, src, :] + a_ref[slab, src, :] * h_in
            dst = pl.ds(pl.multiple_of(out_rows.start + s * seg, seg), seg)
            o_ref[0, dst, lanes] = (h * gate_ref[0, s * seg:(s + 1) * seg, lanes]).astype(o_ref.dtype)
            h_in = h[seg - 1:seg, :]
        h_ref[0:1, lanes] = h_in
        hl_ref[0, :, lanes] = h_in
    cl_ref[0] = ext_ref[pad + tb - (CONV_W - 1):pad + tb, :]
    ext_ref[0:pad, :] = ext_ref[tb:tb + pad, :]


def _segment_pitch(seg):
    pitch = -(-seg // SUBLANES) * SUBLANES
    while (pitch // SUBLANES) % 2 == 0:
        pitch += SUBLANES
    return pitch


def _rg_params(layer, conv_w, conv_b, w_a, b_a, w_i, b_i, lam):
    operands = [_layer_full(conv_w, layer), _layer_vector(conv_b, layer), _layer_full(w_a, layer),
                _layer_vector(b_a, layer), _layer_full(w_i, layer), _layer_vector(b_i, layer),
                _layer_vector(lam, layer)]
    return [arr for arr, _ in operands], [spec for _, spec in operands]


def _rg_prompt(proj, params, *, batch, seq, tp, ts, tb=256):
    p_args, p_specs = params
    m, d2 = proj.shape
    d = d2 // 2
    tm = tp + ts
    nt = seq // tb
    per_tile = tp // tb
    proj3 = proj.reshape(N_TOKEN_TILES, tm, d2)

    def tok(col):
        def index(b, t):
            blk = b * nt + t
            return (blk // per_tile, blk % per_tile, col)
        return index

    mix, hl, cl = pl.pallas_call(
        functools.partial(_rg_prompt_body, nt=nt, per_tile=per_tile),
        grid=(batch, nt),
        in_specs=[pl.BlockSpec((1, tb, d), tok(1)), pl.BlockSpec((1, tb, d), tok(0))] + p_specs,
        out_specs=[
            pl.BlockSpec((1, tm, d), lambda b, t: ((b * nt + t) // per_tile, 0, 0)),
            pl.BlockSpec((1, 1, d), lambda b, t: (b, 0, 0)),
            pl.BlockSpec((1, CONV_W - 1, d), lambda b, t: (b, 0, 0)),
        ],
        out_shape=[
            jax.ShapeDtypeStruct((N_TOKEN_TILES, tm, d), BF16),
            jax.ShapeDtypeStruct((batch, 1, d), F32),
            jax.ShapeDtypeStruct((batch, CONV_W - 1, d), F32),
        ],
        scratch_shapes=[
            pltpu.VMEM((tb + SUBLANES, d), F32),
            pltpu.VMEM((d // LANES, SUBLANES * _segment_pitch(tb // SUBLANES), LANES), F32),
            pltpu.VMEM((d // LANES, SUBLANES * _segment_pitch(tb // SUBLANES), LANES), F32),
            pltpu.VMEM((SUBLANES, d), F32),
        ],
        compiler_params=_cparams("arbitrary", "arbitrary"),
        name="rg_prompt",
    )(proj3, proj3, *p_args)
    return mix, hl, cl


def _rg_sample_body(mix_ref, xw_ref, gate_ref, conv_ref, h0_ref, cw_ref, cb_ref, wa_ref, ba_ref,
                    wi_ref, bi_ref, lam_ref, o_ref, hn_ref, cn_ref, a_ref, b_ref):
    del mix_ref
    n, d = h0_ref.shape
    bw = wa_ref.shape[1]
    x = xw_ref[...].reshape(n, d)
    xc = cb_ref[...]
    for j in range(CONV_W - 1):
        xc = xc + conv_ref[:, j * d:(j + 1) * d] * cw_ref[j:j + 1, :]
    xc = xc + x * cw_ref[CONV_W - 1:CONV_W, :]

    def store(k, a, b):
        a_ref[:, k * bw:(k + 1) * bw] = a
        b_ref[:, k * bw:(k + 1) * bw] = b

    _rglru_coeffs(lambda cols: xc[:, cols], wa_ref, ba_ref, wi_ref, bi_ref, lam_ref, store)
    h = b_ref[...] + a_ref[...] * h0_ref[...]
    hn_ref[...] = h
    o_ref[...] = (h * gate_ref[...].reshape(n, d)).reshape(o_ref.shape).astype(o_ref.dtype)
    cn_ref[:, 0:(CONV_W - 2) * d] = conv_ref[:, d:(CONV_W - 1) * d]
    cn_ref[:, (CONV_W - 2) * d:] = x


def _rg_sample(mix, proj, layer, conv_state, h0, params, *, tp, ts):
    p_args, p_specs = params
    n_layers, n, d = h0.shape
    tm = tp + ts
    proj3 = proj.reshape(N_TOKEN_TILES, tm, 2 * d)
    s_blk = tp // ts
    full2 = lambda shape: pl.BlockSpec(shape, lambda i: (0, 0))
    kc = (CONV_W - 1) * d
    conv_arg, conv_spec = _layer_full(conv_state.reshape(n_layers, n, kc), layer)
    h0_arg, h0_spec = _layer_full(h0, layer)
    return pl.pallas_call(
        _rg_sample_body,
        grid=(1,),
        in_specs=[
            pl.BlockSpec(memory_space=pl.ANY),
            pl.BlockSpec((N_TOKEN_TILES, ts, d), lambda i: (0, s_blk, 1)),
            pl.BlockSpec((N_TOKEN_TILES, ts, d), lambda i: (0, s_blk, 0)),
            conv_spec, h0_spec,
        ] + p_specs,
        out_specs=[
            pl.BlockSpec((N_TOKEN_TILES, ts, d), lambda i: (0, s_blk, 0)),
            full2((n, d)),
            full2((n, kc)),
        ],
        out_shape=[
            jax.ShapeDtypeStruct(mix.shape, mix.dtype),
            jax.ShapeDtypeStruct((n, d), F32),
            jax.ShapeDtypeStruct((n, kc), F32),
        ],
        scratch_shapes=[pltpu.VMEM((n, d), F32), pltpu.VMEM((n, d), F32)],
        input_output_aliases={0: 0},
        compiler_params=_cparams("arbitrary"),
        name="rg_sample",
    )(mix, proj3, proj3, conv_arg, h0_arg, *p_args)


def _gla_log_decay(u, w1_ref, w2_ref, b_ref):
    t = jnp.dot(u, w1_ref[...].astype(BF16), preferred_element_type=F32)
    logit = jnp.dot(t.astype(BF16), w2_ref[...].astype(BF16), preferred_element_type=F32) + b_ref[...]
    log_sigmoid = jnp.minimum(logit, 0.0) - jnp.log(1.0 + jnp.exp(-jnp.abs(logit)))
    return log_sigmoid * (1.0 / GLA_GATE_NORM)


def _gla_sample_update(q_ref, k_ref, v_ref, r_ref, g, s0_ref, on_ref, s_ref):
    bb, dk = q_ref.shape[1:]
    alpha_t = jnp.exp(g).T
    k_t = k_ref[0].astype(F32).T
    q_t = (q_ref[0].astype(F32) * dk ** -0.5).T
    v = v_ref[0].astype(F32)
    rows = []
    for b in range(bb):
        s_new = alpha_t[:, b:b + 1] * s0_ref[b, 0] + k_t[:, b:b + 1] * v[b:b + 1, :]
        s_ref[b, 0] = s_new
        rows.append(jnp.sum(q_t[:, b:b + 1] * s_new, axis=0, keepdims=True))
    o = jnp.concatenate(rows, axis=0)
    o = _rms_scale(o) * on_ref[...]
    return o * r_ref[0].astype(F32)


def _gla_body(q_ref, k_ref, v_ref, r_ref, u_ref, on_ref, wg1_ref, wg2_ref, bg_ref,
              sq_ref, sk_ref, sv_ref, sr_ref, su_ref, s0_ref, o_ref, s_ref, ss_ref, *,
              heads, nt, per_tile):
    t = pl.program_id(1)
    tb = q_ref.shape[1]
    tp = per_tile * tb
    c = GLA_CHUNK
    dk = q_ref.shape[2] // heads
    dv = v_ref.shape[2] // heads

    @pl.when(t == 0)
    def _():
        s_ref[...] = jnp.zeros_like(s_ref)

    blk = pl.program_id(0) * nt + t
    head_s = blk % heads
    g_s_all = _gla_log_decay(su_ref[0], wg1_ref, wg2_ref, bg_ref)
    g_s = jnp.zeros((g_s_all.shape[0], dk), F32)
    for h in range(heads):
        g_s = jnp.where(head_s == h, g_s_all[:, h * dk:(h + 1) * dk], g_s)
    o_s = _gla_sample_update(sq_ref, sk_ref, sv_ref, sr_ref, g_s, s0_ref, on_ref, ss_ref)
    o_s = o_s.astype(o_ref.dtype)

    row = lax.broadcasted_iota(jnp.int32, (tb, tb), 0)
    col = lax.broadcasted_iota(jnp.int32, (tb, tb), 1)
    same_chunk_causal = ((row >= col) & ((row ^ col) < c)).astype(BF16)
    g_rest = _gla_log_decay(u_ref[0], wg1_ref, wg2_ref, bg_ref)
    bcum = jnp.zeros(g_rest.shape, F32)
    for _ in range(3):
        g_part = g_rest.astype(BF16)
        bcum = bcum + jnp.dot(same_chunk_causal, g_part, preferred_element_type=F32)
        g_rest = g_rest - g_part.astype(F32)
    nc = tb // c
    chunk_rows = [slice(ci * c, (ci + 1) * c) for ci in range(nc)]
    q = q_ref[0].astype(F32) * dk ** -0.5
    k = k_ref[0].astype(F32)
    q_i = q * jnp.exp(bcum)
    k_i = (k * jnp.exp(-bcum)).astype(BF16)
    g_tot = [bcum[(ci + 1) * c - 1:(ci + 1) * c, :] for ci in range(nc)]
    g_before = [jnp.zeros_like(g_tot[0])]
    for ci in range(nc):
        g_before.append(g_before[-1] + g_tot[ci])
    k_e = [k[rows, :] * jnp.exp(g_tot[ci] - bcum[rows, :]) for ci, rows in enumerate(chunk_rows)]
    q_in = jnp.concatenate([(q_i[rows, :] * jnp.exp(g_before[ci])).astype(BF16)
                            for ci, rows in enumerate(chunk_rows)], axis=0)
    k_out = jnp.concatenate([(k_e[ci] * jnp.exp(g_before[nc] - g_before[ci + 1])).astype(BF16)
                             for ci in range(nc)], axis=0)
    decay_t = jnp.exp(jnp.broadcast_to(g_before[nc], (SUBLANES, heads * dk))).T
    q_i = q_i.astype(BF16)

    out_base = (blk % per_tile) * tb
    o_inter = []
    for h in range(heads):
        kcols = slice(h * dk, (h + 1) * dk)
        vcols = slice(h * dv, (h + 1) * dv)
        s = s_ref[0, h]
        o_inter.append(jnp.dot(q_in[:, kcols], s.astype(BF16), preferred_element_type=F32))
        s_ref[0, h] = decay_t[kcols, 0:1] * s + lax.dot_general(
            k_out[:, kcols], v_ref[0, :, vcols], (((0,), (0,)), ((), ())), preferred_element_type=F32)

    for ci, rows in enumerate(chunk_rows):
        out_rows = pl.ds(pl.multiple_of(out_base + ci * c, c), c)
        n_keys = (ci + 1) * c
        keys = jnp.concatenate(
            [(k_e[cj] * jnp.exp(g_before[ci] - g_before[cj + 1])).astype(BF16) for cj in range(ci)]
            + [k_i[rows, :]], axis=0)
        visible = (lax.broadcasted_iota(jnp.int32, (c, n_keys), 1)
                   <= lax.broadcasted_iota(jnp.int32, (c, n_keys), 0) + ci * c)
        for h in range(heads):
            kcols = slice(h * dk, (h + 1) * dk)
            vcols = slice(h * dv, (h + 1) * dv)
            scores = lax.dot_general(q_i[rows, kcols], keys[:, kcols], (((1,), (1,)), ((), ())),
                                     preferred_element_type=F32)
            scores = jnp.where(visible, scores, 0.0).astype(BF16)
            o = o_inter[h][rows, :] + jnp.dot(scores, v_ref[0, 0:n_keys, vcols],
                                              preferred_element_type=F32)
            o = _rms_scale(o) * on_ref[...]
            o_ref[0, out_rows, vcols] = (o * r_ref[0, rows, vcols].astype(F32)).astype(o_ref.dtype)

    for h in range(heads):
        @pl.when(head_s == h)
        def _(h=h):
            o_ref[0, tp:, h * dv:(h + 1) * dv] = o_s


def _gla(qkvr, u, layer, gate_params, s0, onorm, *, batch, seq, heads, dk, dv, tp, ts, tb=256):
    tm = tp + ts
    nt = seq // tb
    per_tile = tp // tb
    s_blk = tp // ts
    hk, hv = heads * dk, heads * dv
    assert hv == 2 * hk
    assert batch * nt == N_TOKEN_TILES * heads and per_tile == heads
    qkvr3 = qkvr.reshape(N_TOKEN_TILES, tm, qkvr.shape[1])
    d = u.shape[1]
    u3 = u.reshape(N_TOKEN_TILES, tm, d)
    resident = lambda a: pl.BlockSpec(a.shape, lambda b, t: (0,) * a.ndim)
    kq, vq, rq = heads, (2 * hk) // dv, (2 * hk) // dv + heads

    def tok(col):
        def index(b, t):
            blk = b * nt + t
            return (blk // per_tile, blk % per_tile, col)
        return index

    def smp(col0):
        def index(b, t):
            blk = b * nt + t
            return (blk // heads, s_blk, col0 + blk % heads)
        return index

    state = lambda b, t: ((b * nt + t) // heads, (b * nt + t) % heads, 0, 0)
    on_arg, on_spec = _layer_vector(onorm, layer)
    return pl.pallas_call(
        functools.partial(_gla_body, heads=heads, nt=nt, per_tile=per_tile),
        grid=(batch, nt),
        in_specs=[
            pl.BlockSpec((1, tb, hk), tok(0)),
            pl.BlockSpec((1, tb, hk), tok(1)),
            pl.BlockSpec((1, tb, hv), tok(1)),
            pl.BlockSpec((1, tb, hv), tok(2)),
            pl.BlockSpec((1, tb, d), tok(0)),
            on_spec,
            *[resident(a) for a in gate_params],
            pl.BlockSpec((1, ts, dk), smp(0)),
            pl.BlockSpec((1, ts, dk), smp(kq)),
            pl.BlockSpec((1, ts, dv), smp(vq)),
            pl.BlockSpec((1, ts, dv), smp(rq)),
            pl.BlockSpec((1, ts, d), lambda b, t: ((b * nt + t) // heads, s_blk, 0)),
            _layer_operand(s0, layer, (ts, 1, dk, dv), state)[1],
        ],
        out_specs=[
            pl.BlockSpec((1, tm, hv), lambda b, t: ((b * nt + t) // per_tile, 0, 0)),
            pl.BlockSpec((1, heads, dk, dv), lambda b, t: (b, 0, 0, 0)),
            pl.BlockSpec((ts, 1, dk, dv), state),
        ],
        out_shape=[
            jax.ShapeDtypeStruct((N_TOKEN_TILES, tm, hv), BF16),
            jax.ShapeDtypeStruct((batch, heads, dk, dv), F32),
            jax.ShapeDtypeStruct(s0.shape[1:], F32),
        ],
        compiler_params=_cparams("arbitrary", "arbitrary"),
        name="gla",
    )(qkvr3, qkvr3, qkvr3, qkvr3, u3, on_arg, *gate_params, qkvr3, qkvr3, qkvr3, qkvr3, u3, s0)


def kernel(x_prompt, x_sample, state_rglru_h, state_rglru_conv, state_gla_S, ln_ffn1, ffn1_w1, ffn1_w3, ffn1_w2, ln_mix, ln_ffn2, ffn2_w1, ffn2_w3, ffn2_w2, rg_w_y, rg_w_x, rg_conv_w, rg_conv_b, rg_w_a, rg_b_a, rg_w_i, rg_b_i, rg_lambda, rg_w_o, gla_w_q, gla_w_k, gla_w_v, gla_w_g1, gla_w_g2, gla_b_g, gla_w_r, gla_onorm, gla_w_o, ln_final):
    batch, seq, d = x_prompt.shape
    n_dec = x_sample.shape[0]
    depth = ln_ffn1.shape[0]
    heads, dk, dv = state_gla_S.shape[2:]
    tp, ts = _token_tiles(batch * seq, n_dec)
    tm = tp + ts
    tiles = dict(tp=tp, ts=ts)

    x = (x_prompt.reshape(batch * seq, d), x_sample.reshape(n_dec, d))

    ffn_f32 = []
    for i in range(depth):
        ffn_f32 += [((ffn1_w1, ffn1_w3, ffn1_w2), i), ((ffn2_w1, ffn2_w3, ffn2_w2), i)]
    ffn_f32.append(None)
    ffn_w = tuple(_to_bf16(w, 0) for w in ffn_f32[0][0])

    p_h, p_conv, p_s, s_h, s_conv, s_s = [], [], [], [], [], []
    for i in range(depth):
        j = i // 2
        x, u, ffn_w = _ffn(x, i, ln_ffn1, ffn_w, g_mix=ln_mix, next_weights=ffn_f32[2 * i + 1], **tiles)
        if i % 2 == 0:
            proj = _matmul(u, j, [rg_w_y, rg_w_x], [_gelu_tanh, _identity], tm=tm)
            params = _rg_params(j, rg_conv_w, rg_conv_b, rg_w_a, rg_b_a, rg_w_i, rg_b_i, rg_lambda)
            mix, hl, cl = _rg_prompt(proj, params, batch=batch, seq=seq, **tiles)
            mix, hn, cn = _rg_sample(mix, proj, j, state_rglru_conv, state_rglru_h, params, **tiles)
            p_h.append(hl.reshape(batch, d))
            p_conv.append(cl)
            s_h.append(hn)
            s_conv.append(cn.reshape(n_dec, CONV_W - 1, d))
            w_o = rg_w_o
        else:
            qkvr = _matmul(u, j, [gla_w_q, gla_w_k, gla_w_v, gla_w_r],
                           [_identity, _identity, _identity, _silu], tm=tm, out_dtype=BF16)
            rank = gla_w_g1.shape[2]
            gate_params = (jnp.pad(gla_w_g1[j], ((0, 0), (0, LANES - rank))).astype(BF16),
                           jnp.pad(gla_w_g2[j], ((0, LANES - rank), (0, 0))).astype(BF16),
                           gla_b_g[j].reshape(1, heads * dk))
            mix, s_fin, s_new = _gla(qkvr, u, j, gate_params, state_gla_S, gla_onorm, batch=batch,
                                     seq=seq, heads=heads, dk=dk, dv=dv, **tiles)
            p_s.append(s_fin)
            s_s.append(s_new)
            w_o = gla_w_o
        x = _matmul(mix.reshape(N_TOKEN_TILES * tm, d), j, [w_o], [_identity], res=x, tm=tm)
        g_final = ln_final.reshape(1, d) if i == depth - 1 else None
        x, _, ffn_w = _ffn(x, i, ln_ffn2, ffn_w, g_final=g_final, next_weights=ffn_f32[2 * i + 2],
                           **tiles)

    y_prompt = x[0].reshape(batch, seq, d)
    y_sample = x[1].reshape(n_dec, 1, d)
    return (y_prompt, y_sample, jnp.stack(p_h), jnp.stack(p_conv), jnp.stack(p_s),
            jnp.stack(s_h), jnp.stack(s_conv), jnp.stack(s_s))
```

```python
import functools

import jax
import jax.numpy as jnp
from jax import lax
from jax.experimental import pallas as pl
from jax.experimental.pallas import tpu as pltpu

F32 = jnp.float32
BF16 = jnp.bfloat16

RMS_EPS = 1e-6
RG_C = 8.0
CONV_W = 4
GLA_GATE_NORM = 16.0
GLA_CHUNK = 64

V7X_VMEM_LIMIT_BYTES = 127 * 512 * 1024
SUBLANES = 8
LANES = 128
BF16_SUBLANES = 16
N_TOKEN_TILES = 8


def _cparams(*sem):
    return pltpu.CompilerParams(dimension_semantics=sem, vmem_limit_bytes=V7X_VMEM_LIMIT_BYTES)


def _rms_scale(x):
    return x * lax.rsqrt(jnp.mean(x * x, axis=-1, keepdims=True) + RMS_EPS)


def _silu(x):
    return x * jax.nn.sigmoid(x)


def _gelu_tanh(y):
    return jax.nn.gelu(y, approximate=True)


def _identity(y):
    return y


def _softplus(x):
    return jnp.maximum(x, 0.0) + jnp.log1p(jnp.exp(-jnp.abs(x)))


FFN_FIRST_PIECE = 512
FFN_LAST_PIECE = 512


def _layer_operand(stacked, layer, block, index_map):
    spec = pl.BlockSpec((None,) + tuple(block), lambda *g: (layer,) + tuple(index_map(*g)))
    return stacked, spec


def _layer_vector(stacked, layer):
    n_layers, n = stacked.shape
    return _layer_operand(stacked.reshape(n_layers, 1, n), layer, (1, n), lambda *g: (0, 0))


def _layer_full(stacked, layer):
    block = stacked.shape[1:]
    return _layer_operand(stacked, layer, block, lambda *g: (0,) * len(block))


def _token_tiles(n_prompt, n_sample):
    tp, ts = n_prompt // N_TOKEN_TILES, n_sample // N_TOKEN_TILES
    assert tp * N_TOKEN_TILES == n_prompt and ts * N_TOKEN_TILES == n_sample
    assert tp % BF16_SUBLANES == 0 and ts % BF16_SUBLANES == 0 and tp % ts == 0
    return tp, ts


def _cast_body(w_ref, o_ref):
    o_ref[...] = w_ref[...].astype(o_ref.dtype)


def _to_bf16(w, layer, *, rows=512):
    _, r, c = w.shape
    return pl.pallas_call(
        _cast_body,
        grid=(r // rows,),
        in_specs=[pl.BlockSpec((None, rows, c), lambda i: (layer, i, 0))],
        out_specs=pl.BlockSpec((rows, c), lambda i: (i, 0)),
        out_shape=jax.ShapeDtypeStruct((r, c), BF16),
        compiler_params=_cparams("arbitrary"),
        name="to_bf16",
    )(w)


def _ffn_body(*refs, tp, ts, split_in, split_out, mix_norm, cast_next):
    refs = list(refs)
    x_refs = [refs.pop(0) for _ in range(2 if split_in else 1)]
    g_ref, w1_ref, w3_ref, w2_ref = (refs.pop(0) for _ in range(4))
    g2_ref = refs.pop(0) if (split_out or mix_norm) else None
    next_f32 = [refs.pop(0) for _ in range(3 if cast_next else 0)]
    o_refs = [refs.pop(0) for _ in range(2 if split_out else 1)]
    u_hbm = refs.pop(0) if mix_norm else None
    next_bf16 = [refs.pop(0) for _ in range(3 if cast_next else 0)]
    xn_ref = refs.pop(0)
    acc_ref = refs.pop(0) if split_out else o_refs[0]
    u_ref = refs.pop(0) if mix_norm else None
    (sem,) = refs
    tm = tp + ts
    i = pl.program_id(0)
    f = pl.program_id(1)

    n_f = pl.num_programs(1)

    def segments(blocks):
        if len(blocks) == 2:
            return [(0, tp, blocks[0]), (tp, ts, blocks[1])]
        return [(0, tm, blocks[0])]

    def parts(blocks, r0, r1):
        out = []
        for t0, n_rows, ref in segments(blocks):
            lo, hi = max(r0, t0), min(r1, t0 + n_rows)
            if lo < hi:
                out.append((ref, slice(lo - t0, hi - t0)))
        return out

    def read_rows(blocks, r0, r1):
        vals = [ref[rows, :] for ref, rows in parts(blocks, r0, r1)]
        return vals[0] if len(vals) == 1 else jnp.concatenate(vals, axis=0)

    def write_rows(blocks, r0, r1, value):
        off = 0
        for ref, rows in parts(blocks, r0, r1):
            n = rows.stop - rows.start
            ref[rows, :] = value[off:off + n]
            off += n

    def ffn_product(xn):
        h1 = jnp.dot(xn, w1_ref[...].astype(BF16), preferred_element_type=F32)
        h3 = jnp.dot(xn, w3_ref[...].astype(BF16), preferred_element_type=F32)
        h = (_silu(h1) * h3).astype(BF16)
        return jnp.dot(h, w2_ref[...].astype(BF16), preferred_element_type=F32)

    assert 0 < FFN_FIRST_PIECE < tm and 0 < FFN_LAST_PIECE < tm
    assert FFN_FIRST_PIECE % BF16_SUBLANES == 0 and FFN_LAST_PIECE % BF16_SUBLANES == 0
    pieces = [(0, FFN_FIRST_PIECE), (FFN_FIRST_PIECE, tm)]
    pieces_rev = [(0, tm - FFN_LAST_PIECE), (tm - FFN_LAST_PIECE, tm)]

    def cast_next_block():
        for src_ref, dst_ref in zip(next_f32, next_bf16):
            dst_ref[...] = src_ref[...].astype(BF16)

    @pl.when(f == 0)
    def _():
        for r0, r1 in pieces:
            x = read_rows(x_refs, r0, r1)
            xn = (_rms_scale(x) * g_ref[...]).astype(BF16)
            xn_ref[r0:r1, :] = xn
            acc_ref[r0:r1, :] = 2.0 * x + ffn_product(xn)
        cast_next_block()

    @pl.when((f > 0) & (f < n_f - 1))
    def _():
        acc_ref[...] += ffn_product(xn_ref[...])
        cast_next_block()

    @pl.when(f == n_f - 1)
    def _():
        u_copy = lambda tile: pltpu.make_async_copy(u_ref, u_hbm.at[pl.ds(tile * tm, tm)], sem.at[0])
        if mix_norm:
            pl.when(i > 0)(lambda: u_copy(i - 1).wait())
        cast_next_block()
        for r0, r1 in pieces_rev:
            y = 0.5 * (acc_ref[r0:r1, :] + ffn_product(xn_ref[r0:r1, :]))
            if split_out:
                y = _rms_scale(y) * g2_ref[...]
            write_rows(o_refs, r0, r1, y)
            if mix_norm:
                u_ref[r0:r1, :] = (_rms_scale(y) * g2_ref[...]).astype(BF16)
        if mix_norm:
            u_copy(i).start()
            pl.when(i == pl.num_programs(0) - 1)(lambda: u_copy(i).wait())


def _ffn(x, layer, g, weights, *, tp, ts, g_mix=None, g_final=None, next_weights=None, tf=512):
    split_in = isinstance(x, tuple)
    split_out = g_final is not None
    mix_norm = g_mix is not None
    cast_next = next_weights is not None
    assert not (split_out and mix_norm)
    xs = list(x) if split_in else [x]
    w1, w3, w2 = weights
    d, f = w1.shape
    tm = tp + ts
    nf = f // tf
    operands = [
        _layer_vector(g, layer),
        (w1, pl.BlockSpec((d, tf), lambda i, j: (0, j))),
        (w3, pl.BlockSpec((d, tf), lambda i, j: (0, j))),
        (w2, pl.BlockSpec((tf, d), lambda i, j: (j, 0))),
    ]
    if mix_norm:
        operands.append(_layer_vector(g_mix, layer))
    if split_out:
        operands.append(_layer_vector(g_final, 0))
    row_tile = lambda rows: pl.BlockSpec((rows, d), lambda i, j: (i, 0))
    x_specs = [row_tile(tp), row_tile(ts)] if split_in else [row_tile(tm)]
    if split_out:
        out_specs = [row_tile(tp), row_tile(ts)]
        out_shape = [jax.ShapeDtypeStruct((tp * N_TOKEN_TILES, d), F32),
                     jax.ShapeDtypeStruct((ts * N_TOKEN_TILES, d), F32)]
    else:
        out_specs = [row_tile(tm)]
        out_shape = [jax.ShapeDtypeStruct((tm * N_TOKEN_TILES, d), F32)]
    if mix_norm:
        out_specs.append(pl.BlockSpec(memory_space=pl.ANY))
        out_shape.append(jax.ShapeDtypeStruct((tm * N_TOKEN_TILES, d), BF16))
    if cast_next:
        (n1, n3, n2), next_layer = next_weights
        db = d // N_TOKEN_TILES
        assert n1.shape[1:] == (d, f) and n2.shape[1:] == (f, d) and db * N_TOKEN_TILES == d
        operands += [
            _layer_operand(n1, next_layer, (db, tf), lambda i, j: (i, j)),
            _layer_operand(n3, next_layer, (db, tf), lambda i, j: (i, j)),
            _layer_operand(n2, next_layer, (tf, db), lambda i, j: (j, i)),
        ]
        out_specs += [pl.BlockSpec((db, tf), lambda i, j: (i, j)), pl.BlockSpec((db, tf), lambda i, j: (i, j)),
                      pl.BlockSpec((tf, db), lambda i, j: (j, i))]
        out_shape += [jax.ShapeDtypeStruct((d, f), BF16), jax.ShapeDtypeStruct((d, f), BF16),
                      jax.ShapeDtypeStruct((f, d), BF16)]
    out = pl.pallas_call(
        functools.partial(_ffn_body, tp=tp, ts=ts, split_in=split_in, split_out=split_out,
                          mix_norm=mix_norm, cast_next=cast_next),
        grid=(N_TOKEN_TILES, nf),
        in_specs=x_specs + [spec for _, spec in operands],
        out_specs=out_specs,
        out_shape=out_shape,
        scratch_shapes=[pltpu.VMEM((tm, d), BF16)] + ([pltpu.VMEM((tm, d), F32)] if split_out else [])
        + ([pltpu.VMEM((tm, d), BF16)] if mix_norm else []) + [pltpu.SemaphoreType.DMA((1,))],
        compiler_params=_cparams("arbitrary", "arbitrary"),
        name="ffn",
    )(*xs, *[arr for arr, _ in operands])
    out = list(out)
    acts = tuple(out[:2]) if split_out else out[0]
    rest = out[2:] if split_out else out[1:]
    u = rest.pop(0) if mix_norm else None
    return acts, u, (tuple(rest) if cast_next else None)


def _matmul_body(u_ref, *refs, tiles, layer, has_res):
    n_src = 1 + max(src for src, _, _ in tiles)
    w_hbm = refs[:n_src]
    rest = refs[n_src:]
    if has_res:
        r_ref, o_ref, stage_ref, wb_ref, sem = rest
    else:
        o_ref, stage_ref, wb_ref, sem = rest
    tn = stage_ref.shape[1]
    j = pl.program_id(0)
    i = pl.program_id(1)

    def weight_copy(jj):
        src, col0, _ = tiles[jj]
        return pltpu.make_async_copy(w_hbm[src].at[layer, :, pl.ds(col0, tn)], stage_ref, sem.at[0])

    def at_tile(jj, fn):
        pl.when(j == jj)(fn)

    @pl.when(i == 0)
    def _():
        at_tile(0, lambda: weight_copy(0).start())
        for jj in range(len(tiles)):
            at_tile(jj, lambda jj=jj: weight_copy(jj).wait())
        wb_ref[...] = stage_ref[...].astype(BF16)
        for jj in range(len(tiles) - 1):
            at_tile(jj, lambda jj=jj: weight_copy(jj + 1).start())

    for act in dict.fromkeys(act for _, _, act in tiles):
        uses = functools.reduce(jnp.logical_or, [j == jj for jj, t in enumerate(tiles) if t[2] is act])

        @pl.when(uses)
        def _(act=act):
            y = act(jnp.dot(u_ref[...], wb_ref[...], preferred_element_type=F32))
            if has_res:
                y = r_ref[...] + y
            o_ref[...] = y.astype(o_ref.dtype)


def _matmul(u, layer, ws, acts, res=None, *, tm, tn=1024, out_dtype=F32):
    m, d = u.shape
    tiles = []
    for src, (w, act) in enumerate(zip(ws, acts)):
        assert w.shape[2] % tn == 0
        tiles += [(src, col0, act) for col0 in range(0, w.shape[2], tn)]
    n = len(tiles) * tn
    io_spec = lambda: pl.BlockSpec((tm, tn), lambda j, i: (i, j))
    has_res = res is not None
    return pl.pallas_call(
        functools.partial(_matmul_body, tiles=tuple(tiles), layer=layer, has_res=has_res),
        grid=(len(tiles), m // tm),
        in_specs=[pl.BlockSpec((tm, d), lambda j, i: (i, 0))]
        + [pl.BlockSpec(memory_space=pl.ANY)] * len(ws) + ([io_spec()] if has_res else []),
        out_specs=io_spec(),
        out_shape=jax.ShapeDtypeStruct((m, n), out_dtype),
        scratch_shapes=[pltpu.VMEM((d, tn), F32), pltpu.VMEM((d, tn), BF16),
                        pltpu.SemaphoreType.DMA((1,))],
        compiler_params=_cparams("arbitrary", "arbitrary"),
        name="matmul",
    )(u, *ws, *([res] if has_res else []))


def _prompt_rows_of_tile(o_ref, nt, per_tile, tb):
    blk = pl.program_id(0) * nt + pl.program_id(1)
    k = blk % per_tile
    tp = per_tile * tb

    @pl.when(k == 0)
    def _():
        o_ref[0, tp:, :] = jnp.zeros((o_ref.shape[1] - tp, o_ref.shape[2]), o_ref.dtype)

    return pl.ds(pl.multiple_of(k * tb, tb), tb)


def _sigmoid(x):
    return 0.5 * jnp.tanh(0.5 * x) + 0.5


def _rglru_coeffs(conv_out, wa_ref, ba_ref, wi_ref, bi_ref, lam_ref, store):
    n_blocks, bw, _ = wa_ref.shape
    for n in range(n_blocks):
        cols = slice(n * bw, (n + 1) * bw)
        xb = conv_out(cols)
        xb16 = xb.astype(BF16)
        wa = wa_ref[n].astype(BF16)
        wi = wi_ref[n].astype(BF16)
        r = _sigmoid(jnp.dot(xb16, wa, preferred_element_type=F32) + ba_ref[:, cols])
        i = _sigmoid(jnp.dot(xb16, wi, preferred_element_type=F32) + bi_ref[:, cols])
        log_a = -RG_C * r * _softplus(-lam_ref[:, cols])
        a = jnp.exp(log_a)
        b = jnp.sqrt(1.0 - a * a) * (i * xb)
        store(n, a, b)


def _rg_prompt_body(xw_ref, gate_ref, cw_ref, cb_ref, wa_ref, ba_ref, wi_ref, bi_ref, lam_ref,
                    o_ref, hl_ref, cl_ref, ext_ref, a_ref, b_ref, h_ref, *, nt, per_tile):
    t = pl.program_id(1)
    tb = xw_ref.shape[1]
    bw = wa_ref.shape[1]
    pad = SUBLANES

    @pl.when(t == 0)
    def _():
        ext_ref[0:pad, :] = jnp.zeros((pad, ext_ref.shape[1]), F32)
        h_ref[...] = jnp.zeros_like(h_ref)

    ext_ref[pad:pad + tb, :] = xw_ref[0]

    def conv_out(cols):
        ext = ext_ref[:, cols]
        y = ext * cw_ref[0:1, cols]
        for j in range(1, CONV_W):
            y = ext * cw_ref[j:j + 1, cols] + pltpu.roll(y, 1, axis=0)
        return cb_ref[:, cols] + y[pad:, :]

    seg = tb // SUBLANES
    pitch = a_ref.shape[1] // SUBLANES
    n_slabs = a_ref.shape[0]

    def store(n, a, b):
        for half in range(bw // LANES):
            lanes = slice(half * LANES, (half + 1) * LANES)
            for s in range(SUBLANES):
                dst = slice(s * pitch, s * pitch + seg)
                a_ref[n * (bw // LANES) + half, dst, :] = a[s * seg:(s + 1) * seg, lanes]
                b_ref[n * (bw // LANES) + half, dst, :] = b[s * seg:(s + 1) * seg, lanes]

    _rglru_coeffs(conv_out, wa_ref, ba_ref, wi_ref, bi_ref, lam_ref, store)

    def step(v, carry):
        hs, ps = carry
        rows = pl.ds(v, SUBLANES, stride=pitch)
        new_h, new_p = [], []
        for slab in range(n_slabs):
            a = a_ref[slab, rows, :]
            h = a * hs[slab] + b_ref[slab, rows, :]
            p = a * ps[slab]
            b_ref[slab, rows, :] = h
            a_ref[slab, rows, :] = p
            new_h.append(h)
            new_p.append(p)
        return tuple(new_h), tuple(new_p)

    zeros = tuple(jnp.zeros((SUBLANES, LANES), F32) for _ in range(n_slabs))
    ones = tuple(jnp.ones((SUBLANES, LANES), F32) for _ in range(n_slabs))
    lax.fori_loop(0, seg, step, (zeros, ones))

    out_rows = _prompt_rows_of_tile(o_ref, nt, per_tile, tb)
    for slab in range(n_slabs):
        lanes = slice(slab * LANES, (slab + 1) * LANES)
        h_in = h_ref[0:1, lanes]
        for s in range(SUBLANES):
            src = slice(s * pitch, s * pitch + seg)
            h = b_ref[slab, src, :] + a_ref[slab, src, :] * h_in
            dst = pl.ds(pl.multiple_of(out_rows.start + s * seg, seg), seg)
            o_ref[0, dst, lanes] = (h * gate_ref[0, s * seg:(s + 1) * seg, lanes]).astype(o_ref.dtype)
            h_in = h[seg - 1:seg, :]
        h_ref[0:1, lanes] = h_in
        hl_ref[0, :, lanes] = h_in
    cl_ref[0] = ext_ref[pad + tb - (CONV_W - 1):pad + tb, :]
    ext_ref[0:pad, :] = ext_ref[tb:tb + pad, :]


def _segment_pitch(seg):
    pitch = -(-seg // SUBLANES) * SUBLANES
    while (pitch // SUBLANES) % 2 == 0:
        pitch += SUBLANES
    return pitch


def _rg_params(layer, conv_w, conv_b, w_a, b_a, w_i, b_i, lam):
    operands = [_layer_full(conv_w, layer), _layer_vector(conv_b, layer), _layer_full(w_a, layer),
                _layer_vector(b_a, layer), _layer_full(w_i, layer), _layer_vector(b_i, layer),
                _layer_vector(lam, layer)]
    return [arr for arr, _ in operands], [spec for _, spec in operands]


def _rg_prompt(proj, params, *, batch, seq, tp, ts, tb=256):
    p_args, p_specs = params
    m, d2 = proj.shape
    d = d2 // 2
    tm = tp + ts
    nt = seq // tb
    per_tile = tp // tb
    proj3 = proj.reshape(N_TOKEN_TILES, tm, d2)

    def tok(col):
        def index(b, t):
            blk = b * nt + t
            return (blk // per_tile, blk % per_tile, col)
        return index

    mix, hl, cl = pl.pallas_call(
        functools.partial(_rg_prompt_body, nt=nt, per_tile=per_tile),
        grid=(batch, nt),
        in_specs=[pl.BlockSpec((1, tb, d), tok(1)), pl.BlockSpec((1, tb, d), tok(0))] + p_specs,
        out_specs=[
            pl.BlockSpec((1, tm, d), lambda b, t: ((b * nt + t) // per_tile, 0, 0)),
            pl.BlockSpec((1, 1, d), lambda b, t: (b, 0, 0)),
            pl.BlockSpec((1, CONV_W - 1, d), lambda b, t: (b, 0, 0)),
        ],
        out_shape=[
            jax.ShapeDtypeStruct((N_TOKEN_TILES, tm, d), BF16),
            jax.ShapeDtypeStruct((batch, 1, d), F32),
            jax.ShapeDtypeStruct((batch, CONV_W - 1, d), F32),
        ],
        scratch_shapes=[
            pltpu.VMEM((tb + SUBLANES, d), F32),
            pltpu.VMEM((d // LANES, SUBLANES * _segment_pitch(tb // SUBLANES), LANES), F32),
            pltpu.VMEM((d // LANES, SUBLANES * _segment_pitch(tb // SUBLANES), LANES), F32),
            pltpu.VMEM((SUBLANES, d), F32),
        ],
        compiler_params=_cparams("arbitrary", "arbitrary"),
        name="rg_prompt",
    )(proj3, proj3, *p_args)
    return mix, hl, cl


def _rg_sample_body(mix_ref, xw_ref, gate_ref, conv_ref, h0_ref, cw_ref, cb_ref, wa_ref, ba_ref,
                    wi_ref, bi_ref, lam_ref, o_ref, hn_ref, cn_ref, a_ref, b_ref):
    del mix_ref
    n, d = h0_ref.shape
    bw = wa_ref.shape[1]
    x = xw_ref[...].reshape(n, d)
    xc = cb_ref[...]
    for j in range(CONV_W - 1):
        xc = xc + conv_ref[:, j * d:(j + 1) * d] * cw_ref[j:j + 1, :]
    xc = xc + x * cw_ref[CONV_W - 1:CONV_W, :]

    def store(k, a, b):
        a_ref[:, k * bw:(k + 1) * bw] = a
        b_ref[:, k * bw:(k + 1) * bw] = b

    _rglru_coeffs(lambda cols: xc[:, cols], wa_ref, ba_ref, wi_ref, bi_ref, lam_ref, store)
    h = b_ref[...] + a_ref[...] * h0_ref[...]
    hn_ref[...] = h
    o_ref[...] = (h * gate_ref[...].reshape(n, d)).reshape(o_ref.shape).astype(o_ref.dtype)
    cn_ref[:, 0:(CONV_W - 2) * d] = conv_ref[:, d:(CONV_W - 1) * d]
    cn_ref[:, (CONV_W - 2) * d:] = x


def _rg_sample(mix, proj, layer, conv_state, h0, params, *, tp, ts):
    p_args, p_specs = params
    n_layers, n, d = h0.shape
    tm = tp + ts
    proj3 = proj.reshape(N_TOKEN_TILES, tm, 2 * d)
    s_blk = tp // ts
    full2 = lambda shape: pl.BlockSpec(shape, lambda i: (0, 0))
    kc = (CONV_W - 1) * d
    conv_arg, conv_spec = _layer_full(conv_state.reshape(n_layers, n, kc), layer)
    h0_arg, h0_spec = _layer_full(h0, layer)
    return pl.pallas_call(
        _rg_sample_body,
        grid=(1,),
        in_specs=[
            pl.BlockSpec(memory_space=pl.ANY),
            pl.BlockSpec((N_TOKEN_TILES, ts, d), lambda i: (0, s_blk, 1)),
            pl.BlockSpec((N_TOKEN_TILES, ts, d), lambda i: (0, s_blk, 0)),
            conv_spec, h0_spec,
        ] + p_specs,
        out_specs=[
            pl.BlockSpec((N_TOKEN_TILES, ts, d), lambda i: (0, s_blk, 0)),
            full2((n, d)),
            full2((n, kc)),
        ],
        out_shape=[
            jax.ShapeDtypeStruct(mix.shape, mix.dtype),
            jax.ShapeDtypeStruct((n, d), F32),
            jax.ShapeDtypeStruct((n, kc), F32),
        ],
        scratch_shapes=[pltpu.VMEM((n, d), F32), pltpu.VMEM((n, d), F32)],
        input_output_aliases={0: 0},
        compiler_params=_cparams("arbitrary"),
        name="rg_sample",
    )(mix, proj3, proj3, conv_arg, h0_arg, *p_args)


def _gla_log_decay(u, w1_ref, w2_ref, b_ref):
    t = jnp.dot(u, w1_ref[...].astype(BF16), preferred_element_type=F32)
    logit = jnp.dot(t.astype(BF16), w2_ref[...].astype(BF16), preferred_element_type=F32) + b_ref[...]
    log_sigmoid = jnp.minimum(logit, 0.0) - jnp.log(1.0 + jnp.exp(-jnp.abs(logit)))
    return log_sigmoid * (1.0 / GLA_GATE_NORM)


def _gla_sample_update(q_ref, k_ref, v_ref, r_ref, g, s0_ref, on_ref, s_ref):
    bb, dk = q_ref.shape[1:]
    alpha_t = jnp.exp(g).T
    k_t = k_ref[0].astype(F32).T
    q_t = (q_ref[0].astype(F32) * dk ** -0.5).T
    v = v_ref[0].astype(F32)
    rows = []
    for b in range(bb):
        s_new = alpha_t[:, b:b + 1] * s0_ref[b, 0] + k_t[:, b:b + 1] * v[b:b + 1, :]
        s_ref[b, 0] = s_new
        rows.append(jnp.sum(q_t[:, b:b + 1] * s_new, axis=0, keepdims=True))
    o = jnp.concatenate(rows, axis=0)
    o = _rms_scale(o) * on_ref[...]
    return o * r_ref[0].astype(F32)


def _gla_body(q_ref, k_ref, v_ref, r_ref, u_ref, on_ref, wg1_ref, wg2_ref, bg_ref,
              sq_ref, sk_ref, sv_ref, sr_ref, su_ref, s0_ref, o_ref, s_ref, ss_ref, *,
              heads, nt, per_tile):
    t = pl.program_id(1)
    tb = q_ref.shape[1]
    tp = per_tile * tb
    c = GLA_CHUNK
    dk = q_ref.shape[2] // heads
    dv = v_ref.shape[2] // heads

    @pl.when(t == 0)
    def _():
        s_ref[...] = jnp.zeros_like(s_ref)

    blk = pl.program_id(0) * nt + t
    head_s = blk % heads
    g_s_all = _gla_log_decay(su_ref[0], wg1_ref, wg2_ref, bg_ref)
    g_s = jnp.zeros((g_s_all.shape[0], dk), F32)
    for h in range(heads):
        g_s = jnp.where(head_s == h, g_s_all[:, h * dk:(h + 1) * dk], g_s)
    o_s = _gla_sample_update(sq_ref, sk_ref, sv_ref, sr_ref, g_s, s0_ref, on_ref, ss_ref)
    o_s = o_s.astype(o_ref.dtype)

    row = lax.broadcasted_iota(jnp.int32, (tb, tb), 0)
    col = lax.broadcasted_iota(jnp.int32, (tb, tb), 1)
    same_chunk_causal = ((row >= col) & ((row ^ col) < c)).astype(BF16)
    g_rest = _gla_log_decay(u_ref[0], wg1_ref, wg2_ref, bg_ref)
    bcum = jnp.zeros(g_rest.shape, F32)
    for _ in range(3):
        g_part = g_rest.astype(BF16)
        bcum = bcum + jnp.dot(same_chunk_causal, g_part, preferred_element_type=F32)
        g_rest = g_rest - g_part.astype(F32)
    nc = tb // c
    chunk_rows = [slice(ci * c, (ci + 1) * c) for ci in range(nc)]
    q = q_ref[0].astype(F32) * dk ** -0.5
    k = k_ref[0].astype(F32)
    q_i = q * jnp.exp(bcum)
    k_i = (k * jnp.exp(-bcum)).astype(BF16)
    g_tot = [bcum[(ci + 1) * c - 1:(ci + 1) * c, :] for ci in range(nc)]
    g_before = [jnp.zeros_like(g_tot[0])]
    for ci in range(nc):
        g_before.append(g_before[-1] + g_tot[ci])
    k_e = [k[rows, :] * jnp.exp(g_tot[ci] - bcum[rows, :]) for ci, rows in enumerate(chunk_rows)]
    q_in = jnp.concatenate([(q_i[rows, :] * jnp.exp(g_before[ci])).astype(BF16)
                            for ci, rows in enumerate(chunk_rows)], axis=0)
    k_out = jnp.concatenate([(k_e[ci] * jnp.exp(g_before[nc] - g_before[ci + 1])).astype(BF16)
                             for ci in range(nc)], axis=0)
    decay_t = jnp.exp(jnp.broadcast_to(g_before[nc], (SUBLANES, heads * dk))).T
    q_i = q_i.astype(BF16)

    out_base = (blk % per_tile) * tb
    o_inter = []
    for h in range(heads):
        kcols = slice(h * dk, (h + 1) * dk)
        vcols = slice(h * dv, (h + 1) * dv)
        s = s_ref[0, h]
        o_inter.append(jnp.dot(q_in[:, kcols], s.astype(BF16), preferred_element_type=F32))
        s_ref[0, h] = decay_t[kcols, 0:1] * s + lax.dot_general(
            k_out[:, kcols], v_ref[0, :, vcols], (((0,), (0,)), ((), ())), preferred_element_type=F32)

    for ci, rows in enumerate(chunk_rows):
        out_rows = pl.ds(pl.multiple_of(out_base + ci * c, c), c)
        n_keys = (ci + 1) * c
        keys = jnp.concatenate(
            [(k_e[cj] * jnp.exp(g_before[ci] - g_before[cj + 1])).astype(BF16) for cj in range(ci)]
            + [k_i[rows, :]], axis=0)
        visible = (lax.broadcasted_iota(jnp.int32, (c, n_keys), 1)
                   <= lax.broadcasted_iota(jnp.int32, (c, n_keys), 0) + ci * c)
        for h in range(heads):
            kcols = slice(h * dk, (h + 1) * dk)
            vcols = slice(h * dv, (h + 1) * dv)
            scores = lax.dot_general(q_i[rows, kcols], keys[:, kcols], (((1,), (1,)), ((), ())),
                                     preferred_element_type=F32)
            scores = jnp.where(visible, scores, 0.0).astype(BF16)
            o = o_inter[h][rows, :] + jnp.dot(scores, v_ref[0, 0:n_keys, vcols],
                                              preferred_element_type=F32)
            o = _rms_scale(o) * on_ref[...]
            o_ref[0, out_rows, vcols] = (o * r_ref[0, rows, vcols].astype(F32)).astype(o_ref.dtype)

    for h in range(heads):
        @pl.when(head_s == h)
        def _(h=h):
            o_ref[0, tp:, h * dv:(h + 1) * dv] = o_s


def _gla(qkvr, u, layer, gate_params, s0, onorm, *, batch, seq, heads, dk, dv, tp, ts, tb=256):
    tm = tp + ts
    nt = seq // tb
    per_tile = tp // tb
    s_blk = tp // ts
    hk, hv = heads * dk, heads * dv
    assert hv == 2 * hk
    assert batch * nt == N_TOKEN_TILES * heads and per_tile == heads
    qkvr3 = qkvr.reshape(N_TOKEN_TILES, tm, qkvr.shape[1])
    d = u.shape[1]
    u3 = u.reshape(N_TOKEN_TILES, tm, d)
    resident = lambda a: pl.BlockSpec(a.shape, lambda b, t: (0,) * a.ndim)
    kq, vq, rq = heads, (2 * hk) // dv, (2 * hk) // dv + heads

    def tok(col):
        def index(b, t):
            blk = b * nt + t
            return (blk // per_tile, blk % per_tile, col)
        return index

    def smp(col0):
        def index(b, t):
            blk = b * nt + t
            return (blk // heads, s_blk, col0 + blk % heads)
        return index

    state = lambda b, t: ((b * nt + t) // heads, (b * nt + t) % heads, 0, 0)
    on_arg, on_spec = _layer_vector(onorm, layer)
    return pl.pallas_call(
        functools.partial(_gla_body, heads=heads, nt=nt, per_tile=per_tile),
        grid=(batch, nt),
        in_specs=[
            pl.BlockSpec((1, tb, hk), tok(0)),
            pl.BlockSpec((1, tb, hk), tok(1)),
            pl.BlockSpec((1, tb, hv), tok(1)),
            pl.BlockSpec((1, tb, hv), tok(2)),
            pl.BlockSpec((1, tb, d), tok(0)),
            on_spec,
            *[resident(a) for a in gate_params],
            pl.BlockSpec((1, ts, dk), smp(0)),
            pl.BlockSpec((1, ts, dk), smp(kq)),
            pl.BlockSpec((1, ts, dv), smp(vq)),
            pl.BlockSpec((1, ts, dv), smp(rq)),
            pl.BlockSpec((1, ts, d), lambda b, t: ((b * nt + t) // heads, s_blk, 0)),
            _layer_operand(s0, layer, (ts, 1, dk, dv), state)[1],
        ],
        out_specs=[
            pl.BlockSpec((1, tm, hv), lambda b, t: ((b * nt + t) // per_tile, 0, 0)),
            pl.BlockSpec((1, heads, dk, dv), lambda b, t: (b, 0, 0, 0)),
            pl.BlockSpec((ts, 1, dk, dv), state),
        ],
        out_shape=[
            jax.ShapeDtypeStruct((N_TOKEN_TILES, tm, hv), BF16),
            jax.ShapeDtypeStruct((batch, heads, dk, dv), F32),
            jax.ShapeDtypeStruct(s0.shape[1:], F32),
        ],
        compiler_params=_cparams("arbitrary", "arbitrary"),
        name="gla",
    )(qkvr3, qkvr3, qkvr3, qkvr3, u3, on_arg, *gate_params, qkvr3, qkvr3, qkvr3, qkvr3, u3, s0)


def kernel(x_prompt, x_sample, state_rglru_h, state_rglru_conv, state_gla_S, ln_ffn1, ffn1_w1, ffn1_w3, ffn1_w2, ln_mix, ln_ffn2, ffn2_w1, ffn2_w3, ffn2_w2, rg_w_y, rg_w_x, rg_conv_w, rg_conv_b, rg_w_a, rg_b_a, rg_w_i, rg_b_i, rg_lambda, rg_w_o, gla_w_q, gla_w_k, gla_w_v, gla_w_g1, gla_w_g2, gla_b_g, gla_w_r, gla_onorm, gla_w_o, ln_final):
    batch, seq, d = x_prompt.shape
    n_dec = x_sample.shape[0]
    depth = ln_ffn1.shape[0]
    heads, dk, dv = state_gla_S.shape[2:]
    tp, ts = _token_tiles(batch * seq, n_dec)
    tm = tp + ts
    tiles = dict(tp=tp, ts=ts)

    x = (x_prompt.reshape(batch * seq, d), x_sample.reshape(n_dec, d))

    ffn_f32 = []
    for i in range(depth):
        ffn_f32 += [((ffn1_w1, ffn1_w3, ffn1_w2), i), ((ffn2_w1, ffn2_w3, ffn2_w2), i)]
    ffn_f32.append(None)
    ffn_w = tuple(_to_bf16(w, 0) for w in ffn_f32[0][0])

    p_h, p_conv, p_s, s_h, s_conv, s_s = [], [], [], [], [], []
    for i in range(depth):
        j = i // 2
        x, u, ffn_w = _ffn(x, i, ln_ffn1, ffn_w, g_mix=ln_mix, next_weights=ffn_f32[2 * i + 1], **tiles)
        if i % 2 == 0:
            proj = _matmul(u, j, [rg_w_y, rg_w_x], [_gelu_tanh, _identity], tm=tm)
            params = _rg_params(j, rg_conv_w, rg_conv_b, rg_w_a, rg_b_a, rg_w_i, rg_b_i, rg_lambda)
            mix, hl, cl = _rg_prompt(proj, params, batch=batch, seq=seq, **tiles)
            mix, hn, cn = _rg_sample(mix, proj, j, state_rglru_conv, state_rglru_h, params, **tiles)
            p_h.append(hl.reshape(batch, d))
            p_conv.append(cl)
            s_h.append(hn)
            s_conv.append(cn.reshape(n_dec, CONV_W - 1, d))
            w_o = rg_w_o
        else:
            qkvr = _matmul(u, j, [gla_w_q, gla_w_k, gla_w_v, gla_w_r],
                           [_identity, _identity, _identity, _silu], tm=tm, out_dtype=BF16)
            rank = gla_w_g1.shape[2]
            gate_params = (jnp.pad(gla_w_g1[j], ((0, 0), (0, LANES - rank))).astype(BF16),
                           jnp.pad(gla_w_g2[j], ((0, LANES - rank), (0, 0))).astype(BF16),
                           gla_b_g[j].reshape(1, heads * dk))
            mix, s_fin, s_new = _gla(qkvr, u, j, gate_params, state_gla_S, gla_onorm, batch=batch,
                                     seq=seq, heads=heads, dk=dk, dv=dv, **tiles)
            p_s.append(s_fin)
            s_s.append(s_new)
            w_o = gla_w_o
        x = _matmul(mix.reshape(N_TOKEN_TILES * tm, d), j, [w_o], [_identity], res=x, tm=tm)
        g_final = ln_final.reshape(1, d) if i == depth - 1 else None
        x, _, ffn_w = _ffn(x, i, ln_ffn2, ffn_w, g_final=g_final, next_weights=ffn_f32[2 * i + 2],
                           **tiles)

    y_prompt = x[0].reshape(batch, seq, d)
    y_sample = x[1].reshape(n_dec, 1, d)
    return (y_prompt, y_sample, jnp.stack(p_h), jnp.stack(p_conv), jnp.stack(p_s),
            jnp.stack(s_h), jnp.stack(s_conv), jnp.stack(s_s))
```

```python
import functools

import jax
import jax.numpy as jnp
from jax import lax
from jax.experimental import pallas as pl
from jax.experimental.pallas import tpu as pltpu

F32 = jnp.float32
BF16 = jnp.bfloat16

RMS_EPS = 1e-6
RG_C = 8.0
CONV_W = 4
GLA_GATE_NORM = 16.0
GLA_CHUNK = 64

V7X_VMEM_LIMIT_BYTES = 56 * 1024 * 1024
V7X_VMEM_LIMIT_LARGE_BYTES = 127 * 512 * 1024
SUBLANES = 8
LANES = 128
BF16_SUBLANES = 16
N_TOKEN_TILES = 8


def _cparams(*sem, vmem_limit_bytes=V7X_VMEM_LIMIT_BYTES):
    return pltpu.CompilerParams(dimension_semantics=sem, vmem_limit_bytes=vmem_limit_bytes)


def _rms_scale(x):
    return x * lax.rsqrt(jnp.mean(x * x, axis=-1, keepdims=True) + RMS_EPS)


def _silu(x):
    return x * jax.nn.sigmoid(x)


def _gelu_tanh(y):
    return jax.nn.gelu(y, approximate=True)


def _identity(y):
    return y


def _softplus(x):
    return jnp.maximum(x, 0.0) + jnp.log1p(jnp.exp(-jnp.abs(x)))


FFN_FIRST_PIECE = 512
FFN_LAST_PIECE = 512


def _layer_operand(stacked, layer, block, index_map):
    spec = pl.BlockSpec((None,) + tuple(block), lambda *g: (layer,) + tuple(index_map(*g)))
    return stacked, spec


def _layer_vector(stacked, layer):
    n_layers, n = stacked.shape
    return _layer_operand(stacked.reshape(n_layers, 1, n), layer, (1, n), lambda *g: (0, 0))


def _layer_full(stacked, layer):
    block = stacked.shape[1:]
    return _layer_operand(stacked, layer, block, lambda *g: (0,) * len(block))


def _token_tiles(n_prompt, n_sample):
    tp, ts = n_prompt // N_TOKEN_TILES, n_sample // N_TOKEN_TILES
    assert tp * N_TOKEN_TILES == n_prompt and ts * N_TOKEN_TILES == n_sample
    assert tp % BF16_SUBLANES == 0 and ts % BF16_SUBLANES == 0 and tp % ts == 0
    return tp, ts


def _cast_body(w_ref, o_ref):
    o_ref[...] = w_ref[...].astype(o_ref.dtype)


def _to_bf16(w, layer, *, rows=512):
    _, r, c = w.shape
    return pl.pallas_call(
        _cast_body,
        grid=(r // rows,),
        in_specs=[pl.BlockSpec((None, rows, c), lambda i: (layer, i, 0))],
        out_specs=pl.BlockSpec((rows, c), lambda i: (i, 0)),
        out_shape=jax.ShapeDtypeStruct((r, c), BF16),
        compiler_params=_cparams("arbitrary"),
        name="to_bf16",
    )(w)


def _ffn_body(*refs, tp, ts, split_in, split_out, mix_norm, cast_next):
    refs = list(refs)
    x_refs = [refs.pop(0) for _ in range(2 if split_in else 1)]
    g_ref, w1_ref, w3_ref, w2_ref = (refs.pop(0) for _ in range(4))
    g2_ref = refs.pop(0) if (split_out or mix_norm) else None
    next_f32 = [refs.pop(0) for _ in range(3 if cast_next else 0)]
    o_refs = [refs.pop(0) for _ in range(2 if split_out else 1)]
    u_hbm = refs.pop(0) if mix_norm else None
    next_bf16 = [refs.pop(0) for _ in range(3 if cast_next else 0)]
    xn_ref = refs.pop(0)
    acc_ref = refs.pop(0) if split_out else o_refs[0]
    u_ref = refs.pop(0) if mix_norm else None
    (sem,) = refs
    tm = tp + ts
    i = pl.program_id(0)
    f = pl.program_id(1)

    n_f = pl.num_programs(1)

    def segments(blocks):
        if len(blocks) == 2:
            return [(0, tp, blocks[0]), (tp, ts, blocks[1])]
        return [(0, tm, blocks[0])]

    def parts(blocks, r0, r1):
        out = []
        for t0, n_rows, ref in segments(blocks):
            lo, hi = max(r0, t0), min(r1, t0 + n_rows)
            if lo < hi:
                out.append((ref, slice(lo - t0, hi - t0)))
        return out

    def read_rows(blocks, r0, r1):
        vals = [ref[rows, :] for ref, rows in parts(blocks, r0, r1)]
        return vals[0] if len(vals) == 1 else jnp.concatenate(vals, axis=0)

    def write_rows(blocks, r0, r1, value):
        off = 0
        for ref, rows in parts(blocks, r0, r1):
            n = rows.stop - rows.start
            ref[rows, :] = value[off:off + n]
            off += n

    def ffn_product(xn):
        h1 = jnp.dot(xn, w1_ref[...].astype(BF16), preferred_element_type=F32)
        h3 = jnp.dot(xn, w3_ref[...].astype(BF16), preferred_element_type=F32)
        h = (_silu(h1) * h3).astype(BF16)
        return jnp.dot(h, w2_ref[...].astype(BF16), preferred_element_type=F32)

    assert 0 < FFN_FIRST_PIECE < tm and 0 < FFN_LAST_PIECE < tm
    assert FFN_FIRST_PIECE % BF16_SUBLANES == 0 and FFN_LAST_PIECE % BF16_SUBLANES == 0
    pieces = [(0, FFN_FIRST_PIECE), (FFN_FIRST_PIECE, tm)]
    pieces_rev = [(0, tm - FFN_LAST_PIECE), (tm - FFN_LAST_PIECE, tm)]

    def cast_next_block():
        for src_ref, dst_ref in zip(next_f32, next_bf16):
            dst_ref[...] = src_ref[...].astype(BF16)

    @pl.when(f == 0)
    def _():
        for r0, r1 in pieces:
            x = read_rows(x_refs, r0, r1)
            xn = (_rms_scale(x) * g_ref[...]).astype(BF16)
            xn_ref[r0:r1, :] = xn
            acc_ref[r0:r1, :] = 2.0 * x + ffn_product(xn)
        cast_next_block()

    @pl.when((f > 0) & (f < n_f - 1))
    def _():
        acc_ref[...] += ffn_product(xn_ref[...])
        cast_next_block()

    @pl.when(f == n_f - 1)
    def _():
        u_copy = lambda tile: pltpu.make_async_copy(u_ref, u_hbm.at[pl.ds(tile * tm, tm)], sem.at[0])
        if mix_norm:
            pl.when(i > 0)(lambda: u_copy(i - 1).wait())
        cast_next_block()
        for r0, r1 in pieces_rev:
            y = 0.5 * (acc_ref[r0:r1, :] + ffn_product(xn_ref[r0:r1, :]))
            if split_out:
                y = _rms_scale(y) * g2_ref[...]
            write_rows(o_refs, r0, r1, y)
            if mix_norm:
                u_ref[r0:r1, :] = (_rms_scale(y) * g2_ref[...]).astype(BF16)
        if mix_norm:
            u_copy(i).start()
            pl.when(i == pl.num_programs(0) - 1)(lambda: u_copy(i).wait())


def _ffn(x, layer, g, weights, *, tp, ts, g_mix=None, g_final=None, next_weights=None, tf=512):
    split_in = isinstance(x, tuple)
    split_out = g_final is not None
    mix_norm = g_mix is not None
    cast_next = next_weights is not None
    assert not (split_out and mix_norm)
    xs = list(x) if split_in else [x]
    w1, w3, w2 = weights
    d, f = w1.shape
    tm = tp + ts
    nf = f // tf
    operands = [
        _layer_vector(g, layer),
        (w1, pl.BlockSpec((d, tf), lambda i, j: (0, j))),
        (w3, pl.BlockSpec((d, tf), lambda i, j: (0, j))),
        (w2, pl.BlockSpec((tf, d), lambda i, j: (j, 0))),
    ]
    if mix_norm:
        operands.append(_layer_vector(g_mix, layer))
    if split_out:
        operands.append(_layer_vector(g_final, 0))
    row_tile = lambda rows: pl.BlockSpec((rows, d), lambda i, j: (i, 0))
    x_specs = [row_tile(tp), row_tile(ts)] if split_in else [row_tile(tm)]
    if split_out:
        out_specs = [row_tile(tp), row_tile(ts)]
        out_shape = [jax.ShapeDtypeStruct((tp * N_TOKEN_TILES, d), F32),
                     jax.ShapeDtypeStruct((ts * N_TOKEN_TILES, d), F32)]
    else:
        out_specs = [row_tile(tm)]
        out_shape = [jax.ShapeDtypeStruct((tm * N_TOKEN_TILES, d), F32)]
    if mix_norm:
        out_specs.append(pl.BlockSpec(memory_space=pl.ANY))
        out_shape.append(jax.ShapeDtypeStruct((tm * N_TOKEN_TILES, d), BF16))
    if cast_next:
        (n1, n3, n2), next_layer = next_weights
        db = d // N_TOKEN_TILES
        assert n1.shape[1:] == (d, f) and n2.shape[1:] == (f, d) and db * N_TOKEN_TILES == d
        operands += [
            _layer_operand(n1, next_layer, (db, tf), lambda i, j: (i, j)),
            _layer_operand(n3, next_layer, (db, tf), lambda i, j: (i, j)),
            _layer_operand(n2, next_layer, (tf, db), lambda i, j: (j, i)),
        ]
        out_specs += [pl.BlockSpec((db, tf), lambda i, j: (i, j)), pl.BlockSpec((db, tf), lambda i, j: (i, j)),
                      pl.BlockSpec((tf, db), lambda i, j: (j, i))]
        out_shape += [jax.ShapeDtypeStruct((d, f), BF16), jax.ShapeDtypeStruct((d, f), BF16),
                      jax.ShapeDtypeStruct((f, d), BF16)]
    out = pl.pallas_call(
        functools.partial(_ffn_body, tp=tp, ts=ts, split_in=split_in, split_out=split_out,
                          mix_norm=mix_norm, cast_next=cast_next),
        grid=(N_TOKEN_TILES, nf),
        in_specs=x_specs + [spec for _, spec in operands],
        out_specs=out_specs,
        out_shape=out_shape,
        scratch_shapes=[pltpu.VMEM((tm, d), BF16)] + ([pltpu.VMEM((tm, d), F32)] if split_out else [])
        + ([pltpu.VMEM((tm, d), BF16)] if mix_norm else []) + [pltpu.SemaphoreType.DMA((1,))],
        compiler_params=_cparams("arbitrary", "arbitrary", vmem_limit_bytes=V7X_VMEM_LIMIT_LARGE_BYTES),
        name="ffn",
    )(*xs, *[arr for arr, _ in operands])
    out = list(out)
    acts = tuple(out[:2]) if split_out else out[0]
    rest = out[2:] if split_out else out[1:]
    u = rest.pop(0) if mix_norm else None
    return acts, u, (tuple(rest) if cast_next else None)


def _matmul_body(u_ref, *refs, tiles, layer, has_res):
    n_src = 1 + max(src for src, _, _ in tiles)
    w_hbm = refs[:n_src]
    rest = refs[n_src:]
    if has_res:
        r_ref, o_ref, stage_ref, wb_ref, sem = rest
    else:
        o_ref, stage_ref, wb_ref, sem = rest
    tn = stage_ref.shape[1]
    j = pl.program_id(0)
    i = pl.program_id(1)

    def weight_copy(jj):
        src, col0, _ = tiles[jj]
        return pltpu.make_async_copy(w_hbm[src].at[layer, :, pl.ds(col0, tn)], stage_ref, sem.at[0])

    def at_tile(jj, fn):
        pl.when(j == jj)(fn)

    @pl.when(i == 0)
    def _():
        at_tile(0, lambda: weight_copy(0).start())
        for jj in range(len(tiles)):
            at_tile(jj, lambda jj=jj: weight_copy(jj).wait())
        wb_ref[...] = stage_ref[...].astype(BF16)
        for jj in range(len(tiles) - 1):
            at_tile(jj, lambda jj=jj: weight_copy(jj + 1).start())

    for act in dict.fromkeys(act for _, _, act in tiles):
        uses = functools.reduce(jnp.logical_or, [j == jj for jj, t in enumerate(tiles) if t[2] is act])

        @pl.when(uses)
        def _(act=act):
            y = act(jnp.dot(u_ref[...], wb_ref[...], preferred_element_type=F32))
            if has_res:
                y = r_ref[...] + y
            o_ref[...] = y.astype(o_ref.dtype)


def _matmul(u, layer, ws, acts, res=None, *, tm, tn=1024, out_dtype=F32):
    m, d = u.shape
    tiles = []
    for src, (w, act) in enumerate(zip(ws, acts)):
        assert w.shape[2] % tn == 0
        tiles += [(src, col0, act) for col0 in range(0, w.shape[2], tn)]
    n = len(tiles) * tn
    io_spec = lambda: pl.BlockSpec((tm, tn), lambda j, i: (i, j))
    has_res = res is not None
    return pl.pallas_call(
        functools.partial(_matmul_body, tiles=tuple(tiles), layer=layer, has_res=has_res),
        grid=(len(tiles), m // tm),
        in_specs=[pl.BlockSpec((tm, d), lambda j, i: (i, 0))]
        + [pl.BlockSpec(memory_space=pl.ANY)] * len(ws) + ([io_spec()] if has_res else []),
        out_specs=io_spec(),
        out_shape=jax.ShapeDtypeStruct((m, n), out_dtype),
        scratch_shapes=[pltpu.VMEM((d, tn), F32), pltpu.VMEM((d, tn), BF16),
                        pltpu.SemaphoreType.DMA((1,))],
        compiler_params=_cparams("arbitrary", "arbitrary"),
        name="matmul",
    )(u, *ws, *([res] if has_res else []))


def _prompt_rows_of_tile(o_ref, nt, per_tile, tb):
    blk = pl.program_id(0) * nt + pl.program_id(1)
    k = blk % per_tile
    tp = per_tile * tb

    @pl.when(k == 0)
    def _():
        o_ref[0, tp:, :] = jnp.zeros((o_ref.shape[1] - tp, o_ref.shape[2]), o_ref.dtype)

    return pl.ds(pl.multiple_of(k * tb, tb), tb)


def _sigmoid(x):
    return 0.5 * jnp.tanh(0.5 * x) + 0.5


def _rglru_coeffs(conv_out, wa_ref, ba_ref, wi_ref, bi_ref, lam_ref, store):
    n_blocks, bw, _ = wa_ref.shape
    for n in range(n_blocks):
        cols = slice(n * bw, (n + 1) * bw)
        xb = conv_out(cols)
        xb16 = xb.astype(BF16)
        wa = wa_ref[n].astype(BF16)
        wi = wi_ref[n].astype(BF16)
        r = _sigmoid(jnp.dot(xb16, wa, preferred_element_type=F32) + ba_ref[:, cols])
        i = _sigmoid(jnp.dot(xb16, wi, preferred_element_type=F32) + bi_ref[:, cols])
        log_a = -RG_C * r * _softplus(-lam_ref[:, cols])
        a = jnp.exp(log_a)
        b = jnp.sqrt(1.0 - a * a) * (i * xb)
        store(n, a, b)


def _rg_prompt_body(xw_ref, gate_ref, cw_ref, cb_ref, wa_ref, ba_ref, wi_ref, bi_ref, lam_ref,
                    o_ref, hl_ref, cl_ref, ext_ref, a_ref, b_ref, h_ref, *, nt, per_tile):
    t = pl.program_id(1)
    tb = xw_ref.shape[1]
    bw = wa_ref.shape[1]
    pad = SUBLANES

    @pl.when(t == 0)
    def _():
        ext_ref[0:pad, :] = jnp.zeros((pad, ext_ref.shape[1]), F32)
        h_ref[...] = jnp.zeros_like(h_ref)

    ext_ref[pad:pad + tb, :] = xw_ref[0]

    def conv_out(cols):
        ext = ext_ref[:, cols]
        y = ext * cw_ref[0:1, cols]
        for j in range(1, CONV_W):
            y = ext * cw_ref[j:j + 1, cols] + pltpu.roll(y, 1, axis=0)
        return cb_ref[:, cols] + y[pad:, :]

    seg = tb // SUBLANES
    pitch = a_ref.shape[1] // SUBLANES
    n_slabs = a_ref.shape[0]

    def store(n, a, b):
        for half in range(bw // LANES):
            lanes = slice(half * LANES, (half + 1) * LANES)
            for s in range(SUBLANES):
                dst = slice(s * pitch, s * pitch + seg)
                a_ref[n * (bw // LANES) + half, dst, :] = a[s * seg:(s + 1) * seg, lanes]
                b_ref[n * (bw // LANES) + half, dst, :] = b[s * seg:(s + 1) * seg, lanes]

    _rglru_coeffs(conv_out, wa_ref, ba_ref, wi_ref, bi_ref, lam_ref, store)

    def step(v, carry):
        hs, ps = carry
        rows = pl.ds(v, SUBLANES, stride=pitch)
        new_h, new_p = [], []
        for slab in range(n_slabs):
            a = a_ref[slab, rows, :]
            h = a * hs[slab] + b_ref[slab, rows, :]
            p = a * ps[slab]
            b_ref[slab, rows, :] = h
            a_ref[slab, rows, :] = p
            new_h.append(h)
            new_p.append(p)
        return tuple(new_h), tuple(new_p)

    zeros = tuple(jnp.zeros((SUBLANES, LANES), F32) for _ in range(n_slabs))
    ones = tuple(jnp.ones((SUBLANES, LANES), F32) for _ in range(n_slabs))
    lax.fori_loop(0, seg, step, (zeros, ones))

    out_rows = _prompt_rows_of_tile(o_ref, nt, per_tile, tb)
    for slab in range(n_slabs):
        lanes = slice(slab * LANES, (slab + 1) * LANES)
        h_in = h_ref[0:1, lanes]
        for s in range(SUBLANES):
            src = slice(s * pitch, s * pitch + seg)
            h = b_ref[slab, src, :] + a_ref[slab, src, :] * h_in
            dst = pl.ds(pl.multiple_of(out_rows.start + s * seg, seg), seg)
            o_ref[0, dst, lanes] = (h * gate_ref[0, s * seg:(s + 1) * seg, lanes]).astype(o_ref.dtype)
            h_in = h[seg - 1:seg, :]
        h_ref[0:1, lanes] = h_in
        hl_ref[0, :, lanes] = h_in
    cl_ref[0] = ext_ref[pad + tb - (CONV_W - 1):pad + tb, :]
    ext_ref[0:pad, :] = ext_ref[tb:tb + pad, :]


def _segment_pitch(seg):
    pitch = -(-seg // SUBLANES) * SUBLANES
    while (pitch // SUBLANES) % 2 == 0:
        pitch += SUBLANES
    return pitch


def _rg_params(layer, conv_w, conv_b, w_a, b_a, w_i, b_i, lam):
    operands = [_layer_full(conv_w, layer), _layer_vector(conv_b, layer), _layer_full(w_a, layer),
                _layer_vector(b_a, layer), _layer_full(w_i, layer), _layer_vector(b_i, layer),
                _layer_vector(lam, layer)]
    return [arr for arr, _ in operands], [spec for _, spec in operands]


def _rg_prompt(proj, params, *, batch, seq, tp, ts, tb=256):
    p_args, p_specs = params
    m, d2 = proj.shape
    d = d2 // 2
    tm = tp + ts
    nt = seq // tb
    per_tile = tp // tb
    proj3 = proj.reshape(N_TOKEN_TILES, tm, d2)

    def tok(col):
        def index(b, t):
            blk = b * nt + t
            return (blk // per_tile, blk % per_tile, col)
        return index

    mix, hl, cl = pl.pallas_call(
        functools.partial(_rg_prompt_body, nt=nt, per_tile=per_tile),
        grid=(batch, nt),
        in_specs=[pl.BlockSpec((1, tb, d), tok(1)), pl.BlockSpec((1, tb, d), tok(0))] + p_specs,
        out_specs=[
            pl.BlockSpec((1, tm, d), lambda b, t: ((b * nt + t) // per_tile, 0, 0)),
            pl.BlockSpec((1, 1, d), lambda b, t: (b, 0, 0)),
            pl.BlockSpec((1, CONV_W - 1, d), lambda b, t: (b, 0, 0)),
        ],
        out_shape=[
            jax.ShapeDtypeStruct((N_TOKEN_TILES, tm, d), BF16),
            jax.ShapeDtypeStruct((batch, 1, d), F32),
            jax.ShapeDtypeStruct((batch, CONV_W - 1, d), F32),
        ],
        scratch_shapes=[
            pltpu.VMEM((tb + SUBLANES, d), F32),
            pltpu.VMEM((d // LANES, SUBLANES * _segment_pitch(tb // SUBLANES), LANES), F32),
            pltpu.VMEM((d // LANES, SUBLANES * _segment_pitch(tb // SUBLANES), LANES), F32),
            pltpu.VMEM((SUBLANES, d), F32),
        ],
        compiler_params=_cparams("arbitrary", "arbitrary"),
        name="rg_prompt",
    )(proj3, proj3, *p_args)
    return mix, hl, cl


def _rg_sample_body(mix_ref, xw_ref, gate_ref, conv_ref, h0_ref, cw_ref, cb_ref, wa_ref, ba_ref,
                    wi_ref, bi_ref, lam_ref, o_ref, hn_ref, cn_ref, a_ref, b_ref):
    del mix_ref
    n, d = h0_ref.shape
    bw = wa_ref.shape[1]
    x = xw_ref[...].reshape(n, d)
    xc = cb_ref[...]
    for j in range(CONV_W - 1):
        xc = xc + conv_ref[:, j * d:(j + 1) * d] * cw_ref[j:j + 1, :]
    xc = xc + x * cw_ref[CONV_W - 1:CONV_W, :]

    def store(k, a, b):
        a_ref[:, k * bw:(k + 1) * bw] = a
        b_ref[:, k * bw:(k + 1) * bw] = b

    _rglru_coeffs(lambda cols: xc[:, cols], wa_ref, ba_ref, wi_ref, bi_ref, lam_ref, store)
    h = b_ref[...] + a_ref[...] * h0_ref[...]
    hn_ref[...] = h
    o_ref[...] = (h * gate_ref[...].reshape(n, d)).reshape(o_ref.shape).astype(o_ref.dtype)
    cn_ref[:, 0:(CONV_W - 2) * d] = conv_ref[:, d:(CONV_W - 1) * d]
    cn_ref[:, (CONV_W - 2) * d:] = x


def _rg_sample(mix, proj, layer, conv_state, h0, params, *, tp, ts):
    p_args, p_specs = params
    n_layers, n, d = h0.shape
    tm = tp + ts
    proj3 = proj.reshape(N_TOKEN_TILES, tm, 2 * d)
    s_blk = tp // ts
    full2 = lambda shape: pl.BlockSpec(shape, lambda i: (0, 0))
    kc = (CONV_W - 1) * d
    conv_arg, conv_spec = _layer_full(conv_state.reshape(n_layers, n, kc), layer)
    h0_arg, h0_spec = _layer_full(h0, layer)
    return pl.pallas_call(
        _rg_sample_body,
        grid=(1,),
        in_specs=[
            pl.BlockSpec(memory_space=pl.ANY),
            pl.BlockSpec((N_TOKEN_TILES, ts, d), lambda i: (0, s_blk, 1)),
            pl.BlockSpec((N_TOKEN_TILES, ts, d), lambda i: (0, s_blk, 0)),
            conv_spec, h0_spec,
        ] + p_specs,
        out_specs=[
            pl.BlockSpec((N_TOKEN_TILES, ts, d), lambda i: (0, s_blk, 0)),
            full2((n, d)),
            full2((n, kc)),
        ],
        out_shape=[
            jax.ShapeDtypeStruct(mix.shape, mix.dtype),
            jax.ShapeDtypeStruct((n, d), F32),
            jax.ShapeDtypeStruct((n, kc), F32),
        ],
        scratch_shapes=[pltpu.VMEM((n, d), F32), pltpu.VMEM((n, d), F32)],
        input_output_aliases={0: 0},
        compiler_params=_cparams("arbitrary"),
        name="rg_sample",
    )(mix, proj3, proj3, conv_arg, h0_arg, *p_args)


def _gla_log_decay(u, w1_ref, w2_ref, b_ref):
    t = jnp.dot(u, w1_ref[...].astype(BF16), preferred_element_type=F32)
    logit = jnp.dot(t.astype(BF16), w2_ref[...].astype(BF16), preferred_element_type=F32) + b_ref[...]
    log_sigmoid = jnp.minimum(logit, 0.0) - jnp.log(1.0 + jnp.exp(-jnp.abs(logit)))
    return log_sigmoid * (1.0 / GLA_GATE_NORM)


def _gla_sample_update(q_ref, k_ref, v_ref, r_ref, g, s0_ref, on_ref, s_ref):
    bb, dk = q_ref.shape[1:]
    alpha_t = jnp.exp(g).T
    k_t = k_ref[0].astype(F32).T
    q_t = (q_ref[0].astype(F32) * dk ** -0.5).T
    v = v_ref[0].astype(F32)
    rows = []
    for b in range(bb):
        s_new = alpha_t[:, b:b + 1] * s0_ref[b, 0] + k_t[:, b:b + 1] * v[b:b + 1, :]
        s_ref[b, 0] = s_new
        rows.append(jnp.sum(q_t[:, b:b + 1] * s_new, axis=0, keepdims=True))
    o = jnp.concatenate(rows, axis=0)
    o = _rms_scale(o) * on_ref[...]
    return o * r_ref[0].astype(F32)


def _gla_body(q_ref, k_ref, v_ref, r_ref, u_ref, on_ref, wg1_ref, wg2_ref, bg_ref,
              sq_ref, sk_ref, sv_ref, sr_ref, su_ref, s0_ref, o_ref, s_ref, ss_ref, *,
              heads, nt, per_tile):
    t = pl.program_id(1)
    tb = q_ref.shape[1]
    tp = per_tile * tb
    c = GLA_CHUNK
    dk = q_ref.shape[2] // heads
    dv = v_ref.shape[2] // heads

    @pl.when(t == 0)
    def _():
        s_ref[...] = jnp.zeros_like(s_ref)

    blk = pl.program_id(0) * nt + t
    head_s = blk % heads
    g_s_all = _gla_log_decay(su_ref[0], wg1_ref, wg2_ref, bg_ref)
    g_s = jnp.zeros((g_s_all.shape[0], dk), F32)
    for h in range(heads):
        g_s = jnp.where(head_s == h, g_s_all[:, h * dk:(h + 1) * dk], g_s)
    o_s = _gla_sample_update(sq_ref, sk_ref, sv_ref, sr_ref, g_s, s0_ref, on_ref, ss_ref)
    o_s = o_s.astype(o_ref.dtype)

    row = lax.broadcasted_iota(jnp.int32, (tb, tb), 0)
    col = lax.broadcasted_iota(jnp.int32, (tb, tb), 1)
    same_chunk_causal = ((row >= col) & ((row ^ col) < c)).astype(BF16)
    g_rest = _gla_log_decay(u_ref[0], wg1_ref, wg2_ref, bg_ref)
    bcum = jnp.zeros(g_rest.shape, F32)
    for _ in range(3):
        g_part = g_rest.astype(BF16)
        bcum = bcum + jnp.dot(same_chunk_causal, g_part, preferred_element_type=F32)
        g_rest = g_rest - g_part.astype(F32)
    nc = tb // c
    chunk_rows = [slice(ci * c, (ci + 1) * c) for ci in range(nc)]
    q = q_ref[0].astype(F32) * dk ** -0.5
    k = k_ref[0].astype(F32)
    q_i = q * jnp.exp(bcum)
    k_i = (k * jnp.exp(-bcum)).astype(BF16)
    g_tot = [bcum[(ci + 1) * c - 1:(ci + 1) * c, :] for ci in range(nc)]
    g_before = [jnp.zeros_like(g_tot[0])]
    for ci in range(nc):
        g_before.append(g_before[-1] + g_tot[ci])
    k_e = [k[rows, :] * jnp.exp(g_tot[ci] - bcum[rows, :]) for ci, rows in enumerate(chunk_rows)]
    q_in = jnp.concatenate([(q_i[rows, :] * jnp.exp(g_before[ci])).astype(BF16)
                            for ci, rows in enumerate(chunk_rows)], axis=0)
    k_out = jnp.concatenate([(k_e[ci] * jnp.exp(g_before[nc] - g_before[ci + 1])).astype(BF16)
                             for ci in range(nc)], axis=0)
    decay_t = jnp.exp(jnp.broadcast_to(g_before[nc], (SUBLANES, heads * dk))).T
    q_i = q_i.astype(BF16)

    out_base = (blk % per_tile) * tb
    o_inter = []
    for h in range(heads):
        kcols = slice(h * dk, (h + 1) * dk)
        vcols = slice(h * dv, (h + 1) * dv)
        s = s_ref[0, h]
        o_inter.append(jnp.dot(q_in[:, kcols], s.astype(BF16), preferred_element_type=F32))
        s_ref[0, h] = decay_t[kcols, 0:1] * s + lax.dot_general(
            k_out[:, kcols], v_ref[0, :, vcols], (((0,), (0,)), ((), ())), preferred_element_type=F32)

    for ci, rows in enumerate(chunk_rows):
        out_rows = pl.ds(pl.multiple_of(out_base + ci * c, c), c)
        n_keys = (ci + 1) * c
        keys = jnp.concatenate(
            [(k_e[cj] * jnp.exp(g_before[ci] - g_before[cj + 1])).astype(BF16) for cj in range(ci)]
            + [k_i[rows, :]], axis=0)
        visible = (lax.broadcasted_iota(jnp.int32, (c, n_keys), 1)
                   <= lax.broadcasted_iota(jnp.int32, (c, n_keys), 0) + ci * c)
        for h in range(heads):
            kcols = slice(h * dk, (h + 1) * dk)
            vcols = slice(h * dv, (h + 1) * dv)
            scores = lax.dot_general(q_i[rows, kcols], keys[:, kcols], (((1,), (1,)), ((), ())),
                                     preferred_element_type=F32)
            scores = jnp.where(visible, scores, 0.0).astype(BF16)
            o = o_inter[h][rows, :] + jnp.dot(scores, v_ref[0, 0:n_keys, vcols],
                                              preferred_element_type=F32)
            o = _rms_scale(o) * on_ref[...]
            o_ref[0, out_rows, vcols] = (o * r_ref[0, rows, vcols].astype(F32)).astype(o_ref.dtype)

    for h in range(heads):
        @pl.when(head_s == h)
        def _(h=h):
            o_ref[0, tp:, h * dv:(h + 1) * dv] = o_s


def _gla(qkvr, u, layer, gate_params, s0, onorm, *, batch, seq, heads, dk, dv, tp, ts, tb=256):
    tm = tp + ts
    nt = seq // tb
    per_tile = tp // tb
    s_blk = tp // ts
    hk, hv = heads * dk, heads * dv
    assert hv == 2 * hk
    assert batch * nt == N_TOKEN_TILES * heads and per_tile == heads
    qkvr3 = qkvr.reshape(N_TOKEN_TILES, tm, qkvr.shape[1])
    d = u.shape[1]
    u3 = u.reshape(N_TOKEN_TILES, tm, d)
    resident = lambda a: pl.BlockSpec(a.shape, lambda b, t: (0,) * a.ndim)
    kq, vq, rq = heads, (2 * hk) // dv, (2 * hk) // dv + heads

    def tok(col):
        def index(b, t):
            blk = b * nt + t
            return (blk // per_tile, blk % per_tile, col)
        return index

    def smp(col0):
        def index(b, t):
            blk = b * nt + t
            return (blk // heads, s_blk, col0 + blk % heads)
        return index

    state = lambda b, t: ((b * nt + t) // heads, (b * nt + t) % heads, 0, 0)
    on_arg, on_spec = _layer_vector(onorm, layer)
    return pl.pallas_call(
        functools.partial(_gla_body, heads=heads, nt=nt, per_tile=per_tile),
        grid=(batch, nt),
        in_specs=[
            pl.BlockSpec((1, tb, hk), tok(0)),
            pl.BlockSpec((1, tb, hk), tok(1)),
            pl.BlockSpec((1, tb, hv), tok(1)),
            pl.BlockSpec((1, tb, hv), tok(2)),
            pl.BlockSpec((1, tb, d), tok(0)),
            on_spec,
            *[resident(a) for a in gate_params],
            pl.BlockSpec((1, ts, dk), smp(0)),
            pl.BlockSpec((1, ts, dk), smp(kq)),
            pl.BlockSpec((1, ts, dv), smp(vq)),
            pl.BlockSpec((1, ts, dv), smp(rq)),
            pl.BlockSpec((1, ts, d), lambda b, t: ((b * nt + t) // heads, s_blk, 0)),
            _layer_operand(s0, layer, (ts, 1, dk, dv), state)[1],
        ],
        out_specs=[
            pl.BlockSpec((1, tm, hv), lambda b, t: ((b * nt + t) // per_tile, 0, 0)),
            pl.BlockSpec((1, heads, dk, dv), lambda b, t: (b, 0, 0, 0)),
            pl.BlockSpec((ts, 1, dk, dv), state),
        ],
        out_shape=[
            jax.ShapeDtypeStruct((N_TOKEN_TILES, tm, hv), BF16),
            jax.ShapeDtypeStruct((batch, heads, dk, dv), F32),
            jax.ShapeDtypeStruct(s0.shape[1:], F32),
        ],
        compiler_params=_cparams("arbitrary", "arbitrary", vmem_limit_bytes=V7X_VMEM_LIMIT_LARGE_BYTES),
        name="gla",
    )(qkvr3, qkvr3, qkvr3, qkvr3, u3, on_arg, *gate_params, qkvr3, qkvr3, qkvr3, qkvr3, u3, s0)


def kernel(x_prompt, x_sample, state_rglru_h, state_rglru_conv, state_gla_S, ln_ffn1, ffn1_w1, ffn1_w3, ffn1_w2, ln_mix, ln_ffn2, ffn2_w1, ffn2_w3, ffn2_w2, rg_w_y, rg_w_x, rg_conv_w, rg_conv_b, rg_w_a, rg_b_a, rg_w_i, rg_b_i, rg_lambda, rg_w_o, gla_w_q, gla_w_k, gla_w_v, gla_w_g1, gla_w_g2, gla_b_g, gla_w_r, gla_onorm, gla_w_o, ln_final):
    batch, seq, d = x_prompt.shape
    n_dec = x_sample.shape[0]
    depth = ln_ffn1.shape[0]
    heads, dk, dv = state_gla_S.shape[2:]
    tp, ts = _token_tiles(batch * seq, n_dec)
    tm = tp + ts
    tiles = dict(tp=tp, ts=ts)

    x = (x_prompt.reshape(batch * seq, d), x_sample.reshape(n_dec, d))

    ffn_f32 = []
    for i in range(depth):
        ffn_f32 += [((ffn1_w1, ffn1_w3, ffn1_w2), i), ((ffn2_w1, ffn2_w3, ffn2_w2), i)]
    ffn_f32.append(None)
    ffn_w = tuple(_to_bf16(w, 0) for w in ffn_f32[0][0])

    p_h, p_conv, p_s, s_h, s_conv, s_s = [], [], [], [], [], []
    for i in range(depth):
        j = i // 2
        x, u, ffn_w = _ffn(x, i, ln_ffn1, ffn_w, g_mix=ln_mix, next_weights=ffn_f32[2 * i + 1], **tiles)
        if i % 2 == 0:
            proj = _matmul(u, j, [rg_w_y, rg_w_x], [_gelu_tanh, _identity], tm=tm)
            params = _rg_params(j, rg_conv_w, rg_conv_b, rg_w_a, rg_b_a, rg_w_i, rg_b_i, rg_lambda)
            mix, hl, cl = _rg_prompt(proj, params, batch=batch, seq=seq, **tiles)
            mix, hn, cn = _rg_sample(mix, proj, j, state_rglru_conv, state_rglru_h, params, **tiles)
            p_h.append(hl.reshape(batch, d))
            p_conv.append(cl)
            s_h.append(hn)
            s_conv.append(cn.reshape(n_dec, CONV_W - 1, d))
            w_o = rg_w_o
        else:
            qkvr = _matmul(u, j, [gla_w_q, gla_w_k, gla_w_v, gla_w_r],
                           [_identity, _identity, _identity, _silu], tm=tm, out_dtype=BF16)
            rank = gla_w_g1.shape[2]
            gate_params = (jnp.pad(gla_w_g1[j], ((0, 0), (0, LANES - rank))).astype(BF16),
                           jnp.pad(gla_w_g2[j], ((0, LANES - rank), (0, 0))).astype(BF16),
                           gla_b_g[j].reshape(1, heads * dk))
            mix, s_fin, s_new = _gla(qkvr, u, j, gate_params, state_gla_S, gla_onorm, batch=batch,
                                     seq=seq, heads=heads, dk=dk, dv=dv, **tiles)
            p_s.append(s_fin)
            s_s.append(s_new)
            w_o = gla_w_o
        x = _matmul(mix.reshape(N_TOKEN_TILES * tm, d), j, [w_o], [_identity], res=x, tm=tm)
        g_final = ln_final.reshape(1, d) if i == depth - 1 else None
        x, _, ffn_w = _ffn(x, i, ln_ffn2, ffn_w, g_final=g_final, next_weights=ffn_f32[2 * i + 2],
                           **tiles)

    y_prompt = x[0].reshape(batch, seq, d)
    y_sample = x[1].reshape(n_dec, 1, d)
    return (y_prompt, y_sample, jnp.stack(p_h), jnp.stack(p_conv), jnp.stack(p_s),
            jnp.stack(s_h), jnp.stack(s_conv), jnp.stack(s_s))
```
